```python
import jax
import jax.numpy as jnp
from jax import lax
import numpy as np

D_MODEL = 1024
BATCH = 2
SEQ = 8192
DEPTH = 2

HEAD_DIM = 64
H_NSA = 4
H_DIL = 6
H_SB = 6
DIL_PAIRS = ((128, 1), (512, 4), (2048, 16))
N_DIL_GROUPS = len(DIL_PAIRS)
H_PER_DIL = H_DIL // N_DIL_GROUPS
CMP_LEN = 32
CMP_STRIDE = 16
CMP_HIDDEN = 128
SEL_BLOCK = 64
SEL_TOPK = 16
WIN_NSA = 512
Q_BLOCK = 128
D_FF = 4 * D_MODEL
N_SOFTMAX_HEADS = H_NSA + H_DIL
RMS_EPS = 1e-6
NEG_INF = -1e30
FORCE_BONUS = 1e4

NSA_Q_W = H_NSA * HEAD_DIM
NSA_KV_W = 6 * HEAD_DIM
NSA_GATE_W = 3 * H_NSA
DIL_W = 3 * H_DIL * HEAD_DIM
SB_W = 3 * H_SB * HEAD_DIM
D_PROJ = NSA_Q_W + NSA_KV_W + NSA_GATE_W + DIL_W + SB_W
PROJ_SPLITS = (NSA_Q_W, NSA_Q_W + NSA_KV_W, NSA_Q_W + NSA_KV_W + NSA_GATE_W,
               NSA_Q_W + NSA_KV_W + NSA_GATE_W + DIL_W)
D_CAT = (H_NSA + H_PER_DIL + H_SB) * HEAD_DIM

kernel_name = 'hybrid_nsa_dilated_stickbreak_block'


def rms_norm(x, g):
    xf = x.astype(jnp.float32)
    y = xf * lax.rsqrt(jnp.mean(xf * xf, axis=-1, keepdims=True) + RMS_EPS)
    return (y * g.astype(jnp.float32)).astype(x.dtype)


def alibi_slopes():
    i = jnp.arange(1, N_SOFTMAX_HEADS + 1, dtype=jnp.float32)
    return jnp.exp2(-8.0 * i / N_SOFTMAX_HEADS)


def masked_softmax(s, mask):
    s = jnp.where(mask, s, NEG_INF)
    m = jnp.max(s, axis=-1, keepdims=True)
    p = jnp.where(mask, jnp.exp(s - m), 0.0)
    l = jnp.maximum(jnp.sum(p, axis=-1, keepdims=True), 1e-30)
    return p / l, (m + jnp.log(l))[..., 0]


def unblock(y):
    y = jnp.moveaxis(y, 0, 1)
    return y.reshape((y.shape[0], y.shape[1] * y.shape[2]) + y.shape[3:])


def nsa_mixer(q, kv, gate_logits, qk_gain, cmp_pe, cmp_w1, cmp_w2, slopes):
    B, S = q.shape[0], q.shape[1]
    dt = q.dtype
    scale = HEAD_DIM ** -0.5
    q = rms_norm(q, qk_gain[0])
    k_sel = rms_norm(kv[:, :, 2], qk_gain[2])
    v_sel = kv[:, :, 3]
    k_win = rms_norm(kv[:, :, 4], qk_gain[3])
    v_win = kv[:, :, 5]
    n_cmp = (S - CMP_LEN) // CMP_STRIDE + 1
    cmp_start = CMP_STRIDE * jnp.arange(n_cmp)
    cmp_end = cmp_start + CMP_LEN - 1
    cmp_idx = cmp_start[:, None] + jnp.arange(CMP_LEN)[None]
    raw = kv[:, :, 0:2].transpose(0, 2, 1, 3)[:, :, cmp_idx]
    raw = (raw + cmp_pe[None, :, None]).reshape(B, 2, n_cmp, CMP_LEN * HEAD_DIM)
    hid = jax.nn.gelu(jnp.einsum('bcnf,cfh->bcnh', raw, cmp_w1))
    kv_c = jnp.einsum('bcnh,chd->bcnd', hid, cmp_w2)
    k_c = rms_norm(kv_c[:, 0], qk_gain[1])
    v_c = kv_c[:, 1]
    n_sel = S // SEL_BLOCK
    sel_start = SEL_BLOCK * jnp.arange(n_sel)
    overlap = ((cmp_start[:, None] <= sel_start[None] + SEL_BLOCK - 1)
               & (cmp_end[:, None] >= sel_start[None])).astype(jnp.float32)
    k_top = min(SEL_TOPK, n_sel)
    sel_offsets = jnp.arange(SEL_BLOCK)
    sel_ids = jnp.arange(n_sel)
    k_win_p = jnp.pad(k_win, ((0, 0), (WIN_NSA, 0), (0, 0)))
    v_win_p = jnp.pad(v_win, ((0, 0), (WIN_NSA, 0), (0, 0)))
    win_offsets = jnp.arange(WIN_NSA + Q_BLOCK)
    gates = jax.nn.sigmoid(gate_logits.astype(jnp.float32)).astype(dt)
    gather_rows = jax.vmap(lambda a, i: a[i])

    def block(n):
        t0 = n * Q_BLOCK
        t = t0 + jnp.arange(Q_BLOCK)
        qb = lax.dynamic_slice_in_dim(q, t0, Q_BLOCK, axis=1)
        d_c = (t[:, None] - cmp_end[None]).astype(jnp.float32)
        s = jnp.einsum('bqhd,bnd->bhqn', qb, k_c).astype(jnp.float32) * scale - slopes[:, None, None] * d_c
        p_c, _ = masked_softmax(s, d_c >= 0)
        o_c = jnp.einsum('bhqn,bnd->bqhd', p_c.astype(dt), v_c)
        imp = jnp.einsum('bhqn,nj->bqj', p_c, overlap)
        cur = t // SEL_BLOCK
        forced = (sel_ids[None] == 0) | (sel_ids[None] == cur[:, None]) | (sel_ids[None] == cur[:, None] - 1)
        imp = jnp.where(forced, imp + FORCE_BONUS, imp)
        imp = jnp.where(sel_start[None] <= t[:, None], imp, NEG_INF)
        _, top = lax.top_k(imp, k_top)
        tok = (top[..., None] * SEL_BLOCK + sel_offsets).reshape(B, Q_BLOCK, k_top * SEL_BLOCK)
        k_g = gather_rows(k_sel, tok)
        v_g = gather_rows(v_sel, tok)
        d_s = (t[None, :, None] - tok).astype(jnp.float32)
        s = jnp.einsum('bqhd,bqkd->bhqk', qb, k_g).astype(jnp.float32) * scale - slopes[None, :, None, None] * d_s[:, None]
        p_s, _ = masked_softmax(s, (d_s >= 0)[:, None])
        o_s = jnp.einsum('bhqk,bqkd->bqhd', p_s.astype(dt), v_g)
        k_w = lax.dynamic_slice_in_dim(k_win_p, t0, WIN_NSA + Q_BLOCK, axis=1)
        v_w = lax.dynamic_slice_in_dim(v_win_p, t0, WIN_NSA + Q_BLOCK, axis=1)
        kpos = t0 - WIN_NSA + win_offsets
        d_w = (t[:, None] - kpos[None]).astype(jnp.float32)
        mask_w = (d_w >= 0) & (d_w < WIN_NSA) & (kpos[None] >= 0)
        s = jnp.einsum('bqhd,bkd->bhqk', qb, k_w).astype(jnp.float32) * scale - slopes[:, None, None] * d_w
        p_w, _ = masked_softmax(s, mask_w)
        o_w = jnp.einsum('bhqk,bkd->bqhd', p_w.astype(dt), v_w)
        g = lax.dynamic_slice_in_dim(gates, t0, Q_BLOCK, axis=1)
        return g[..., 0:1] * o_c + g[..., 1:2] * o_s + g[..., 2:3] * o_w

    return unblock(lax.map(block, jnp.arange(S // Q_BLOCK)))


def dilated_mixer(qkv, qk_gain, slopes):
    S = qkv.shape[1]
    dt = qkv.dtype
    scale = HEAD_DIM ** -0.5
    q = rms_norm(qkv[:, :, 0], qk_gain[0])
    k = rms_norm(qkv[:, :, 1], qk_gain[1])
    v = qkv[:, :, 2]
    k_groups = [k[:, :, g] for g in range(N_DIL_GROUPS)]
    v_groups = [v[:, :, g] for g in range(N_DIL_GROUPS)]
    slopes_g = slopes.reshape(N_DIL_GROUPS, H_PER_DIL)

    def block(n):
        t0 = n * Q_BLOCK
        t = t0 + jnp.arange(Q_BLOCK)
        qb = lax.dynamic_slice_in_dim(q, t0, Q_BLOCK, axis=1)
        outs, lses = [], []
        for g, (w, r) in enumerate(DIL_PAIRS):
            dist = r * jnp.arange(w // r + 1)
            kidx = t[:, None] - dist[None]
            valid = kidx >= 0
            kidx = jnp.maximum(kidx, 0)
            k_g = k_groups[g][:, kidx]
            v_g = v_groups[g][:, kidx]
            s = (jnp.einsum('bqhd,bqjhd->bhqj', qb[:, :, g], k_g).astype(jnp.float32) * scale
                 - slopes_g[g][:, None, None] * dist.astype(jnp.float32))
            p, lse = masked_softmax(s, valid)
            outs.append(jnp.einsum('bhqj,bqjhd->bqhd', p.astype(dt), v_g))
            lses.append(lse)
        alpha = jax.nn.softmax(jnp.stack(lses), axis=0)
        alpha = jnp.transpose(alpha, (0, 1, 3, 2))[..., None].astype(dt)
        return jnp.sum(alpha * jnp.stack(outs), axis=0)

    return unblock(lax.map(block, jnp.arange(S // Q_BLOCK)))


def stick_breaking_mixer(qkv):
    S = qkv.shape[1]
    dt = qkv.dtype
    scale = HEAD_DIM ** -0.5
    q, k, v = qkv[:, :, 0], qkv[:, :, 1], qkv[:, :, 2]
    kpos = jnp.arange(S)

    def block(n):
        t0 = n * Q_BLOCK
        t = t0 + jnp.arange(Q_BLOCK)
        qb = lax.dynamic_slice_in_dim(q, t0, Q_BLOCK, axis=1)
        z = jnp.einsum('bqhd,bshd->bhqs', qb, k).astype(jnp.float32) * scale
        causal = kpos[None] < t[:, None]
        log_beta = jax.nn.log_sigmoid(z)
        log_fail = jnp.where(causal, jax.nn.log_sigmoid(-z), 0.0)
        after = lax.cumsum(log_fail, axis=3, reverse=True) - log_fail
        a = jnp.where(causal, jnp.exp(log_beta + after), 0.0)
        return jnp.einsum('bhqs,bshd->bqhd', a.astype(dt), v)

    return unblock(lax.map(block, jnp.arange(S // Q_BLOCK)))


def setup_inputs(seed: int = 0) -> dict:
    key = jax.random.key(seed)
    ks = jax.random.split(key, 12)
    f32 = jnp.float32

    def nrm(k, shape, fan_in):
        return jax.random.normal(k, shape, f32) * (fan_in ** -0.5)

    def gain(k, shape):
        return 1.0 + 0.02 * jax.random.normal(k, shape, f32)

    return {
        'x': jax.random.normal(ks[0], (BATCH, SEQ, D_MODEL), f32),
        'norm_mix': gain(ks[1], (DEPTH, D_MODEL)),
        'norm_mlp': gain(ks[2], (DEPTH, D_MODEL)),
        'w_in': nrm(ks[3], (DEPTH, D_MODEL, D_PROJ), D_MODEL),
        'qk_gain_nsa': gain(ks[4], (DEPTH, 4, HEAD_DIM)),
        'qk_gain_dil': gain(ks[5], (DEPTH, 2, HEAD_DIM)),
        'cmp_pe': 0.1 * jax.random.normal(ks[6], (DEPTH, 2, CMP_LEN, HEAD_DIM), f32),
        'cmp_w1': nrm(ks[7], (DEPTH, 2, CMP_LEN * HEAD_DIM, CMP_HIDDEN), CMP_LEN * HEAD_DIM),
        'cmp_w2': nrm(ks[8], (DEPTH, 2, CMP_HIDDEN, HEAD_DIM), CMP_HIDDEN),
        'w_out': nrm(ks[9], (DEPTH, D_CAT, D_MODEL), D_CAT),
        'w_up': nrm(ks[10], (DEPTH, D_MODEL, D_FF), D_MODEL),
        'w_down': nrm(ks[11], (DEPTH, D_FF, D_MODEL), D_FF),
    }


def reference(x, norm_mix, norm_mlp, w_in, qk_gain_nsa, qk_gain_dil, cmp_pe, cmp_w1, cmp_w2,
              w_out, w_up, w_down):
    B, S, _ = x.shape
    slopes = alibi_slopes()
    slopes_dil = slopes[:H_DIL]
    slopes_nsa = slopes[H_DIL:]
    for l in range(DEPTH):
        h = rms_norm(x, norm_mix[l])
        proj = h @ w_in[l]
        a_q, a_kv, a_g, b_qkv, c_qkv = jnp.split(proj, PROJ_SPLITS, axis=-1)
        o_a = nsa_mixer(a_q.reshape(B, S, H_NSA, HEAD_DIM), a_kv.reshape(B, S, 6, HEAD_DIM),
                        a_g.reshape(B, S, H_NSA, 3), qk_gain_nsa[l], cmp_pe[l], cmp_w1[l],
                        cmp_w2[l], slopes_nsa)
        o_b = dilated_mixer(b_qkv.reshape(B, S, 3, N_DIL_GROUPS, H_PER_DIL, HEAD_DIM),
                            qk_gain_dil[l], slopes_dil)
        o_c = stick_breaking_mixer(c_qkv.reshape(B, S, 3, H_SB, HEAD_DIM))
        cat = jnp.concatenate([o_a.reshape(B, S, -1), o_b.reshape(B, S, -1),
                               o_c.reshape(B, S, -1)], axis=-1)
        x = x + cat @ w_out[l]
        h = rms_norm(x, norm_mlp[l])
        x = x + jnp.square(jax.nn.relu(h @ w_up[l])) @ w_down[l]
    return x
```

```python
import functools
import math

import numpy as np
import jax
import jax.numpy as jnp
from jax import lax
from jax.experimental import pallas as pl
from jax.experimental.pallas import tpu as pltpu

F32 = jnp.float32
BF16 = jnp.bfloat16

D_MODEL = 1024
HEAD_DIM = 64
H_NSA = 4
H_DIL = 6
H_SB = 6
DIL_PAIRS = ((128, 1), (512, 4), (2048, 16))
CMP_LEN = 32
CMP_STRIDE = 16
CMP_HIDDEN = 128
SEL_BLOCK = 64
SEL_TOPK = 16
WIN_NSA = 512
D_FF = 4 * D_MODEL
RMS_EPS = 1e-6
NEG = -1e30
FORCE_BONUS = 1e4
LOG2E = 1.4426950408889634
SCALE = HEAD_DIM ** -0.5
LANES = 128
QB = 128
SEL_LANES = 128
SB_UNDERFLOW = -104.0

_SLOPES = [2.0 ** (-8.0 * i / (H_NSA + H_DIL)) for i in range(1, H_NSA + H_DIL + 1)]
SLOPES_DIL = _SLOPES[:H_DIL]
SLOPES_NSA = _SLOPES[H_DIL:]

_IN_SEGS = (("qa", 2, "norm"), ("kvc", 1, "raw"), ("ksks", 1, "norm"), ("kwkw", 1, "norm"),
            ("vsvs", 1, "raw"), ("vwvw", 1, "raw"), ("gate", 1, "gate"),
            ("dq", 3, "norm"), ("dk", 3, "norm"), ("dv", 3, "raw"),
            ("sq", 3, "raw"), ("sk", 3, "raw"), ("sv", 3, "raw"))
_IN_COLS = sum(n for _, n, _ in _IN_SEGS) * LANES
_VMEM_LIMIT = 56 * 1024 * 1024


def _cparams(*sem, vmem=_VMEM_LIMIT):
    return pltpu.CompilerParams(dimension_semantics=sem, vmem_limit_bytes=vmem)


def _nt_dot(a, b):
    return lax.dot_general(a, b, (((1,), (1,)), ((), ())), preferred_element_type=F32)


def _dot(a, b):
    return jnp.dot(a, b, preferred_element_type=F32)


def _split_dot(x, m):
    hi = x.astype(BF16)
    lo = (x - hi.astype(F32)).astype(BF16)
    return _dot(hi, m) + _dot(lo, m)


def _half_select(shape, even, odd):
    lane = lax.broadcasted_iota(jnp.int32, shape, len(shape) - 1)
    return jnp.where(lane % LANES < HEAD_DIM, even, odd)


def _mask_half(x, hh):
    lane = lax.broadcasted_iota(jnp.int32, x.shape, x.ndim - 1)
    keep = (lane % LANES < HEAD_DIM) if hh == 0 else (lane % LANES >= HEAD_DIM)
    return jnp.where(keep, x, jnp.zeros_like(x))


def _in_proj_body(x_ref, nw_ref, w_ref, cv_ref, gm_ref, *out_refs):
    x = x_ref[...]
    ms = jnp.mean(x * x, axis=-1, keepdims=True)
    h = (x * lax.rsqrt(ms + RMS_EPS) * nw_ref[...]).astype(BF16)
    gm = gm_ref[...]
    col = 0
    for (name, ntile, kind), o_ref in zip(_IN_SEGS, out_refs):
        for t in range(ntile):
            c0 = col + t * LANES
            y = _dot(h, w_ref[:, c0:c0 + LANES])
            if kind == "norm":
                msq = _split_dot(y * y, gm)
                y = y * lax.rsqrt(msq + RMS_EPS) * cv_ref[:, c0:c0 + LANES]
            elif kind == "gate":
                y = jax.nn.sigmoid(y)
            else:
                y = y * cv_ref[:, c0:c0 + LANES]
            o_ref[:, t * LANES:(t + 1) * LANES] = y.astype(o_ref.dtype)
        col += ntile * LANES


def _in_proj(x2d, nw, w_re, cv, gm, tm=256):
    n = x2d.shape[0]
    out_shape, out_specs = [], []
    for name, ntile, kind in _IN_SEGS:
        dt = F32 if kind == "gate" else BF16
        out_shape.append(jax.ShapeDtypeStruct((n, ntile * LANES), dt))
        out_specs.append(pl.BlockSpec((tm, ntile * LANES), lambda i: (i, 0)))
    return pl.pallas_call(
        _in_proj_body,
        grid=(n // tm,),
        in_specs=[pl.BlockSpec((tm, D_MODEL), lambda i: (i, 0)),
                  pl.BlockSpec((1, D_MODEL), lambda i: (0, 0)),
                  pl.BlockSpec((D_MODEL, _IN_COLS), lambda i: (0, 0)),
                  pl.BlockSpec((1, _IN_COLS), lambda i: (0, 0)),
                  pl.BlockSpec((LANES, LANES), lambda i: (0, 0))],
        out_specs=out_specs,
        out_shape=out_shape,
        compiler_params=_cparams("parallel"),
        name="in_proj",
    )(x2d, nw, w_re, cv, gm)


def _gelu_tanh(x):
    return 0.5 * x * (1.0 + jnp.tanh(0.7978845608028654 * (x + 0.044715 * (x * x * x))))


def _cmp_body(x_ref, w1_ref, pe_ref, w2_ref, gk_ref, kc_ref, vc_ref):
    x = x_ref[0]
    ncp = x.shape[0]
    pe = pe_ref[...].astype(BF16)
    outs = []
    for c in range(2):
        top = _dot(x, w1_ref[2 * c])
        bot = _dot(x, w1_ref[2 * c + 1])
        bias = _dot(pe, w1_ref[2 * c])[2 * c:2 * c + 1] + _dot(pe, w1_ref[2 * c + 1])[2 * c + 1:2 * c + 2]
        hid = top + pltpu.roll(bot, ncp - 1, 0) + bias
        outs.append(_dot(_gelu_tanh(hid).astype(BF16), w2_ref[c]))
    kc = outs[0]
    kc = kc * lax.rsqrt(jnp.mean(kc * kc, axis=-1, keepdims=True) + RMS_EPS) * gk_ref[...]
    kc_ref[0] = kc.astype(BF16)
    vc_ref[0] = outs[1].astype(BF16)


def _compress(kvc16, w1i, pe8, w2r, gk):
    b, ncp, wid = kvc16.shape
    return pl.pallas_call(
        _cmp_body,
        grid=(b,),
        in_specs=[pl.BlockSpec((1, ncp, wid), lambda i: (i, 0, 0)),
                  pl.BlockSpec((4, wid, CMP_HIDDEN), lambda i: (0, 0, 0)),
                  pl.BlockSpec((8, wid), lambda i: (0, 0)),
                  pl.BlockSpec((2, CMP_HIDDEN, LANES), lambda i: (0, 0, 0)),
                  pl.BlockSpec((1, LANES), lambda i: (0, 0))],
        out_specs=[pl.BlockSpec((1, ncp, LANES), lambda i: (i, 0, 0)),
                   pl.BlockSpec((1, ncp, LANES), lambda i: (i, 0, 0))],
        out_shape=[jax.ShapeDtypeStruct((b, ncp, LANES), BF16)] * 2,
        compiler_params=_cparams("parallel"),
        name="nsa_compress",
    )(kvc16, w1i, pe8, w2r, gk)


def _softmax_rows(s, mask):
    m = jnp.max(s, axis=-1, keepdims=True)
    p = jnp.where(mask, jnp.exp2(s - m), 0.0)
    l = jnp.maximum(jnp.sum(p, axis=-1, keepdims=True), 1e-30)
    return p, l


def _cw_body(q_ref, kc_ref, vc_ref, kw_ref, vw_ref, g_ref, ov_ref, oa_ref, ns_ref):
    i = pl.program_id(1)
    t0 = i * QB
    q = q_ref[0]
    g = g_ref[0]
    kc = kc_ref[0]
    vc = vc_ref[0]
    ncp = kc.shape[0]
    wk = WIN_NSA + QB

    n_idx = lax.broadcasted_iota(jnp.int32, (QB, ncp), 1)
    q_idx = lax.broadcasted_iota(jnp.int32, (QB, ncp), 0)
    vis = (t0 - (CMP_LEN - 1)) + q_idx - CMP_STRIDE * n_idx >= 0
    cend = (CMP_STRIDE * lax.broadcasted_iota(jnp.int32, (1, ncp), 1) + (CMP_LEN - 1)).astype(F32)

    start = pl.multiple_of(jnp.maximum(t0 - WIN_NSA, 0), QB)
    kw = kw_ref[0, pl.ds(start, wk), :]
    vw = vw_ref[0, pl.ds(start, wk), :]
    j_idx = lax.broadcasted_iota(jnp.int32, (QB, wk), 1)
    r_idx = lax.broadcasted_iota(jnp.int32, (QB, wk), 0)
    dw = (t0 - start) + r_idx - j_idx
    wmask = (dw >= 0) & (dw < WIN_NSA)
    kposw = (start + lax.broadcasted_iota(jnp.int32, (1, wk), 1)).astype(F32)

    psum = jnp.zeros((QB, ncp), F32)
    o_heads = []
    for h in range(H_NSA):
        slab = q[:, (h // 2) * LANES:(h // 2 + 1) * LANES]
        qm = _mask_half(slab, h % 2)
        sl = SLOPES_NSA[h] * LOG2E
        s = _nt_dot(qm, kc) + sl * cend
        s = jnp.where(vis, s, NEG)
        p, l = _softmax_rows(s, vis)
        pn = p / l
        psum = psum + pn
        oc = _dot((pn * g[:, 3 * h:3 * h + 1]).astype(BF16), vc)
        s = _nt_dot(qm, kw) + sl * kposw
        s = jnp.where(wmask, s, NEG)
        p, l = _softmax_rows(s, wmask)
        ow = _dot((p * (g[:, 3 * h + 2:3 * h + 3] / l)).astype(BF16), vw)
        o_heads.append(oc + ow)
    for hp in range(H_NSA // 2):
        oa_ref[0, :, hp * LANES:(hp + 1) * LANES] = _half_select((QB, LANES), o_heads[2 * hp], o_heads[2 * hp + 1])

    imp = _split_dot(psum, ov_ref[...])
    j = lax.broadcasted_iota(jnp.int32, (QB, SEL_LANES), 1)
    qi = lax.broadcasted_iota(jnp.int32, (QB, SEL_LANES), 0)
    cur = jnp.right_shift(t0 + qi, int(math.log2(SEL_BLOCK)))
    forced = (j == 0) | (j == cur) | (j == cur - 1)
    imp = jnp.where(forced, imp + FORCE_BONUS, imp)
    imp = jnp.where(j <= cur, imp, NEG)
    jf = j.astype(F32)
    notsel = jnp.ones((QB, SEL_LANES), F32)
    for _ in range(SEL_TOPK):
        mx = jnp.max(imp, axis=-1, keepdims=True)
        idx = jnp.min(jnp.where(imp == mx, jf, float(SEL_LANES)), axis=-1, keepdims=True)
        hit = jf == idx
        notsel = jnp.where(hit, 0.0, notsel)
        imp = jnp.where(hit, -3e38, imp)
    ns_ref[0] = notsel.astype(BF16)


def _nsa_cmp_win(qa, kcr, vcr, kwkw, vwvw, gates, ov):
    b, s, _ = qa.shape
    ncp = kcr.shape[1]
    return pl.pallas_call(
        _cw_body,
        grid=(b, s // QB),
        in_specs=[pl.BlockSpec((1, QB, 2 * LANES), lambda bi, i: (bi, i, 0)),
                  pl.BlockSpec((1, ncp, LANES), lambda bi, i: (bi, 0, 0)),
                  pl.BlockSpec((1, ncp, LANES), lambda bi, i: (bi, 0, 0)),
                  pl.BlockSpec((1, s, LANES), lambda bi, i: (bi, 0, 0)),
                  pl.BlockSpec((1, s, LANES), lambda bi, i: (bi, 0, 0)),
                  pl.BlockSpec((1, QB, LANES), lambda bi, i: (bi, i, 0)),
                  pl.BlockSpec((ncp, SEL_LANES), lambda bi, i: (0, 0))],
        out_specs=[pl.BlockSpec((1, QB, 2 * LANES), lambda bi, i: (bi, i, 0)),
                   pl.BlockSpec((1, QB, SEL_LANES), lambda bi, i: (bi, i, 0))],
        out_shape=[jax.ShapeDtypeStruct((b, s, 2 * LANES), F32),
                   jax.ShapeDtypeStruct((b, s, SEL_LANES), BF16)],
        compiler_params=_cparams("parallel", "parallel"),
        name="nsa_cmp_win",
    )(qa, kcr, vcr, kwkw, vwvw, gates, ov)


def _sel_body(q_ref, ns_ref, g_ref, part_ref, ks_ref, vs_ref, oh_ref, oa_ref,
              kaug_ref, qaug_ref, m_ref, l_ref, acc_ref, *, tk):
    i = pl.program_id(1)
    t0 = i * QB

    @pl.when(i == 0)
    def _():
        kaug_ref[:, 0:LANES] = ks_ref[0]
        kaug_ref[:, LANES:2 * LANES] = oh_ref[...]

    q = q_ref[0]
    ns = ns_ref[0]
    for h in range(H_NSA):
        slab = q[:, (h // 2) * LANES:(h // 2 + 1) * LANES]
        qaug_ref[h * QB:(h + 1) * QB, 0:LANES] = _mask_half(slab, h % 2)
        qaug_ref[h * QB:(h + 1) * QB, LANES:2 * LANES] = ns
    m_ref[...] = jnp.full(m_ref.shape, NEG, F32)
    l_ref[...] = jnp.zeros(l_ref.shape, F32)
    acc_ref[...] = jnp.zeros(acc_ref.shape, F32)

    row = lax.broadcasted_iota(jnp.int32, (QB, tk), 0)
    colk = lax.broadcasted_iota(jnp.int32, (QB, tk), 1)
    col1 = lax.broadcasted_iota(jnp.int32, (1, tk), 1)

    def body(jt, carry):
        k0 = pl.multiple_of(jt * tk, tk)
        ka = kaug_ref[pl.ds(k0, tk), :]
        va = vs_ref[0, pl.ds(k0, tk), :]
        s_all = _nt_dot(qaug_ref[...], ka)
        causal = (k0 + colk) <= (t0 + row)
        kposf = (k0 + col1).astype(F32)
        for h in range(H_NSA):
            rs = slice(h * QB, (h + 1) * QB)
            s = s_all[rs] + (SLOPES_NSA[h] * LOG2E) * kposf
            s = jnp.where(causal, s, NEG)
            m_old = m_ref[rs, 0:1]
            m_new = jnp.maximum(m_old, jnp.max(s, axis=-1, keepdims=True))
            alpha = jnp.exp2(m_old - m_new)
            p = jnp.exp2(s - m_new)
            l_ref[rs, :] = jnp.broadcast_to(alpha * l_ref[rs, 0:1] + jnp.sum(p, axis=-1, keepdims=True), (QB, LANES))
            m_ref[rs, :] = jnp.broadcast_to(m_new, (QB, LANES))
            acc_ref[rs, :] = alpha * acc_ref[rs, :] + _dot(p.astype(BF16), va)
        return carry

    n_tiles = (t0 + QB + tk - 1) // tk
    lax.fori_loop(0, n_tiles, body, 0)

    g = g_ref[0]
    for hp in range(H_NSA // 2):
        o = []
        for hh in range(2):
            h = 2 * hp + hh
            rs = slice(h * QB, (h + 1) * QB)
            o.append(acc_ref[rs, :] * (g[:, 3 * h + 1:3 * h + 2] / l_ref[rs, 0:1]))
        cs = slice(hp * LANES, (hp + 1) * LANES)
        oa_ref[0, :, cs] = part_ref[0, :, cs] + _half_select((QB, LANES), o[0], o[1])


def _nsa_selected(qa, nsel, gates, part, ksks, vsvs, oh, tk=256):
    b, s, _ = qa.shape
    return pl.pallas_call(
        functools.partial(_sel_body, tk=tk),
        grid=(b, s // QB),
        in_specs=[pl.BlockSpec((1, QB, 2 * LANES), lambda bi, i: (bi, i, 0)),
                  pl.BlockSpec((1, QB, SEL_LANES), lambda bi, i: (bi, i, 0)),
                  pl.BlockSpec((1, QB, LANES), lambda bi, i: (bi, i, 0)),
                  pl.BlockSpec((1, QB, 2 * LANES), lambda bi, i: (bi, i, 0)),
                  pl.BlockSpec((1, s, LANES), lambda bi, i: (bi, 0, 0)),
                  pl.BlockSpec((1, s, LANES), lambda bi, i: (bi, 0, 0)),
                  pl.BlockSpec((s, SEL_LANES), lambda bi, i: (0, 0))],
        out_specs=pl.BlockSpec((1, QB, 2 * LANES), lambda bi, i: (bi, i, 0)),
        out_shape=jax.ShapeDtypeStruct((b, s, 2 * LANES), F32),
        scratch_shapes=[pltpu.VMEM((s, 2 * LANES), BF16),
                        pltpu.VMEM((H_NSA * QB, 2 * LANES), BF16),
                        pltpu.VMEM((H_NSA * QB, LANES), F32),
                        pltpu.VMEM((H_NSA * QB, LANES), F32),
                        pltpu.VMEM((H_NSA * QB, LANES), F32)],
        compiler_params=_cparams("arbitrary", "arbitrary"),
        name="nsa_selected",
    )(qa, nsel, gates, part, ksks, vsvs, oh)


def _dil_body(q_ref, kp_ref, kc_ref, vp_ref, vc_ref, bias_ref, o_ref, lse_ref):
    i = pl.program_id(2)
    q = q_ref[0]
    kk = jnp.concatenate([kp_ref[0], kc_ref[0]], axis=0)
    vv = jnp.concatenate([vp_ref[0], vc_ref[0]], axis=0)
    colk = lax.broadcasted_iota(jnp.int32, (QB, 2 * QB), 1)
    valid = colk >= jnp.where(i > 0, 0, QB)
    outs, lses = [], []
    for hh in range(2):
        s = _nt_dot(_mask_half(q, hh), kk) + bias_ref[hh]
        s = jnp.where(valid, s, NEG)
        m = jnp.max(s, axis=-1, keepdims=True)
        p = jnp.exp2(s - m)
        l = jnp.sum(p, axis=-1, keepdims=True)
        outs.append(_dot(p.astype(BF16), vv) / l)
        lses.append(jnp.broadcast_to(m + jnp.log2(l), (QB, LANES)))
    o_ref[0] = _half_select((QB, LANES), outs[0], outs[1])
    lse_ref[0] = _half_select((QB, LANES), lses[0], lses[1])


def _dilated(dq, dk, dv, bias, g, r):
    b, s, _ = dq.shape
    sr = s // r
    ng = len(DIL_PAIRS)
    qv, kv_, vv = (a.reshape(b, sr, r * ng * LANES) for a in (dq, dk, dv))
    cur = lambda bi, c, i: (bi, i, c * ng + g)
    prev = lambda bi, c, i: (bi, jnp.maximum(i - 1, 0), c * ng + g)
    o, lse = pl.pallas_call(
        _dil_body,
        grid=(b, r, sr // QB),
        in_specs=[pl.BlockSpec((1, QB, LANES), cur),
                  pl.BlockSpec((1, QB, LANES), prev), pl.BlockSpec((1, QB, LANES), cur),
                  pl.BlockSpec((1, QB, LANES), prev), pl.BlockSpec((1, QB, LANES), cur),
                  pl.BlockSpec((2, QB, 2 * QB), lambda bi, c, i: (0, 0, 0))],
        out_specs=[pl.BlockSpec((1, QB, LANES), lambda bi, c, i: (bi, i, c))] * 2,
        out_shape=[jax.ShapeDtypeStruct((b, sr, r * LANES), F32)] * 2,
        compiler_params=_cparams("parallel", "parallel", "parallel"),
        name=f"dilated_r{r}",
    )(qv, kv_, kv_, vv, vv, bias)
    return o.reshape(b, s, LANES), lse.reshape(b, s, LANES)


def _dil_bias(g):
    w, r = DIL_PAIRS[g]
    assert w // r == QB
    iq = np.arange(QB)[:, None]
    jk = np.arange(2 * QB)[None, :]
    dist = iq + QB - jk
    out = np.empty((2, QB, 2 * QB), np.float32)
    for hh in range(2):
        slope = SLOPES_DIL[2 * g + hh]
        out[hh] = np.where((dist >= 0) & (dist <= QB), -slope * LOG2E * r * dist, NEG)
    return jnp.asarray(out)


def _sb_body(q_ref, k_ref, v_ref, ut_ref, o_ref):
    i = pl.program_id(2)
    q = q_ref[0]
    ut = ut_ref[...]
    row = lax.broadcasted_iota(jnp.int32, (QB, QB), 0)
    col = lax.broadcasted_iota(jnp.int32, (QB, QB), 1)
    outs = []
    for hh in range(2):
        qm = _mask_half(q, hh)

        def cond(c):
            jt, _, _, cmax = c
            return (jt >= 0) & (cmax > SB_UNDERFLOW)

        def body(c):
            jt, acc, carry, _ = c
            k0 = pl.multiple_of(jt * QB, QB)
            kt = k_ref[0, pl.ds(k0, QB), :]
            vt = v_ref[0, pl.ds(k0, QB), :]
            z = _nt_dot(qm, kt)
            lb = jnp.minimum(z, 0.0) - jnp.log1p(jnp.exp(-jnp.abs(z)))
            mask = col < jnp.where(jt < i, QB, row)
            lf = jnp.where(mask, lb - z, 0.0)
            cs = _split_dot(lf, ut)
            a = jnp.where(mask, jnp.exp(lb + cs[:, :QB] + carry), 0.0)
            acc = acc + _dot(a.astype(BF16), vt)
            carry = carry + cs[:, QB:]
            return jt - 1, acc, carry, jnp.max(carry)

        init = (i, jnp.zeros((QB, LANES), F32), jnp.zeros((QB, LANES), F32), jnp.float32(0.0))
        outs.append(lax.while_loop(cond, body, init)[1])
    o_ref[0] = _half_select((QB, LANES), outs[0], outs[1])


def _stick_breaking(sq, sk, sv, ut):
    b, s, _ = sq.shape
    npair = H_SB // 2
    return pl.pallas_call(
        _sb_body,
        grid=(b, npair, s // QB),
        in_specs=[pl.BlockSpec((1, QB, LANES), lambda bi, hp, i: (bi, i, hp)),
                  pl.BlockSpec((1, s, LANES), lambda bi, hp, i: (bi, 0, hp)),
                  pl.BlockSpec((1, s, LANES), lambda bi, hp, i: (bi, 0, hp)),
                  pl.BlockSpec((QB, 2 * QB), lambda bi, hp, i: (0, 0))],
        out_specs=pl.BlockSpec((1, QB, LANES), lambda bi, hp, i: (bi, i, hp)),
        out_shape=jax.ShapeDtypeStruct((b, s, npair * LANES), F32),
        compiler_params=_cparams("parallel", "parallel", "parallel"),
        name="stick_breaking",
    )(sq, sk, sv, ut)


def _out_body(x_ref, oa_ref, d0, l0, d1, l1, d2, l2, oc_ref, w_ref, o_ref):
    lses = [l0[...], l1[...], l2[...]]
    m = jnp.maximum(jnp.maximum(lses[0], lses[1]), lses[2])
    es = [jnp.exp2(l - m) for l in lses]
    den = es[0] + es[1] + es[2]
    ob = (es[0] * d0[...] + es[1] * d1[...] + es[2] * d2[...]) / den
    na = H_NSA * HEAD_DIM
    nb = na + LANES
    acc = _dot(oa_ref[...].astype(BF16), w_ref[0:na, :])
    acc = acc + _dot(ob.astype(BF16), w_ref[na:nb, :])
    acc = acc + _dot(oc_ref[...].astype(BF16), w_ref[nb:, :])
    o_ref[...] = x_ref[...] + acc


def _out_proj(x2d, oa, dil, oc, w_out, tm=512):
    n = x2d.shape[0]
    row = lambda w: pl.BlockSpec((tm, w), lambda i: (i, 0))
    d_cat = w_out.shape[0]
    ins = [x2d, oa]
    specs = [row(D_MODEL), row(2 * LANES)]
    for o, lse in dil:
        ins += [o, lse]
        specs += [row(LANES), row(LANES)]
    ins += [oc, w_out]
    specs += [row(oc.shape[1]), pl.BlockSpec((d_cat, D_MODEL), lambda i: (0, 0))]
    return pl.pallas_call(
        _out_body,
        grid=(n // tm,),
        in_specs=specs,
        out_specs=row(D_MODEL),
        out_shape=jax.ShapeDtypeStruct((n, D_MODEL), F32),
        compiler_params=_cparams("parallel"),
        name="out_proj",
    )(*ins)


def _mlp_body(x_ref, nw_ref, wu_ref, wd_ref, o_ref, *, fc):
    x = x_ref[...]
    ms = jnp.mean(x * x, axis=-1, keepdims=True)
    h = (x * lax.rsqrt(ms + RMS_EPS) * nw_ref[...]).astype(BF16)
    acc = x
    for c in range(D_FF // fc):
        u = jnp.maximum(_dot(h, wu_ref[:, c * fc:(c + 1) * fc]), 0.0)
        acc = acc + _dot((u * u).astype(BF16), wd_ref[c * fc:(c + 1) * fc, :])
    o_ref[...] = acc


def _mlp(x2d, nw, wu, wd, tm=512, fc=1024):
    n = x2d.shape[0]
    const = dict(pipeline_mode=pl.Buffered(1))
    return pl.pallas_call(
        functools.partial(_mlp_body, fc=fc),
        grid=(n // tm,),
        in_specs=[pl.BlockSpec((tm, D_MODEL), lambda i: (i, 0)),
                  pl.BlockSpec((1, D_MODEL), lambda i: (0, 0)),
                  pl.BlockSpec((D_MODEL, D_FF), lambda i: (0, 0), **const),
                  pl.BlockSpec((D_FF, D_MODEL), lambda i: (0, 0), **const)],
        out_specs=pl.BlockSpec((tm, D_MODEL), lambda i: (i, 0)),
        out_shape=jax.ShapeDtypeStruct((n, D_MODEL), F32),
        compiler_params=_cparams("parallel"),
        name="mlp",
    )(x2d, nw, wu, wd)


def _relayout_in_weight(w, g_nsa, g_dil):
    hd = HEAD_DIM
    kv0 = H_NSA * hd
    g0 = kv0 + 6 * hd
    b0 = g0 + 3 * H_NSA
    c0 = b0 + 3 * H_DIL * hd
    kv = lambda c: w[:, kv0 + c * hd:kv0 + (c + 1) * hd]
    gate = jnp.pad(w[:, g0:b0], ((0, 0), (0, LANES - 3 * H_NSA)))
    w_re = jnp.concatenate([w[:, :kv0], kv(0), kv(1), kv(2), kv(2), kv(4), kv(4), kv(3), kv(3), kv(5), kv(5),
                            gate, w[:, b0:c0], w[:, c0:]], axis=1).astype(BF16)
    one = lambda n: jnp.ones((n,), F32)
    cv = jnp.concatenate([
        jnp.tile(g_nsa[0], H_NSA) * (SCALE * LOG2E), one(LANES),
        jnp.tile(g_nsa[2], 2), jnp.tile(g_nsa[3], 2), one(3 * LANES),
        jnp.tile(g_dil[0], H_DIL) * (SCALE * LOG2E), jnp.tile(g_dil[1], H_DIL), one(H_DIL * hd),
        one(H_SB * hd) * SCALE, one(2 * H_SB * hd)])
    return w_re, cv.reshape(1, _IN_COLS)


def _relayout_cmp(w1, pe, w2, gk):
    hd, half = HEAD_DIM, CMP_LEN // 2
    w1r = w1.reshape(2, 2, half, hd, CMP_HIDDEN)
    per = pe.reshape(2, 2, half, hd)
    w1i, pei = [], []
    for c in range(2):
        pad = ((0, 0), (0, 0), (0, hd), (0, 0)) if c == 0 else ((0, 0), (0, 0), (hd, 0), (0, 0))
        w1i.append(jnp.pad(w1r[c], pad).reshape(2, half * LANES, CMP_HIDDEN))
        pei.append(jnp.pad(per[c], pad[:3]).reshape(2, half * LANES))
    w1i = jnp.concatenate(w1i, axis=0).astype(BF16)
    pe8 = jnp.pad(jnp.concatenate(pei, axis=0), ((0, 4), (0, 0)))
    w2r = jnp.concatenate([w2, w2], axis=-1).astype(BF16)
    return w1i, pe8, w2r, jnp.tile(gk, 2).reshape(1, LANES)


def _constants(s):
    ncp = s // CMP_STRIDE
    n_cmp = (s - CMP_LEN) // CMP_STRIDE + 1
    n = np.arange(ncp)[:, None]
    j = np.arange(SEL_LANES)[None, :]
    ov = ((CMP_STRIDE * n <= SEL_BLOCK * j + SEL_BLOCK - 1) & (CMP_STRIDE * n + CMP_LEN - 1 >= SEL_BLOCK * j)
          & (n < n_cmp) & (j < s // SEL_BLOCK))
    oh = np.where(np.arange(s)[:, None] // SEL_BLOCK == j, NEG, 0.0)
    a = np.arange(QB)
    ut = np.concatenate([(a[:, None] > a[None, :]).astype(np.float32), np.ones((QB, QB), np.float32)], axis=1)
    gm = (a[:, None] // HEAD_DIM == a[None, :] // HEAD_DIM).astype(np.float32) / HEAD_DIM
    return dict(ov=jnp.asarray(ov, BF16), oh=jnp.asarray(oh, BF16), ut=jnp.asarray(ut, BF16),
                gm=jnp.asarray(gm, BF16), dil_bias=[_dil_bias(g) for g in range(len(DIL_PAIRS))])


def kernel(x, norm_mix, norm_mlp, w_in, qk_gain_nsa, qk_gain_dil, cmp_pe, cmp_w1, cmp_w2, w_out, w_up, w_down):
    b, s, d = x.shape
    assert d == D_MODEL and s % (DIL_PAIRS[-1][1] * QB) == 0 and s // SEL_BLOCK <= SEL_LANES
    assert s >= WIN_NSA + QB
    n = b * s
    cst = _constants(s)
    x2d = x.reshape(n, d)
    for l in range(w_in.shape[0]):
        w_re, cv = _relayout_in_weight(w_in[l], qk_gain_nsa[l], qk_gain_dil[l])
        qa, kvc, ksks, kwkw, vsvs, vwvw, gates, dq, dk, dv, sq, sk, sv = _in_proj(
            x2d, norm_mix[l].reshape(1, d), w_re, cv, cst["gm"])
        tok = lambda a: a.reshape(b, s, a.shape[-1])
        w1i, pe8, w2r, gk = _relayout_cmp(cmp_w1[l], cmp_pe[l], cmp_w2[l], qk_gain_nsa[l, 1])
        kcr, vcr = _compress(kvc.reshape(b, s // CMP_STRIDE, CMP_STRIDE * LANES), w1i, pe8, w2r, gk)
        part, nsel = _nsa_cmp_win(tok(qa), kcr, vcr, tok(kwkw), tok(vwvw), tok(gates), cst["ov"])
        oa = _nsa_selected(tok(qa), nsel, tok(gates), part, tok(ksks), tok(vsvs), cst["oh"])
        dil = []
        for g, (_, r) in enumerate(DIL_PAIRS):
            o, lse = _dilated(tok(dq), tok(dk), tok(dv), cst["dil_bias"][g], g, r)
            dil.append((o.reshape(n, LANES), lse.reshape(n, LANES)))
        oc = _stick_breaking(tok(sq), tok(sk), tok(sv), cst["ut"])
        x2d = _out_proj(x2d, oa.reshape(n, 2 * LANES), dil, oc.reshape(n, -1), w_out[l].astype(BF16))
        x2d = _mlp(x2d, norm_mlp[l].reshape(1, d), w_up[l].astype(BF16), w_down[l].astype(BF16))
    return x2d.reshape(b, s, d)
```

```python
import functools
import math

import numpy as np
import jax
import jax.numpy as jnp
from jax import lax
from jax.experimental import pallas as pl
from jax.experimental.pallas import tpu as pltpu

F32 = jnp.float32
BF16 = jnp.bfloat16

D_MODEL = 1024
HEAD_DIM = 64
H_NSA = 4
H_DIL = 6
H_SB = 6
DIL_PAIRS = ((128, 1), (512, 4), (2048, 16))
CMP_LEN = 32
CMP_STRIDE = 16
CMP_HIDDEN = 128
SEL_BLOCK = 64
SEL_TOPK = 16
WIN_NSA = 512
D_FF = 4 * D_MODEL
RMS_EPS = 1e-6
NEG = -1e30
FORCE_BONUS = 1e4
LOG2E = 1.4426950408889634
SCALE = HEAD_DIM ** -0.5
LANES = 128
QB = 128
SEL_LANES = 128
SB_UNDERFLOW = -104.0

_SLOPES = [2.0 ** (-8.0 * i / (H_NSA + H_DIL)) for i in range(1, H_NSA + H_DIL + 1)]
SLOPES_DIL = _SLOPES[:H_DIL]
SLOPES_NSA = _SLOPES[H_DIL:]

_IN_SEGS = (("qa", 2, "norm"), ("kvc", 1, "raw"), ("ksks", 1, "norm"), ("kwkw", 1, "norm"),
            ("vsvs", 1, "raw"), ("vwvw", 1, "raw"), ("gate", 1, "gate"),
            ("dq", 3, "norm"), ("dk", 3, "norm"), ("dv", 3, "raw"),
            ("sq", 3, "raw"), ("sk", 3, "raw"), ("sv", 3, "raw"))
_IN_COLS = sum(n for _, n, _ in _IN_SEGS) * LANES
_VMEM_LIMIT = 56 * 1024 * 1024


def _cparams(*sem, vmem=_VMEM_LIMIT):
    return pltpu.CompilerParams(dimension_semantics=sem, vmem_limit_bytes=vmem)


def _nt_dot(a, b):
    return lax.dot_general(a, b, (((1,), (1,)), ((), ())), preferred_element_type=F32)


def _dot(a, b):
    return jnp.dot(a, b, preferred_element_type=F32)


def _tn_dot(a, b):
    return lax.dot_general(a, b, (((0,), (0,)), ((), ())), preferred_element_type=F32)


def _split_dot(x, m):
    hi = x.astype(BF16)
    lo = (x - hi.astype(F32)).astype(BF16)
    return _dot(hi, m) + _dot(lo, m)


def _mask_half(x, hh):
    lane = lax.broadcasted_iota(jnp.int32, x.shape, x.ndim - 1)
    keep = (lane % LANES < HEAD_DIM) if hh == 0 else (lane % LANES >= HEAD_DIM)
    return jnp.where(keep, x, jnp.zeros_like(x))


def _in_proj_body(x_ref, nw_ref, w_ref, cv_ref, gm_ref, *out_refs):
    x = x_ref[...]
    ms = jnp.mean(x * x, axis=-1, keepdims=True)
    h = (x * lax.rsqrt(ms + RMS_EPS) * nw_ref[...]).astype(BF16)
    gm = gm_ref[...]
    col = 0
    for (name, ntile, kind), o_ref in zip(_IN_SEGS, out_refs):
        for t in range(ntile):
            c0 = col + t * LANES
            y = _dot(h, w_ref[:, c0:c0 + LANES])
            if kind == "norm":
                msq = _split_dot(y * y, gm)
                y = y * lax.rsqrt(msq + RMS_EPS) * cv_ref[:, c0:c0 + LANES]
            elif kind == "gate":
                y = jax.nn.sigmoid(y)
            else:
                y = y * cv_ref[:, c0:c0 + LANES]
            o_ref[:, t * LANES:(t + 1) * LANES] = y.astype(o_ref.dtype)
        col += ntile * LANES


def _in_proj(x2d, nw, w_re, cv, gm, tm=256):
    n = x2d.shape[0]
    out_shape, out_specs = [], []
    for name, ntile, kind in _IN_SEGS:
        dt = F32 if kind == "gate" else BF16
        out_shape.append(jax.ShapeDtypeStruct((n, ntile * LANES), dt))
        out_specs.append(pl.BlockSpec((tm, ntile * LANES), lambda i: (i, 0)))
    return pl.pallas_call(
        _in_proj_body,
        grid=(n // tm,),
        in_specs=[pl.BlockSpec((tm, D_MODEL), lambda i: (i, 0)),
                  pl.BlockSpec((1, D_MODEL), lambda i: (0, 0)),
                  pl.BlockSpec((D_MODEL, _IN_COLS), lambda i: (0, 0)),
                  pl.BlockSpec((1, _IN_COLS), lambda i: (0, 0)),
                  pl.BlockSpec((LANES, LANES), lambda i: (0, 0))],
        out_specs=out_specs,
        out_shape=out_shape,
        compiler_params=_cparams("parallel"),
        name="in_proj",
    )(x2d, nw, w_re, cv, gm)


def _gelu_tanh(x):
    return 0.5 * x * (1.0 + jnp.tanh(0.7978845608028654 * (x + 0.044715 * (x * x * x))))


def _cmp_body(x_ref, w1_ref, pe_ref, w2_ref, gk_ref, kc_ref, vc_ref):
    x = x_ref[0]
    ncp = x.shape[0]
    pe = pe_ref[...].astype(BF16)
    outs = []
    for c in range(2):
        top = _dot(x, w1_ref[2 * c])
        bot = _dot(x, w1_ref[2 * c + 1])
        bias = _dot(pe, w1_ref[2 * c])[2 * c:2 * c + 1] + _dot(pe, w1_ref[2 * c + 1])[2 * c + 1:2 * c + 2]
        hid = top + pltpu.roll(bot, ncp - 1, 0) + bias
        outs.append(_dot(_gelu_tanh(hid).astype(BF16), w2_ref[c]))
    kc = outs[0]
    kc = kc * lax.rsqrt(jnp.mean(kc * kc, axis=-1, keepdims=True) + RMS_EPS) * gk_ref[...]
    kc_ref[0] = kc.astype(BF16)
    vc_ref[0] = outs[1].astype(BF16)


def _compress(kvc16, w1i, pe8, w2r, gk):
    b, ncp, wid = kvc16.shape
    return pl.pallas_call(
        _cmp_body,
        grid=(b,),
        in_specs=[pl.BlockSpec((1, ncp, wid), lambda i: (i, 0, 0)),
                  pl.BlockSpec((4, wid, CMP_HIDDEN), lambda i: (0, 0, 0)),
                  pl.BlockSpec((8, wid), lambda i: (0, 0)),
                  pl.BlockSpec((2, CMP_HIDDEN, LANES), lambda i: (0, 0, 0)),
                  pl.BlockSpec((1, LANES), lambda i: (0, 0))],
        out_specs=[pl.BlockSpec((1, ncp, LANES), lambda i: (i, 0, 0)),
                   pl.BlockSpec((1, ncp, LANES), lambda i: (i, 0, 0))],
        out_shape=[jax.ShapeDtypeStruct((b, ncp, LANES), BF16)] * 2,
        compiler_params=_cparams("parallel"),
        name="nsa_compress",
    )(kvc16, w1i, pe8, w2r, gk)


def _softmax_cols(s, mask):
    m = jnp.max(s, axis=0, keepdims=True)
    p = jnp.where(mask, jnp.exp2(s - m), 0.0)
    l = jnp.maximum(jnp.sum(p, axis=0, keepdims=True), 1e-30)
    return p, l


def _cw_body(q_ref, kc_ref, vc_ref, kw_ref, vw_ref, g_ref, ovt_ref, oa_ref, ns_ref):
    i = pl.program_id(1)
    t0 = i * QB
    q = q_ref[0]
    gt = g_ref[0].T
    kc = kc_ref[0]
    vc = vc_ref[0]
    ncp = kc.shape[0]
    wk = WIN_NSA + QB

    n_row = lax.broadcasted_iota(jnp.int32, (ncp, QB), 0)
    q_lane = lax.broadcasted_iota(jnp.int32, (ncp, QB), 1)
    vis = (t0 - (CMP_LEN - 1)) + q_lane - CMP_STRIDE * n_row >= 0
    cend = (CMP_STRIDE * n_row + (CMP_LEN - 1)).astype(F32)

    start = pl.multiple_of(jnp.maximum(t0 - WIN_NSA, 0), QB)
    kw = kw_ref[0, pl.ds(start, wk), :]
    vw = vw_ref[0, pl.ds(start, wk), :]
    j_row = lax.broadcasted_iota(jnp.int32, (wk, QB), 0)
    r_lane = lax.broadcasted_iota(jnp.int32, (wk, QB), 1)
    dw = (t0 - start) + r_lane - j_row
    wmask = (dw >= 0) & (dw < WIN_NSA)
    kposw = (start + j_row).astype(F32)

    psum = jnp.zeros((ncp, QB), F32)
    o_heads = []
    for h in range(H_NSA):
        slab = q[:, (h // 2) * LANES:(h // 2 + 1) * LANES]
        qm = _mask_half(slab, h % 2)
        sl = SLOPES_NSA[h] * LOG2E
        s = _nt_dot(kc, qm) + sl * cend
        s = jnp.where(vis, s, NEG)
        p, l = _softmax_cols(s, vis)
        pn = p * (1.0 / l)
        psum = psum + pn
        oc = _tn_dot(vc, (pn * gt[3 * h:3 * h + 1]).astype(BF16))
        s = _nt_dot(kw, qm) + sl * kposw
        s = jnp.where(wmask, s, NEG)
        p, l = _softmax_cols(s, wmask)
        ow = _tn_dot(vw, (p * (gt[3 * h + 2:3 * h + 3] / l)).astype(BF16))
        o_heads.append((oc + ow)[:HEAD_DIM])
    for hp in range(H_NSA // 2):
        oa_ref[0, :, hp * LANES:(hp + 1) * LANES] = jnp.concatenate(o_heads[2 * hp:2 * hp + 2], axis=0).T

    hi = psum.astype(BF16)
    lo = (psum - hi.astype(F32)).astype(BF16)
    imp = _dot(ovt_ref[...], hi) + _dot(ovt_ref[...], lo)
    j = lax.broadcasted_iota(jnp.int32, (SEL_LANES, QB), 0)
    qi = lax.broadcasted_iota(jnp.int32, (SEL_LANES, QB), 1)
    cur = jnp.right_shift(t0 + qi, int(math.log2(SEL_BLOCK)))
    forced = (j == 0) | (j == cur) | (j == cur - 1)
    imp = jnp.where(forced, imp + FORCE_BONUS, imp)
    imp = jnp.where(j <= cur, imp, NEG)
    jf = j.astype(F32)
    notsel = jnp.ones((SEL_LANES, QB), F32)
    for _ in range(SEL_TOPK):
        mx = jnp.max(imp, axis=0, keepdims=True)
        idx = jnp.min(jnp.where(imp == mx, jf, float(SEL_LANES)), axis=0, keepdims=True)
        hit = jf == idx
        notsel = jnp.where(hit, 0.0, notsel)
        imp = jnp.where(hit, -3e38, imp)
    ns_ref[0] = notsel.T.astype(BF16)


def _nsa_cmp_win(qa, kcr, vcr, kwkw, vwvw, gates, ovt):
    b, s, _ = qa.shape
    ncp = kcr.shape[1]
    return pl.pallas_call(
        _cw_body,
        grid=(b, s // QB),
        in_specs=[pl.BlockSpec((1, QB, 2 * LANES), lambda bi, i: (bi, i, 0)),
                  pl.BlockSpec((1, ncp, LANES), lambda bi, i: (bi, 0, 0)),
                  pl.BlockSpec((1, ncp, LANES), lambda bi, i: (bi, 0, 0)),
                  pl.BlockSpec((1, s, LANES), lambda bi, i: (bi, 0, 0)),
                  pl.BlockSpec((1, s, LANES), lambda bi, i: (bi, 0, 0)),
                  pl.BlockSpec((1, QB, LANES), lambda bi, i: (bi, i, 0)),
                  pl.BlockSpec((SEL_LANES, ncp), lambda bi, i: (0, 0))],
        out_specs=[pl.BlockSpec((1, QB, 2 * LANES), lambda bi, i: (bi, i, 0)),
                   pl.BlockSpec((1, QB, SEL_LANES), lambda bi, i: (bi, i, 0))],
        out_shape=[jax.ShapeDtypeStruct((b, s, 2 * LANES), F32),
                   jax.ShapeDtypeStruct((b, s, SEL_LANES), BF16)],
        compiler_params=_cparams("parallel", "parallel"),
        name="nsa_cmp_win",
    )(qa, kcr, vcr, kwkw, vwvw, gates, ovt)


def _sel_body(q_ref, ns_ref, g_ref, part_ref, ks_ref, vs_ref, oh_ref, oa_ref,
              kaug_ref, qaug_ref, m_ref, l_ref, acc_ref, *, tk):
    i = pl.program_id(1)
    t0 = i * QB
    nq = H_NSA * QB

    @pl.when(i == 0)
    def _():
        kaug_ref[:, 0:LANES] = ks_ref[0]
        kaug_ref[:, LANES:2 * LANES] = oh_ref[...]

    q = q_ref[0]
    ns = ns_ref[0]
    for h in range(H_NSA):
        slab = q[:, (h // 2) * LANES:(h // 2 + 1) * LANES]
        qaug_ref[h * QB:(h + 1) * QB, 0:LANES] = _mask_half(slab, h % 2)
        qaug_ref[h * QB:(h + 1) * QB, LANES:2 * LANES] = ns
    m_ref[...] = jnp.full(m_ref.shape, NEG, F32)
    l_ref[...] = jnp.zeros(l_ref.shape, F32)
    acc_ref[...] = jnp.zeros(acc_ref.shape, F32)

    key_row = lax.broadcasted_iota(jnp.int32, (tk, nq), 0)
    q_lane = lax.broadcasted_iota(jnp.int32, (tk, nq), 1)
    dmat = key_row - q_lane % QB
    rowf = key_row.astype(F32)
    head = lax.broadcasted_iota(jnp.int32, (1, nq), 1) // QB
    slope = jnp.zeros((1, nq), F32)
    for h in range(H_NSA):
        slope = jnp.where(head == h, SLOPES_NSA[h] * LOG2E, slope)

    def body(jt, carry):
        k0 = pl.multiple_of(jt * tk, tk)
        ka = kaug_ref[pl.ds(k0, tk), :]
        va = vs_ref[0, pl.ds(k0, tk), :]
        s = _nt_dot(ka, qaug_ref[...])
        s = s + slope * (rowf + k0.astype(F32))
        s = jnp.where(dmat <= t0 - k0, s, NEG)
        m_old = m_ref[...]
        m_new = jnp.maximum(m_old, jnp.max(s, axis=0, keepdims=True))
        alpha = jnp.exp2(m_old - m_new)
        p = jnp.exp2(s - m_new)
        l_ref[...] = alpha * l_ref[...] + jnp.sum(p, axis=0, keepdims=True)
        m_ref[...] = m_new
        acc_ref[...] = alpha * acc_ref[...] + _tn_dot(va, p.astype(BF16))
        return carry

    n_tiles = (t0 + QB + tk - 1) // tk
    lax.fori_loop(0, n_tiles, body, 0)

    gt = g_ref[0].T
    for hp in range(H_NSA // 2):
        rows = []
        for hh in range(2):
            h = 2 * hp + hh
            cs = slice(h * QB, (h + 1) * QB)
            rows.append(acc_ref[0:HEAD_DIM, cs] * (gt[3 * h + 1:3 * h + 2] / l_ref[:, cs]))
        cs = slice(hp * LANES, (hp + 1) * LANES)
        oa_ref[0, :, cs] = part_ref[0, :, cs] + jnp.concatenate(rows, axis=0).T


def _nsa_selected(qa, nsel, gates, part, ksks, vsvs, oh, tk=256):
    b, s, _ = qa.shape
    return pl.pallas_call(
        functools.partial(_sel_body, tk=tk),
        grid=(b, s // QB),
        in_specs=[pl.BlockSpec((1, QB, 2 * LANES), lambda bi, i: (bi, i, 0)),
                  pl.BlockSpec((1, QB, SEL_LANES), lambda bi, i: (bi, i, 0)),
                  pl.BlockSpec((1, QB, LANES), lambda bi, i: (bi, i, 0)),
                  pl.BlockSpec((1, QB, 2 * LANES), lambda bi, i: (bi, i, 0)),
                  pl.BlockSpec((1, s, LANES), lambda bi, i: (bi, 0, 0)),
                  pl.BlockSpec((1, s, LANES), lambda bi, i: (bi, 0, 0)),
                  pl.BlockSpec((s, SEL_LANES), lambda bi, i: (0, 0))],
        out_specs=pl.BlockSpec((1, QB, 2 * LANES), lambda bi, i: (bi, i, 0)),
        out_shape=jax.ShapeDtypeStruct((b, s, 2 * LANES), F32),
        scratch_shapes=[pltpu.VMEM((s, 2 * LANES), BF16),
                        pltpu.VMEM((H_NSA * QB, 2 * LANES), BF16),
                        pltpu.VMEM((1, H_NSA * QB), F32),
                        pltpu.VMEM((1, H_NSA * QB), F32),
                        pltpu.VMEM((LANES, H_NSA * QB), F32)],
        compiler_params=_cparams("arbitrary", "arbitrary"),
        name="nsa_selected",
    )(qa, nsel, gates, part, ksks, vsvs, oh)


def _dil_body(q_ref, kp_ref, kc_ref, vp_ref, vc_ref, bias_ref, o_ref, lse_ref):
    i = pl.program_id(2)
    q = q_ref[0]
    kk = jnp.concatenate([kp_ref[0], kc_ref[0]], axis=0)
    vv = jnp.concatenate([vp_ref[0], vc_ref[0]], axis=0)
    key_row = lax.broadcasted_iota(jnp.int32, (2 * QB, QB), 0)
    valid = key_row >= jnp.where(i > 0, 0, QB)
    outs, lses = [], []
    for hh in range(2):
        s = _nt_dot(kk, _mask_half(q, hh)) + bias_ref[hh]
        s = jnp.where(valid, s, NEG)
        m = jnp.max(s, axis=0, keepdims=True)
        p = jnp.exp2(s - m)
        l = jnp.sum(p, axis=0, keepdims=True)
        o = _tn_dot(vv, p.astype(BF16)) * (1.0 / l)
        outs.append(o[hh * HEAD_DIM:(hh + 1) * HEAD_DIM])
        lses.append(jnp.broadcast_to(m + jnp.log2(l), (HEAD_DIM, QB)))
    o_ref[0] = jnp.concatenate(outs, axis=0).T
    lse_ref[0] = jnp.concatenate(lses, axis=0).T


def _dilated(dq, dk, dv, bias, g, r):
    b, s, _ = dq.shape
    sr = s // r
    ng = len(DIL_PAIRS)
    qv, kv_, vv = (a.reshape(b, sr, r * ng * LANES) for a in (dq, dk, dv))
    cur = lambda bi, c, i: (bi, i, c * ng + g)
    prev = lambda bi, c, i: (bi, jnp.maximum(i - 1, 0), c * ng + g)
    o, lse = pl.pallas_call(
        _dil_body,
        grid=(b, r, sr // QB),
        in_specs=[pl.BlockSpec((1, QB, LANES), cur),
                  pl.BlockSpec((1, QB, LANES), prev), pl.BlockSpec((1, QB, LANES), cur),
                  pl.BlockSpec((1, QB, LANES), prev), pl.BlockSpec((1, QB, LANES), cur),
                  pl.BlockSpec((2, 2 * QB, QB), lambda bi, c, i: (0, 0, 0))],
        out_specs=[pl.BlockSpec((1, QB, LANES), lambda bi, c, i: (bi, i, c))] * 2,
        out_shape=[jax.ShapeDtypeStruct((b, sr, r * LANES), F32)] * 2,
        compiler_params=_cparams("parallel", "parallel", "parallel"),
        name=f"dilated_r{r}",
    )(qv, kv_, kv_, vv, vv, bias)
    return o.reshape(b, s, LANES), lse.reshape(b, s, LANES)


def _dil_bias(g):
    w, r = DIL_PAIRS[g]
    assert w // r == QB
    iq = np.arange(QB)[None, :]
    jk = np.arange(2 * QB)[:, None]
    dist = iq + QB - jk
    out = np.empty((2, 2 * QB, QB), np.float32)
    for hh in range(2):
        slope = SLOPES_DIL[2 * g + hh]
        out[hh] = np.where((dist >= 0) & (dist <= QB), -slope * LOG2E * r * dist, NEG)
    return jnp.asarray(out)


def _sb_body(q_ref, k_ref, v_ref, lt_ref, o_ref):
    i = pl.program_id(2)
    q = q_ref[0]
    lt = lt_ref[...]
    dmat = lax.broadcasted_iota(jnp.int32, (QB, QB), 0) - lax.broadcasted_iota(jnp.int32, (QB, QB), 1)
    accs = []
    for hh in range(2):
        qm = _mask_half(q, hh)

        def cond(c):
            jt, _, _, cmax = c
            return (jt >= 0) & (cmax > SB_UNDERFLOW)

        def body(c):
            jt, acc, carry, _ = c
            k0 = pl.multiple_of(jt * QB, QB)
            kt = k_ref[0, pl.ds(k0, QB), :]
            vt = v_ref[0, pl.ds(k0, QB), :]
            z = _nt_dot(kt, qm)
            lb = jnp.minimum(z, 0.0) - jnp.log1p(jnp.exp(-jnp.abs(z)))
            mask = dmat < jnp.where(jt < i, QB, 0)
            lf = jnp.where(mask, lb - z, 0.0)
            hi = lf.astype(BF16)
            lo = (lf - hi.astype(F32)).astype(BF16)
            later = _dot(lt, hi) + _dot(lt, lo)
            a = jnp.where(mask, jnp.exp(lb + later + carry), 0.0)
            acc = acc + _tn_dot(vt, a.astype(BF16))
            carry = carry + jnp.sum(lf, axis=0, keepdims=True)
            return jt - 1, acc, carry, jnp.max(carry)

        init = (i, jnp.zeros((LANES, QB), F32), jnp.zeros((1, QB), F32), jnp.float32(0.0))
        accs.append(lax.while_loop(cond, body, init)[1])
    o_ref[0] = jnp.concatenate([accs[0][:HEAD_DIM], accs[1][HEAD_DIM:]], axis=0).T


def _stick_breaking(sq, sk, sv, lt):
    b, s, _ = sq.shape
    npair = H_SB // 2
    return pl.pallas_call(
        _sb_body,
        grid=(b, npair, s // QB),
        in_specs=[pl.BlockSpec((1, QB, LANES), lambda bi, hp, i: (bi, i, hp)),
                  pl.BlockSpec((1, s, LANES), lambda bi, hp, i: (bi, 0, hp)),
                  pl.BlockSpec((1, s, LANES), lambda bi, hp, i: (bi, 0, hp)),
                  pl.BlockSpec((QB, QB), lambda bi, hp, i: (0, 0))],
        out_specs=pl.BlockSpec((1, QB, LANES), lambda bi, hp, i: (bi, i, hp)),
        out_shape=jax.ShapeDtypeStruct((b, s, npair * LANES), F32),
        compiler_params=_cparams("parallel", "parallel", "parallel"),
        name="stick_breaking",
    )(sq, sk, sv, lt)


def _out_body(x_ref, oa_ref, d0, l0, d1, l1, d2, l2, oc_ref, w_ref, o_ref):
    lses = [l0[...], l1[...], l2[...]]
    m = jnp.maximum(jnp.maximum(lses[0], lses[1]), lses[2])
    es = [jnp.exp2(l - m) for l in lses]
    den = es[0] + es[1] + es[2]
    ob = (es[0] * d0[...] + es[1] * d1[...] + es[2] * d2[...]) / den
    na = H_NSA * HEAD_DIM
    nb = na + LANES
    acc = _dot(oa_ref[...].astype(BF16), w_ref[0:na, :])
    acc = acc + _dot(ob.astype(BF16), w_ref[na:nb, :])
    acc = acc + _dot(oc_ref[...].astype(BF16), w_ref[nb:, :])
    o_ref[...] = x_ref[...] + acc


def _out_proj(x2d, oa, dil, oc, w_out, tm=512):
    n = x2d.shape[0]
    row = lambda w: pl.BlockSpec((tm, w), lambda i: (i, 0))
    d_cat = w_out.shape[0]
    ins = [x2d, oa]
    specs = [row(D_MODEL), row(2 * LANES)]
    for o, lse in dil:
        ins += [o, lse]
        specs += [row(LANES), row(LANES)]
    ins += [oc, w_out]
    specs += [row(oc.shape[1]), pl.BlockSpec((d_cat, D_MODEL), lambda i: (0, 0))]
    return pl.pallas_call(
        _out_body,
        grid=(n // tm,),
        in_specs=specs,
        out_specs=row(D_MODEL),
        out_shape=jax.ShapeDtypeStruct((n, D_MODEL), F32),
        compiler_params=_cparams("parallel"),
        name="out_proj",
    )(*ins)


def _mlp_body(x_ref, nw_ref, wu_ref, wd_ref, o_ref, *, fc):
    x = x_ref[...]
    ms = jnp.mean(x * x, axis=-1, keepdims=True)
    h = (x * lax.rsqrt(ms + RMS_EPS) * nw_ref[...]).astype(BF16)
    acc = x
    for c in range(D_FF // fc):
        u = jnp.maximum(_dot(h, wu_ref[:, c * fc:(c + 1) * fc]), 0.0)
        acc = acc + _dot((u * u).astype(BF16), wd_ref[c * fc:(c + 1) * fc, :])
    o_ref[...] = acc


def _mlp(x2d, nw, wu, wd, tm=512, fc=1024):
    n = x2d.shape[0]
    const = dict(pipeline_mode=pl.Buffered(1))
    return pl.pallas_call(
        functools.partial(_mlp_body, fc=fc),
        grid=(n // tm,),
        in_specs=[pl.BlockSpec((tm, D_MODEL), lambda i: (i, 0)),
                  pl.BlockSpec((1, D_MODEL), lambda i: (0, 0)),
                  pl.BlockSpec((D_MODEL, D_FF), lambda i: (0, 0), **const),
                  pl.BlockSpec((D_FF, D_MODEL), lambda i: (0, 0), **const)],
        out_specs=pl.BlockSpec((tm, D_MODEL), lambda i: (i, 0)),
        out_shape=jax.ShapeDtypeStruct((n, D_MODEL), F32),
        compiler_params=_cparams("parallel"),
        name="mlp",
    )(x2d, nw, wu, wd)


def _relayout_in_weight(w, g_nsa, g_dil):
    hd = HEAD_DIM
    kv0 = H_NSA * hd
    g0 = kv0 + 6 * hd
    b0 = g0 + 3 * H_NSA
    c0 = b0 + 3 * H_DIL * hd
    kv = lambda c: w[:, kv0 + c * hd:kv0 + (c + 1) * hd]
    gate = jnp.pad(w[:, g0:b0], ((0, 0), (0, LANES - 3 * H_NSA)))
    w_re = jnp.concatenate([w[:, :kv0], kv(0), kv(1), kv(2), kv(2), kv(4), kv(4), kv(3), kv(3), kv(5), kv(5),
                            gate, w[:, b0:c0], w[:, c0:]], axis=1).astype(BF16)
    one = lambda n: jnp.ones((n,), F32)
    cv = jnp.concatenate([
        jnp.tile(g_nsa[0], H_NSA) * (SCALE * LOG2E), one(LANES),
        jnp.tile(g_nsa[2], 2), jnp.tile(g_nsa[3], 2), one(3 * LANES),
        jnp.tile(g_dil[0], H_DIL) * (SCALE * LOG2E), jnp.tile(g_dil[1], H_DIL), one(H_DIL * hd),
        one(H_SB * hd) * SCALE, one(2 * H_SB * hd)])
    return w_re, cv.reshape(1, _IN_COLS)


def _relayout_cmp(w1, pe, w2, gk):
    hd, half = HEAD_DIM, CMP_LEN // 2
    w1r = w1.reshape(2, 2, half, hd, CMP_HIDDEN)
    per = pe.reshape(2, 2, half, hd)
    w1i, pei = [], []
    for c in range(2):
        pad = ((0, 0), (0, 0), (0, hd), (0, 0)) if c == 0 else ((0, 0), (0, 0), (hd, 0), (0, 0))
        w1i.append(jnp.pad(w1r[c], pad).reshape(2, half * LANES, CMP_HIDDEN))
        pei.append(jnp.pad(per[c], pad[:3]).reshape(2, half * LANES))
    w1i = jnp.concatenate(w1i, axis=0).astype(BF16)
    pe8 = jnp.pad(jnp.concatenate(pei, axis=0), ((0, 4), (0, 0)))
    w2r = jnp.concatenate([w2, w2], axis=-1).astype(BF16)
    return w1i, pe8, w2r, jnp.tile(gk, 2).reshape(1, LANES)


def _constants(s):
    ncp = s // CMP_STRIDE
    n_cmp = (s - CMP_LEN) // CMP_STRIDE + 1
    n = np.arange(ncp)[:, None]
    j = np.arange(SEL_LANES)[None, :]
    ov = ((CMP_STRIDE * n <= SEL_BLOCK * j + SEL_BLOCK - 1) & (CMP_STRIDE * n + CMP_LEN - 1 >= SEL_BLOCK * j)
          & (n < n_cmp) & (j < s // SEL_BLOCK))
    oh = np.where(np.arange(s)[:, None] // SEL_BLOCK == j, NEG, 0.0)
    a = np.arange(QB)
    lt = (a[None, :] > a[:, None]).astype(np.float32)
    gm = (a[:, None] // HEAD_DIM == a[None, :] // HEAD_DIM).astype(np.float32) / HEAD_DIM
    return dict(ovt=jnp.asarray(ov.T, BF16), oh=jnp.asarray(oh, BF16), lt=jnp.asarray(lt, BF16),
                gm=jnp.asarray(gm, BF16), dil_bias=[_dil_bias(g) for g in range(len(DIL_PAIRS))])


def kernel(x, norm_mix, norm_mlp, w_in, qk_gain_nsa, qk_gain_dil, cmp_pe, cmp_w1, cmp_w2, w_out, w_up, w_down):
    b, s, d = x.shape
    assert d == D_MODEL and s % (DIL_PAIRS[-1][1] * QB) == 0 and s // SEL_BLOCK <= SEL_LANES
    assert s >= WIN_NSA + QB
    n = b * s
    cst = _constants(s)
    x2d = x.reshape(n, d)
    for l in range(w_in.shape[0]):
        w_re, cv = _relayout_in_weight(w_in[l], qk_gain_nsa[l], qk_gain_dil[l])
        qa, kvc, ksks, kwkw, vsvs, vwvw, gates, dq, dk, dv, sq, sk, sv = _in_proj(
            x2d, norm_mix[l].reshape(1, d), w_re, cv, cst["gm"])
        tok = lambda a: a.reshape(b, s, a.shape[-1])
        w1i, pe8, w2r, gk = _relayout_cmp(cmp_w1[l], cmp_pe[l], cmp_w2[l], qk_gain_nsa[l, 1])
        kcr, vcr = _compress(kvc.reshape(b, s // CMP_STRIDE, CMP_STRIDE * LANES), w1i, pe8, w2r, gk)
        part, nsel = _nsa_cmp_win(tok(qa), kcr, vcr, tok(kwkw), tok(vwvw), tok(gates), cst["ovt"])
        oa = _nsa_selected(tok(qa), nsel, tok(gates), part, tok(ksks), tok(vsvs), cst["oh"])
        dil = []
        for g, (_, r) in enumerate(DIL_PAIRS):
            o, lse = _dilated(tok(dq), tok(dk), tok(dv), cst["dil_bias"][g], g, r)
            dil.append((o.reshape(n, LANES), lse.reshape(n, LANES)))
        oc = _stick_breaking(tok(sq), tok(sk), tok(sv), cst["lt"])
        x2d = _out_proj(x2d, oa.reshape(n, 2 * LANES), dil, oc.reshape(n, -1), w_out[l].astype(BF16))
        x2d = _mlp(x2d, norm_mlp[l].reshape(1, d), w_up[l].astype(BF16), w_down[l].astype(BF16))
    return x2d.reshape(b, s, d)
```

```python
import functools
import math

import numpy as np
import jax
import jax.numpy as jnp
from jax import lax
from jax.experimental import pallas as pl
from jax.experimental.pallas import tpu as pltpu

F32 = jnp.float32
BF16 = jnp.bfloat16

D_MODEL = 1024
HEAD_DIM = 64
H_NSA = 4
H_DIL = 6
H_SB = 6
DIL_PAIRS = ((128, 1), (512, 4), (2048, 16))
CMP_LEN = 32
CMP_STRIDE = 16
CMP_HIDDEN = 128
SEL_BLOCK = 64
SEL_TOPK = 16
WIN_NSA = 512
D_FF = 4 * D_MODEL
RMS_EPS = 1e-6
NEG = -1e30
FORCE_BONUS = 1e4
LOG2E = 1.4426950408889634
SCALE = HEAD_DIM ** -0.5
LANES = 128
QB = 128
SEL_LANES = 128
SB_UNDERFLOW = -104.0

_SLOPES = [2.0 ** (-8.0 * i / (H_NSA + H_DIL)) for i in range(1, H_NSA + H_DIL + 1)]
SLOPES_DIL = _SLOPES[:H_DIL]
SLOPES_NSA = _SLOPES[H_DIL:]

_IN_SEGS = (("qa", 2, "norm"), ("ksks", 1, "norm"), ("kwkw", 1, "norm"), ("dq", 3, "norm"), ("dk", 3, "norm"),
            ("kvc", 1, "raw"), ("vsvs", 1, "raw"), ("vwvw", 1, "raw"), ("gate", 1, "gate"),
            ("dv", 3, "raw"), ("sq", 3, "raw"), ("sk", 3, "raw"), ("sv", 3, "raw"))
_IN_COLS = sum(n for _, n, _ in _IN_SEGS) * LANES
_SLAB = 2 * LANES
_VMEM_LIMIT = 56 * 1024 * 1024


def _cparams(*sem, vmem=_VMEM_LIMIT):
    return pltpu.CompilerParams(dimension_semantics=sem, vmem_limit_bytes=vmem)


def _nt_dot(a, b):
    return lax.dot_general(a, b, (((1,), (1,)), ((), ())), preferred_element_type=F32)


def _dot(a, b):
    return jnp.dot(a, b, preferred_element_type=F32)


def _tn_dot(a, b):
    return lax.dot_general(a, b, (((0,), (0,)), ((), ())), preferred_element_type=F32)


def _split_dot(x, m):
    hi = x.astype(BF16)
    lo = (x - hi.astype(F32)).astype(BF16)
    return _dot(hi, m) + _dot(lo, m)


def _mask_half(x, hh):
    lane = lax.broadcasted_iota(jnp.int32, x.shape, x.ndim - 1)
    keep = (lane % LANES < HEAD_DIM) if hh == 0 else (lane % LANES >= HEAD_DIM)
    return jnp.where(keep, x, jnp.zeros_like(x))


def _in_proj_body(x_ref, nw_ref, w_ref, cv_ref, gm_ref, *out_refs):
    x = x_ref[...]
    ms = jnp.mean(x * x, axis=-1, keepdims=True)
    h = (x * lax.rsqrt(ms + RMS_EPS) * nw_ref[...]).astype(BF16)
    gm = gm_ref[...]
    tiles = [(o_ref, t, kind) for (name, ntile, kind), o_ref in zip(_IN_SEGS, out_refs) for t in range(ntile)]
    for sl in range(_IN_COLS // _SLAB):
        c0 = sl * _SLAB
        y = _dot(h, w_ref[:, c0:c0 + _SLAB])
        if tiles[2 * sl][2] == "norm":
            msq = _split_dot(y * y, gm)
            y = y * lax.rsqrt(msq + RMS_EPS)
        y = y * cv_ref[:, c0:c0 + _SLAB]
        for half in range(2):
            o_ref, t, kind = tiles[2 * sl + half]
            yh = y[:, half * LANES:(half + 1) * LANES]
            if kind == "gate":
                yh = jax.nn.sigmoid(yh)
            o_ref[:, t * LANES:(t + 1) * LANES] = yh.astype(o_ref.dtype)


def _in_proj(x2d, nw, w_re, cv, gm, tm=256):
    n = x2d.shape[0]
    out_shape, out_specs = [], []
    for name, ntile, kind in _IN_SEGS:
        dt = F32 if kind == "gate" else BF16
        out_shape.append(jax.ShapeDtypeStruct((n, ntile * LANES), dt))
        out_specs.append(pl.BlockSpec((tm, ntile * LANES), lambda i: (i, 0)))
    return pl.pallas_call(
        _in_proj_body,
        grid=(n // tm,),
        in_specs=[pl.BlockSpec((tm, D_MODEL), lambda i: (i, 0)),
                  pl.BlockSpec((1, D_MODEL), lambda i: (0, 0)),
                  pl.BlockSpec((D_MODEL, _IN_COLS), lambda i: (0, 0)),
                  pl.BlockSpec((1, _IN_COLS), lambda i: (0, 0)),
                  pl.BlockSpec((_SLAB, _SLAB), lambda i: (0, 0))],
        out_specs=out_specs,
        out_shape=out_shape,
        compiler_params=_cparams("parallel"),
        name="in_proj",
    )(x2d, nw, w_re, cv, gm)


def _gelu_tanh(x):
    return 0.5 * x * (1.0 + jnp.tanh(0.7978845608028654 * (x + 0.044715 * (x * x * x))))


def _cmp_body(x_ref, w1_ref, pe_ref, w2_ref, gk_ref, kc_ref, vc_ref):
    x = x_ref[0]
    ncp = x.shape[0]
    pe = pe_ref[...].astype(BF16)
    outs = []
    for c in range(2):
        top = _dot(x, w1_ref[2 * c])
        bot = _dot(x, w1_ref[2 * c + 1])
        bias = _dot(pe, w1_ref[2 * c])[2 * c:2 * c + 1] + _dot(pe, w1_ref[2 * c + 1])[2 * c + 1:2 * c + 2]
        hid = top + pltpu.roll(bot, ncp - 1, 0) + bias
        outs.append(_dot(_gelu_tanh(hid).astype(BF16), w2_ref[c]))
    kc = outs[0]
    kc = kc * lax.rsqrt(jnp.mean(kc * kc, axis=-1, keepdims=True) + RMS_EPS) * gk_ref[...]
    kc_ref[0] = kc.astype(BF16)
    vc_ref[0] = outs[1].astype(BF16)


def _compress(kvc16, w1i, pe8, w2r, gk):
    b, ncp, wid = kvc16.shape
    return pl.pallas_call(
        _cmp_body,
        grid=(b,),
        in_specs=[pl.BlockSpec((1, ncp, wid), lambda i: (i, 0, 0)),
                  pl.BlockSpec((4, wid, CMP_HIDDEN), lambda i: (0, 0, 0)),
                  pl.BlockSpec((8, wid), lambda i: (0, 0)),
                  pl.BlockSpec((2, CMP_HIDDEN, LANES), lambda i: (0, 0, 0)),
                  pl.BlockSpec((1, LANES), lambda i: (0, 0))],
        out_specs=[pl.BlockSpec((1, ncp, LANES), lambda i: (i, 0, 0)),
                   pl.BlockSpec((1, ncp, LANES), lambda i: (i, 0, 0))],
        out_shape=[jax.ShapeDtypeStruct((b, ncp, LANES), BF16)] * 2,
        compiler_params=_cparams("parallel"),
        name="nsa_compress",
    )(kvc16, w1i, pe8, w2r, gk)


def _softmax_cols(s, mask):
    m = jnp.max(s, axis=0, keepdims=True)
    p = jnp.where(mask, jnp.exp2(s - m), 0.0)
    l = jnp.maximum(jnp.sum(p, axis=0, keepdims=True), 1e-30)
    return p, l


def _cw_body(q_ref, kc_ref, vc_ref, kw_ref, vw_ref, g_ref, ovt_ref, oa_ref, ns_ref, fl_ref):
    i = pl.program_id(1)
    t0 = i * QB
    q = q_ref[0]
    gt = g_ref[0].T
    kc = kc_ref[0]
    vc = vc_ref[0]
    ncp = kc.shape[0]
    wk = WIN_NSA + QB

    n_row = lax.broadcasted_iota(jnp.int32, (ncp, QB), 0)
    q_lane = lax.broadcasted_iota(jnp.int32, (ncp, QB), 1)
    vis = (t0 - (CMP_LEN - 1)) + q_lane - CMP_STRIDE * n_row >= 0
    cend = (CMP_STRIDE * n_row + (CMP_LEN - 1)).astype(F32)

    start = pl.multiple_of(jnp.maximum(t0 - WIN_NSA, 0), QB)
    kw = kw_ref[0, pl.ds(start, wk), :]
    vw = vw_ref[0, pl.ds(start, wk), :]
    j_row = lax.broadcasted_iota(jnp.int32, (wk, QB), 0)
    r_lane = lax.broadcasted_iota(jnp.int32, (wk, QB), 1)
    dw = (t0 - start) + r_lane - j_row
    wmask = (dw >= 0) & (dw < WIN_NSA)
    kposw = (start + j_row).astype(F32)

    psum = jnp.zeros((ncp, QB), F32)
    o_heads = []
    for h in range(H_NSA):
        slab = q[:, (h // 2) * LANES:(h // 2 + 1) * LANES]
        qm = _mask_half(slab, h % 2)
        sl = SLOPES_NSA[h] * LOG2E
        s = _nt_dot(kc, qm) + sl * cend
        s = jnp.where(vis, s, NEG)
        p, l = _softmax_cols(s, vis)
        pn = p * (1.0 / l)
        psum = psum + pn
        oc = _tn_dot(vc, (pn * gt[3 * h:3 * h + 1]).astype(BF16))
        s = _nt_dot(kw, qm) + sl * kposw
        s = jnp.where(wmask, s, NEG)
        p, l = _softmax_cols(s, wmask)
        ow = _tn_dot(vw, (p * (gt[3 * h + 2:3 * h + 3] / l)).astype(BF16))
        o_heads.append((oc + ow)[:HEAD_DIM])
    for hp in range(H_NSA // 2):
        oa_ref[0, :, hp * LANES:(hp + 1) * LANES] = jnp.concatenate(o_heads[2 * hp:2 * hp + 2], axis=0).T

    hi = psum.astype(BF16)
    lo = (psum - hi.astype(F32)).astype(BF16)
    imp = _dot(ovt_ref[...], hi) + _dot(ovt_ref[...], lo)
    j = lax.broadcasted_iota(jnp.int32, (SEL_LANES, QB), 0)
    qi = lax.broadcasted_iota(jnp.int32, (SEL_LANES, QB), 1)
    cur = jnp.right_shift(t0 + qi, int(math.log2(SEL_BLOCK)))
    forced = (j == 0) | (j == cur) | (j == cur - 1)
    imp = jnp.where(forced, imp + FORCE_BONUS, imp)
    imp = jnp.where(j <= cur, imp, NEG)
    jf = j.astype(F32)
    notsel = jnp.ones((SEL_LANES, QB), F32)
    for _ in range(SEL_TOPK):
        mx = jnp.max(imp, axis=0, keepdims=True)
        idx = jnp.min(jnp.where(imp == mx, jf, float(SEL_LANES)), axis=0, keepdims=True)
        hit = jf == idx
        notsel = jnp.where(hit, 0.0, notsel)
        imp = jnp.where(hit, -3e38, imp)
    nst = notsel.T
    ns_ref[0] = nst.astype(BF16)
    used = 1.0 - jnp.min(nst, axis=0, keepdims=True)
    fl_ref[0, 0] = jnp.broadcast_to(used, (8, SEL_LANES)).astype(jnp.int32)


def _nsa_cmp_win(qa, kcr, vcr, kwkw, vwvw, gates, ovt):
    b, s, _ = qa.shape
    ncp = kcr.shape[1]
    return pl.pallas_call(
        _cw_body,
        grid=(b, s // QB),
        in_specs=[pl.BlockSpec((1, QB, 2 * LANES), lambda bi, i: (bi, i, 0)),
                  pl.BlockSpec((1, ncp, LANES), lambda bi, i: (bi, 0, 0)),
                  pl.BlockSpec((1, ncp, LANES), lambda bi, i: (bi, 0, 0)),
                  pl.BlockSpec((1, s, LANES), lambda bi, i: (bi, 0, 0)),
                  pl.BlockSpec((1, s, LANES), lambda bi, i: (bi, 0, 0)),
                  pl.BlockSpec((1, QB, LANES), lambda bi, i: (bi, i, 0)),
                  pl.BlockSpec((SEL_LANES, ncp), lambda bi, i: (0, 0))],
        out_specs=[pl.BlockSpec((1, QB, 2 * LANES), lambda bi, i: (bi, i, 0)),
                   pl.BlockSpec((1, QB, SEL_LANES), lambda bi, i: (bi, i, 0)),
                   pl.BlockSpec((1, 1, 8, SEL_LANES), lambda bi, i: (bi, i, 0, 0))],
        out_shape=[jax.ShapeDtypeStruct((b, s, 2 * LANES), F32),
                   jax.ShapeDtypeStruct((b, s, SEL_LANES), BF16),
                   jax.ShapeDtypeStruct((b, s // QB, 8, SEL_LANES), jnp.int32)],
        compiler_params=_cparams("parallel", "parallel"),
        name="nsa_cmp_win",
    )(qa, kcr, vcr, kwkw, vwvw, gates, ovt)


def _sel_body(fl_ref, q_ref, ns_ref, g_ref, part_ref, ks_ref, vs_ref, kc_ref, sc_ref, oa_ref,
              kaug_ref, vaug_ref, qaug_ref, m_ref, acc_ref, sa_ref, sb_ref, tl_ref, *, tk):
    i = pl.program_id(1)
    t0 = i * QB
    nq = H_NSA * QB
    last_tile = kaug_ref.shape[0] // tk - 1

    @pl.when(i == 0)
    def _():
        lane = lax.broadcasted_iota(jnp.int32, (kaug_ref.shape[0], LANES), 1)
        kaug_ref[:, 0:LANES] = jnp.where(lane < HEAD_DIM, ks_ref[0], kc_ref[:, 0:LANES])
        kaug_ref[:, LANES:2 * LANES] = kc_ref[:, LANES:2 * LANES]
        vaug_ref[...] = jnp.where(lane < HEAD_DIM, vs_ref[0], jnp.ones((), BF16))

    q = q_ref[0]
    ns = ns_ref[0]
    lane = lax.broadcasted_iota(jnp.int32, (QB, LANES), 1)
    for h in range(H_NSA):
        slab = q[:, (h // 2) * LANES:(h // 2 + 1) * LANES].astype(F32)
        if h % 2:
            slab = pltpu.roll(slab, HEAD_DIM, 1)
        qaug_ref[h * QB:(h + 1) * QB, 0:LANES] = jnp.where(lane < HEAD_DIM, slab, sc_ref[h:h + 1, :]).astype(BF16)
        qaug_ref[h * QB:(h + 1) * QB, LANES:2 * LANES] = ns
    m_ref[...] = jnp.full(m_ref.shape, NEG, F32)
    acc_ref[...] = jnp.zeros(acc_ref.shape, F32)

    key_row = lax.broadcasted_iota(jnp.int32, (tk, nq), 0)
    q_lane = lax.broadcasted_iota(jnp.int32, (tk, nq), 1)
    dmat = key_row - q_lane % QB

    def scores(jt):
        k0 = pl.multiple_of(jnp.minimum(jt, last_tile) * tk, tk)
        return _nt_dot(kaug_ref[pl.ds(k0, tk), :], qaug_ref[...])

    def update(s_ref, jt, diagonal):
        k0 = pl.multiple_of(jnp.minimum(jt, last_tile) * tk, tk)
        s = s_ref[...]
        if diagonal:
            s = jnp.where(dmat <= t0 - jt * tk, s, NEG)
        m_old = m_ref[...]
        m_new = jnp.maximum(m_old, jnp.max(s, axis=0, keepdims=True))
        alpha = jnp.exp2(m_old - m_new)
        p = jnp.exp2(s - m_new)
        m_ref[...] = m_new
        pv = _tn_dot(vaug_ref[pl.ds(k0, tk), :], p.astype(BF16))
        acc_ref[...] = alpha * acc_ref[...] + pv[:acc_ref.shape[0]]

    blocks_per_tile = tk // SEL_BLOCK
    diag_tile = t0 // tk
    frow = pl.program_id(0) * pl.num_programs(1) + i

    def collect(jt, cnt):
        used = fl_ref[frow, jt * blocks_per_tile]
        for c in range(1, blocks_per_tile):
            used = used | fl_ref[frow, jt * blocks_per_tile + c]
        tl_ref[cnt] = jt
        return cnt + used

    n_before = lax.fori_loop(0, diag_tile, collect, 0)
    tl_ref[n_before] = diag_tile
    tl_ref[n_before + 1] = 2 * (last_tile + 1)

    def body(jj, carry):
        sb_ref[...] = scores(tl_ref[2 * jj + 1])
        update(sa_ref, tl_ref[2 * jj], False)
        sa_ref[...] = scores(tl_ref[2 * jj + 2])
        update(sb_ref, tl_ref[2 * jj + 1], False)
        return carry

    n_pairs = n_before // 2
    sa_ref[...] = scores(tl_ref[0])
    lax.fori_loop(0, n_pairs, body, 0)
    sb_ref[...] = scores(tl_ref[2 * n_pairs + 1])
    update(sa_ref, tl_ref[2 * n_pairs], True)
    update(sb_ref, tl_ref[2 * n_pairs + 1], True)

    gt = g_ref[0].T
    for hp in range(H_NSA // 2):
        rows = []
        for hh in range(2):
            h = 2 * hp + hh
            cs = slice(h * QB, (h + 1) * QB)
            rows.append(acc_ref[0:HEAD_DIM, cs] * (gt[3 * h + 1:3 * h + 2] / acc_ref[HEAD_DIM:HEAD_DIM + 1, cs]))
        cs = slice(hp * LANES, (hp + 1) * LANES)
        oa_ref[0, :, cs] = part_ref[0, :, cs] + jnp.concatenate(rows, axis=0).T


def _nsa_selected(flags, qa, nsel, gates, part, ksks, vsvs, kconst, sconst, tk=256):
    b, s, _ = qa.shape
    nq = H_NSA * QB
    grid_spec = pltpu.PrefetchScalarGridSpec(
        num_scalar_prefetch=1,
        grid=(b, s // QB),
        in_specs=[pl.BlockSpec((1, QB, 2 * LANES), lambda bi, i, fl: (bi, i, 0)),
                  pl.BlockSpec((1, QB, SEL_LANES), lambda bi, i, fl: (bi, i, 0)),
                  pl.BlockSpec((1, QB, LANES), lambda bi, i, fl: (bi, i, 0)),
                  pl.BlockSpec((1, QB, 2 * LANES), lambda bi, i, fl: (bi, i, 0)),
                  pl.BlockSpec((1, s, LANES), lambda bi, i, fl: (bi, 0, 0)),
                  pl.BlockSpec((1, s, LANES), lambda bi, i, fl: (bi, 0, 0)),
                  pl.BlockSpec((s, 2 * LANES), lambda bi, i, fl: (0, 0)),
                  pl.BlockSpec((8, LANES), lambda bi, i, fl: (0, 0))],
        out_specs=pl.BlockSpec((1, QB, 2 * LANES), lambda bi, i, fl: (bi, i, 0)),
        scratch_shapes=[pltpu.VMEM((s, 2 * LANES), BF16),
                        pltpu.VMEM((s, LANES), BF16),
                        pltpu.VMEM((nq, 2 * LANES), BF16),
                        pltpu.VMEM((1, nq), F32),
                        pltpu.VMEM((HEAD_DIM + 8, nq), F32),
                        pltpu.VMEM((tk, nq), F32),
                        pltpu.VMEM((tk, nq), F32),
                        pltpu.SMEM((s // tk + 8,), jnp.int32)])
    return pl.pallas_call(
        functools.partial(_sel_body, tk=tk),
        grid_spec=grid_spec,
        out_shape=jax.ShapeDtypeStruct((b, s, 2 * LANES), F32),
        compiler_params=_cparams("arbitrary", "arbitrary"),
        name="nsa_selected",
    )(flags, qa, nsel, gates, part, ksks, vsvs, kconst, sconst)


def _dil_body(q_ref, kp_ref, kc_ref, vp_ref, vc_ref, bias_ref, o_ref, lse_ref):
    i = pl.program_id(2)
    q = q_ref[0]
    kk = jnp.concatenate([kp_ref[0], kc_ref[0]], axis=0)
    vv = jnp.concatenate([vp_ref[0], vc_ref[0]], axis=0)
    key_row = lax.broadcasted_iota(jnp.int32, (2 * QB, QB), 0)
    valid = key_row >= jnp.where(i > 0, 0, QB)
    outs, lses = [], []
    for hh in range(2):
        s = _nt_dot(kk, _mask_half(q, hh)) + bias_ref[hh]
        s = jnp.where(valid, s, NEG)
        m = jnp.max(s, axis=0, keepdims=True)
        p = jnp.exp2(s - m)
        l = jnp.sum(p, axis=0, keepdims=True)
        o = _tn_dot(vv, p.astype(BF16)) * (1.0 / l)
        outs.append(o[hh * HEAD_DIM:(hh + 1) * HEAD_DIM])
        lses.append(jnp.broadcast_to(m + jnp.log2(l), (HEAD_DIM, QB)))
    o_ref[0] = jnp.concatenate(outs, axis=0).T
    lse_ref[0] = jnp.concatenate(lses, axis=0).T


def _dilated(dq, dk, dv, bias, g, r):
    b, s, _ = dq.shape
    sr = s // r
    ng = len(DIL_PAIRS)
    qv, kv_, vv = (a.reshape(b, sr, r * ng * LANES) for a in (dq, dk, dv))
    cur = lambda bi, c, i: (bi, i, c * ng + g)
    prev = lambda bi, c, i: (bi, jnp.maximum(i - 1, 0), c * ng + g)
    o, lse = pl.pallas_call(
        _dil_body,
        grid=(b, r, sr // QB),
        in_specs=[pl.BlockSpec((1, QB, LANES), cur),
                  pl.BlockSpec((1, QB, LANES), prev), pl.BlockSpec((1, QB, LANES), cur),
                  pl.BlockSpec((1, QB, LANES), prev), pl.BlockSpec((1, QB, LANES), cur),
                  pl.BlockSpec((2, 2 * QB, QB), lambda bi, c, i: (0, 0, 0))],
        out_specs=[pl.BlockSpec((1, QB, LANES), lambda bi, c, i: (bi, i, c))] * 2,
        out_shape=[jax.ShapeDtypeStruct((b, sr, r * LANES), F32)] * 2,
        compiler_params=_cparams("parallel", "parallel", "parallel"),
        name=f"dilated_r{r}",
    )(qv, kv_, kv_, vv, vv, bias)
    return o.reshape(b, s, LANES), lse.reshape(b, s, LANES)


def _dil_bias(g):
    w, r = DIL_PAIRS[g]
    assert w // r == QB
    iq = np.arange(QB)[None, :]
    jk = np.arange(2 * QB)[:, None]
    dist = iq + QB - jk
    out = np.empty((2, 2 * QB, QB), np.float32)
    for hh in range(2):
        slope = SLOPES_DIL[2 * g + hh]
        out[hh] = np.where((dist >= 0) & (dist <= QB), -slope * LOG2E * r * dist, NEG)
    return jnp.asarray(out)


def _sb_body(q_ref, k_ref, v_ref, lt_ref, o_ref, acc_ref):
    i = pl.program_id(2)
    tq = q_ref.shape[1]
    q = q_ref[0]
    lt = lt_ref[...]
    dmat = lax.broadcasted_iota(jnp.int32, (tq, tq), 0) - lax.broadcasted_iota(jnp.int32, (tq, tq), 1)
    qms = [_mask_half(q, hh) for hh in range(2)]
    acc_ref[...] = jnp.zeros(acc_ref.shape, F32)

    def cond(c):
        jt, _, _, cmax = c
        return (jt >= 0) & (cmax > SB_UNDERFLOW)

    def body(c):
        jt, carry0, carry1, _ = c
        k0 = pl.multiple_of(jt * tq, tq)
        kt = k_ref[0, pl.ds(k0, tq), :]
        vt = v_ref[0, pl.ds(k0, tq), :]
        mask = dmat < jnp.where(jt < i, tq, 0)
        carries = []
        for hh, carry in enumerate((carry0, carry1)):
            z = _nt_dot(kt, qms[hh])
            lb = jnp.minimum(z, 0.0) - jnp.log(1.0 + jnp.exp(-jnp.abs(z)))
            lf = jnp.where(mask, lb - z, 0.0)
            hi = lf.astype(BF16)
            lo = (lf - hi.astype(F32)).astype(BF16)
            later = _dot(lt, hi) + _dot(lt, lo)
            a = jnp.where(mask, jnp.exp(lb + later + carry), 0.0)
            acc_ref[hh] += _tn_dot(vt, a.astype(BF16))[hh * HEAD_DIM:(hh + 1) * HEAD_DIM]
            carries.append(carry + jnp.sum(lf, axis=0, keepdims=True))
        cmax = jnp.maximum(jnp.max(carries[0]), jnp.max(carries[1]))
        return jt - 1, carries[0], carries[1], cmax

    zero = jnp.zeros((1, tq), F32)
    lax.while_loop(cond, body, (i, zero, zero, jnp.float32(0.0)))
    o_ref[0] = jnp.concatenate([acc_ref[0], acc_ref[1]], axis=0).T


def _stick_breaking(sq, sk, sv, lt):
    b, s, _ = sq.shape
    npair = H_SB // 2
    tq = lt.shape[0]
    return pl.pallas_call(
        _sb_body,
        grid=(b, npair, s // tq),
        in_specs=[pl.BlockSpec((1, tq, LANES), lambda bi, hp, i: (bi, i, hp)),
                  pl.BlockSpec((1, s, LANES), lambda bi, hp, i: (bi, 0, hp)),
                  pl.BlockSpec((1, s, LANES), lambda bi, hp, i: (bi, 0, hp)),
                  pl.BlockSpec((tq, tq), lambda bi, hp, i: (0, 0))],
        out_specs=pl.BlockSpec((1, tq, LANES), lambda bi, hp, i: (bi, i, hp)),
        out_shape=jax.ShapeDtypeStruct((b, s, npair * LANES), F32),
        scratch_shapes=[pltpu.VMEM((2, HEAD_DIM, tq), F32)],
        compiler_params=_cparams("parallel", "parallel", "parallel"),
        name="stick_breaking",
    )(sq, sk, sv, lt)


def _out_body(x_ref, oa_ref, d0, l0, d1, l1, d2, l2, oc_ref, w_ref, o_ref):
    lses = [l0[...], l1[...], l2[...]]
    m = jnp.maximum(jnp.maximum(lses[0], lses[1]), lses[2])
    es = [jnp.exp2(l - m) for l in lses]
    den = es[0] + es[1] + es[2]
    ob = (es[0] * d0[...] + es[1] * d1[...] + es[2] * d2[...]) / den
    na = H_NSA * HEAD_DIM
    nb = na + LANES
    acc = _dot(oa_ref[...].astype(BF16), w_ref[0:na, :])
    acc = acc + _dot(ob.astype(BF16), w_ref[na:nb, :])
    acc = acc + _dot(oc_ref[...].astype(BF16), w_ref[nb:, :])
    o_ref[...] = x_ref[...] + acc


def _out_proj(x2d, oa, dil, oc, w_out, tm=512):
    n = x2d.shape[0]
    row = lambda w: pl.BlockSpec((tm, w), lambda i: (i, 0))
    d_cat = w_out.shape[0]
    ins = [x2d, oa]
    specs = [row(D_MODEL), row(2 * LANES)]
    for o, lse in dil:
        ins += [o, lse]
        specs += [row(LANES), row(LANES)]
    ins += [oc, w_out]
    specs += [row(oc.shape[1]), pl.BlockSpec((d_cat, D_MODEL), lambda i: (0, 0))]
    return pl.pallas_call(
        _out_body,
        grid=(n // tm,),
        in_specs=specs,
        out_specs=row(D_MODEL),
        out_shape=jax.ShapeDtypeStruct((n, D_MODEL), F32),
        compiler_params=_cparams("parallel"),
        name="out_proj",
    )(*ins)


def _mlp_body(x_ref, nw_ref, wu_ref, wd_ref, o_ref, *, fc):
    x = x_ref[...]
    ms = jnp.mean(x * x, axis=-1, keepdims=True)
    h = (x * lax.rsqrt(ms + RMS_EPS) * nw_ref[...]).astype(BF16)
    acc = x
    for c in range(D_FF // fc):
        u = jnp.maximum(_dot(h, wu_ref[:, c * fc:(c + 1) * fc]), 0.0)
        acc = acc + _dot((u * u).astype(BF16), wd_ref[c * fc:(c + 1) * fc, :])
    o_ref[...] = acc


def _mlp(x2d, nw, wu, wd, tm=512, fc=1024):
    n = x2d.shape[0]
    const = dict(pipeline_mode=pl.Buffered(1))
    return pl.pallas_call(
        functools.partial(_mlp_body, fc=fc),
        grid=(n // tm,),
        in_specs=[pl.BlockSpec((tm, D_MODEL), lambda i: (i, 0)),
                  pl.BlockSpec((1, D_MODEL), lambda i: (0, 0)),
                  pl.BlockSpec((D_MODEL, D_FF), lambda i: (0, 0), **const),
                  pl.BlockSpec((D_FF, D_MODEL), lambda i: (0, 0), **const)],
        out_specs=pl.BlockSpec((tm, D_MODEL), lambda i: (i, 0)),
        out_shape=jax.ShapeDtypeStruct((n, D_MODEL), F32),
        compiler_params=_cparams("parallel"),
        name="mlp",
    )(x2d, nw, wu, wd)


def _relayout_in_weight(w, g_nsa, g_dil):
    hd = HEAD_DIM
    kv0 = H_NSA * hd
    g0 = kv0 + 6 * hd
    b0 = g0 + 3 * H_NSA
    c0 = b0 + 3 * H_DIL * hd
    kv = lambda c: w[:, kv0 + c * hd:kv0 + (c + 1) * hd]
    gate = jnp.pad(w[:, g0:b0], ((0, 0), (0, LANES - 3 * H_NSA)))
    nd = H_DIL * hd
    w_re = jnp.concatenate([w[:, :kv0], kv(2), kv(2), kv(4), kv(4), w[:, b0:b0 + 2 * nd],
                            kv(0), kv(1), kv(3), kv(3), kv(5), kv(5), gate, w[:, b0 + 2 * nd:c0], w[:, c0:]],
                           axis=1).astype(BF16)
    one = lambda n: jnp.ones((n,), F32)
    cv = jnp.concatenate([
        jnp.tile(g_nsa[0], H_NSA) * (SCALE * LOG2E), jnp.tile(g_nsa[2], 2), jnp.tile(g_nsa[3], 2),
        jnp.tile(g_dil[0], H_DIL) * (SCALE * LOG2E), jnp.tile(g_dil[1], H_DIL),
        one(4 * LANES), one(nd), one(H_SB * hd) * SCALE, one(2 * H_SB * hd)])
    return w_re, cv.reshape(1, _IN_COLS)


def _relayout_cmp(w1, pe, w2, gk):
    hd, half = HEAD_DIM, CMP_LEN // 2
    w1r = w1.reshape(2, 2, half, hd, CMP_HIDDEN)
    per = pe.reshape(2, 2, half, hd)
    w1i, pei = [], []
    for c in range(2):
        pad = ((0, 0), (0, 0), (0, hd), (0, 0)) if c == 0 else ((0, 0), (0, 0), (hd, 0), (0, 0))
        w1i.append(jnp.pad(w1r[c], pad).reshape(2, half * LANES, CMP_HIDDEN))
        pei.append(jnp.pad(per[c], pad[:3]).reshape(2, half * LANES))
    w1i = jnp.concatenate(w1i, axis=0).astype(BF16)
    pe8 = jnp.pad(jnp.concatenate(pei, axis=0), ((0, 4), (0, 0)))
    w2r = jnp.concatenate([w2, w2], axis=-1).astype(BF16)
    return w1i, pe8, w2r, jnp.tile(gk, 2).reshape(1, LANES)


def _constants(s):
    ncp = s // CMP_STRIDE
    n_cmp = (s - CMP_LEN) // CMP_STRIDE + 1
    n = np.arange(ncp)[:, None]
    j = np.arange(SEL_LANES)[None, :]
    ov = ((CMP_STRIDE * n <= SEL_BLOCK * j + SEL_BLOCK - 1) & (CMP_STRIDE * n + CMP_LEN - 1 >= SEL_BLOCK * j)
          & (n < n_cmp) & (j < s // SEL_BLOCK))
    pos = np.arange(s)
    kconst = np.zeros((s, 2 * LANES), np.float32)
    kconst[:, HEAD_DIM:HEAD_DIM + 3] = (pos // SEL_BLOCK * SEL_BLOCK)[:, None]
    kconst[:, HEAD_DIM + 3:HEAD_DIM + 6] = (pos % SEL_BLOCK)[:, None]
    kconst[:, LANES:] = np.where(pos[:, None] // SEL_BLOCK == j, NEG, 0.0)
    sconst = np.zeros((8, LANES), np.float32)
    for h in range(H_NSA):
        rest = np.float32(SLOPES_NSA[h] * LOG2E)
        for c in range(3):
            piece = rest.astype(BF16).astype(np.float32)
            sconst[h, HEAD_DIM + c] = sconst[h, HEAD_DIM + 3 + c] = piece
            rest = np.float32(rest - piece)
    a2 = np.arange(2 * QB)
    lt = (a2[None, :] > a2[:, None]).astype(np.float32)
    a = np.arange(_SLAB)
    gm = (a[:, None] // HEAD_DIM == a[None, :] // HEAD_DIM).astype(np.float32) / HEAD_DIM
    return dict(ovt=jnp.asarray(ov.T, BF16), kconst=jnp.asarray(kconst, BF16), sconst=jnp.asarray(sconst),
                lt=jnp.asarray(lt, BF16), gm=jnp.asarray(gm, BF16),
                dil_bias=[_dil_bias(g) for g in range(len(DIL_PAIRS))])


def kernel(x, norm_mix, norm_mlp, w_in, qk_gain_nsa, qk_gain_dil, cmp_pe, cmp_w1, cmp_w2, w_out, w_up, w_down):
    b, s, d = x.shape
    assert d == D_MODEL and s % (DIL_PAIRS[-1][1] * QB) == 0 and s // SEL_BLOCK <= SEL_LANES
    assert s >= WIN_NSA + QB
    n = b * s
    cst = _constants(s)
    x2d = x.reshape(n, d)
    for l in range(w_in.shape[0]):
        w_re, cv = _relayout_in_weight(w_in[l], qk_gain_nsa[l], qk_gain_dil[l])
        qa, ksks, kwkw, dq, dk, kvc, vsvs, vwvw, gates, dv, sq, sk, sv = _in_proj(
            x2d, norm_mix[l].reshape(1, d), w_re, cv, cst["gm"])
        tok = lambda a: a.reshape(b, s, a.shape[-1])
        w1i, pe8, w2r, gk = _relayout_cmp(cmp_w1[l], cmp_pe[l], cmp_w2[l], qk_gain_nsa[l, 1])
        kcr, vcr = _compress(kvc.reshape(b, s // CMP_STRIDE, CMP_STRIDE * LANES), w1i, pe8, w2r, gk)
        part, nsel, used = _nsa_cmp_win(tok(qa), kcr, vcr, tok(kwkw), tok(vwvw), tok(gates), cst["ovt"])
        oa = _nsa_selected(used[:, :, 0, :].reshape(n // QB, SEL_LANES), tok(qa), nsel, tok(gates), part,
                           tok(ksks), tok(vsvs), cst["kconst"], cst["sconst"])
        dil = []
        for g, (_, r) in enumerate(DIL_PAIRS):
            o, lse = _dilated(tok(dq), tok(dk), tok(dv), cst["dil_bias"][g], g, r)
            dil.append((o.reshape(n, LANES), lse.reshape(n, LANES)))
        oc = _stick_breaking(tok(sq), tok(sk), tok(sv), cst["lt"])
        x2d = _out_proj(x2d, oa.reshape(n, 2 * LANES), dil, oc.reshape(n, -1), w_out[l].astype(BF16))
        x2d = _mlp(x2d, norm_mlp[l].reshape(1, d), w_up[l].astype(BF16), w_down[l].astype(BF16))
    return x2d.reshape(b, s, d)
```

```python
import functools
import math

import numpy as np
import jax
import jax.numpy as jnp
from jax import lax
from jax.experimental import pallas as pl
from jax.experimental.pallas import tpu as pltpu

F32 = jnp.float32
BF16 = jnp.bfloat16

D_MODEL = 1024
HEAD_DIM = 64
H_NSA = 4
H_DIL = 6
H_SB = 6
DIL_PAIRS = ((128, 1), (512, 4), (2048, 16))
CMP_LEN = 32
CMP_STRIDE = 16
CMP_HIDDEN = 128
SEL_BLOCK = 64
SEL_TOPK = 16
WIN_NSA = 512
D_FF = 4 * D_MODEL
RMS_EPS = 1e-6
NEG = -1e30
FORCE_BONUS = 1e4
LOG2E = 1.4426950408889634
SCALE = HEAD_DIM ** -0.5
LANES = 128
QB = 128
SEL_LANES = 128
SB_UNDERFLOW = -104.0

_SLOPES = [2.0 ** (-8.0 * i / (H_NSA + H_DIL)) for i in range(1, H_NSA + H_DIL + 1)]
SLOPES_DIL = _SLOPES[:H_DIL]
SLOPES_NSA = _SLOPES[H_DIL:]

_IN_SEGS = (("qa", 2, "norm"), ("ksks", 1, "norm"), ("kwkw", 1, "norm"), ("dq", 3, "norm"), ("dk", 3, "norm"),
            ("kvc", 1, "raw"), ("vsvs", 1, "raw"), ("vwvw", 1, "raw"), ("gate", 1, "gate"),
            ("dv", 3, "raw"), ("sq", 3, "raw"), ("sk", 3, "raw"), ("sv", 3, "raw"))
_IN_COLS = sum(n for _, n, _ in _IN_SEGS) * LANES
_SLAB = 2 * LANES
_F32_SEGS = ("kvc", "dq", "dk", "dv")
_VMEM_LIMIT = 56 * 1024 * 1024


def _cparams(*sem, vmem=_VMEM_LIMIT):
    return pltpu.CompilerParams(dimension_semantics=sem, vmem_limit_bytes=vmem)


def _nt_dot(a, b):
    return lax.dot_general(a, b, (((1,), (1,)), ((), ())), preferred_element_type=F32)


def _dot(a, b):
    return jnp.dot(a, b, preferred_element_type=F32)


def _tn_dot(a, b):
    return lax.dot_general(a, b, (((0,), (0,)), ((), ())), preferred_element_type=F32)


def _split_dot(x, m):
    hi = x.astype(BF16)
    lo = (x - hi.astype(F32)).astype(BF16)
    return _dot(hi, m) + _dot(lo, m)


def _mask_half(x, hh):
    lane = lax.broadcasted_iota(jnp.int32, x.shape, x.ndim - 1)
    keep = (lane % LANES < HEAD_DIM) if hh == 0 else (lane % LANES >= HEAD_DIM)
    return jnp.where(keep, x, jnp.zeros_like(x))


def _in_proj_body(x_ref, nw_ref, w_ref, cv_ref, gm_ref, *out_refs):
    x = x_ref[...]
    ms = jnp.mean(x * x, axis=-1, keepdims=True)
    h = (x * lax.rsqrt(ms + RMS_EPS) * nw_ref[...]).astype(BF16)
    gm = gm_ref[...]
    tiles = [(o_ref, t, kind) for (name, ntile, kind), o_ref in zip(_IN_SEGS, out_refs) for t in range(ntile)]
    for sl in range(_IN_COLS // _SLAB):
        c0 = sl * _SLAB
        y = _dot(h, w_ref[:, c0:c0 + _SLAB])
        if tiles[2 * sl][2] == "norm":
            msq = _split_dot(y * y, gm)
            y = y * lax.rsqrt(msq + RMS_EPS)
        y = y * cv_ref[:, c0:c0 + _SLAB]
        for half in range(2):
            o_ref, t, kind = tiles[2 * sl + half]
            yh = y[:, half * LANES:(half + 1) * LANES]
            if kind == "gate":
                yh = jax.nn.sigmoid(yh)
            o_ref[:, t * LANES:(t + 1) * LANES] = yh.astype(o_ref.dtype)


def _in_proj(x2d, nw, w_re, cv, gm, tm=512):
    n = x2d.shape[0]
    out_shape, out_specs = [], []
    for name, ntile, kind in _IN_SEGS:
        dt = F32 if kind == "gate" or name in _F32_SEGS else BF16
        out_shape.append(jax.ShapeDtypeStruct((n, ntile * LANES), dt))
        out_specs.append(pl.BlockSpec((tm, ntile * LANES), lambda i: (i, 0)))
    return pl.pallas_call(
        _in_proj_body,
        grid=(n // tm,),
        in_specs=[pl.BlockSpec((tm, D_MODEL), lambda i: (i, 0)),
                  pl.BlockSpec((1, D_MODEL), lambda i: (0, 0)),
                  pl.BlockSpec((D_MODEL, _IN_COLS), lambda i: (0, 0)),
                  pl.BlockSpec((1, _IN_COLS), lambda i: (0, 0)),
                  pl.BlockSpec((_SLAB, _SLAB), lambda i: (0, 0))],
        out_specs=out_specs,
        out_shape=out_shape,
        compiler_params=_cparams("parallel"),
        name="in_proj",
    )(x2d, nw, w_re, cv, gm)


def _gelu_tanh(x):
    return 0.5 * x * (1.0 + jnp.tanh(0.7978845608028654 * (x + 0.044715 * (x * x * x))))


def _cmp_body(x_ref, w1_ref, pe_ref, w2_ref, gk_ref, kc_ref, vc_ref):
    ncp = kc_ref.shape[1]
    pe = pe_ref[...].astype(BF16)
    xs = [x_ref[0, pl.ds(l, ncp, stride=CMP_STRIDE), :].astype(BF16) for l in range(CMP_STRIDE)]
    outs = []
    for c in range(2):
        w_top = lambda l: w1_ref[2 * c, l * LANES:(l + 1) * LANES, :]
        w_bot = lambda l: w1_ref[2 * c + 1, l * LANES:(l + 1) * LANES, :]
        top = bot = bias = 0.0
        for l in range(CMP_STRIDE):
            top = top + _dot(xs[l], w_top(l))
            bot = bot + _dot(xs[l], w_bot(l))
            pl_ = pe[:, l * LANES:(l + 1) * LANES]
            bias = bias + _dot(pl_, w_top(l))[2 * c:2 * c + 1] + _dot(pl_, w_bot(l))[2 * c + 1:2 * c + 2]
        hid = top + pltpu.roll(bot, ncp - 1, 0) + bias
        outs.append(_dot(_gelu_tanh(hid).astype(BF16), w2_ref[c]))
    kc = outs[0]
    kc = kc * lax.rsqrt(jnp.mean(kc * kc, axis=-1, keepdims=True) + RMS_EPS) * gk_ref[...]
    kc_ref[0] = kc.astype(BF16)
    vc_ref[0] = outs[1].astype(BF16)


def _compress(kvc, w1i, pe8, w2r, gk):
    b, s, _ = kvc.shape
    ncp, wid = s // CMP_STRIDE, CMP_STRIDE * LANES
    return pl.pallas_call(
        _cmp_body,
        grid=(b,),
        in_specs=[pl.BlockSpec((1, s, LANES), lambda i: (i, 0, 0)),
                  pl.BlockSpec((4, wid, CMP_HIDDEN), lambda i: (0, 0, 0)),
                  pl.BlockSpec((8, wid), lambda i: (0, 0)),
                  pl.BlockSpec((2, CMP_HIDDEN, LANES), lambda i: (0, 0, 0)),
                  pl.BlockSpec((1, LANES), lambda i: (0, 0))],
        out_specs=[pl.BlockSpec((1, ncp, LANES), lambda i: (i, 0, 0)),
                   pl.BlockSpec((1, ncp, LANES), lambda i: (i, 0, 0))],
        out_shape=[jax.ShapeDtypeStruct((b, ncp, LANES), BF16)] * 2,
        compiler_params=_cparams("parallel"),
        name="nsa_compress",
    )(kvc, w1i, pe8, w2r, gk)


def _softmax_cols(s, mask):
    m = jnp.max(s, axis=0, keepdims=True)
    p = jnp.where(mask, jnp.exp2(s - m), 0.0)
    l = jnp.maximum(jnp.sum(p, axis=0, keepdims=True), 1e-30)
    return p, l


def _cw_body(q_ref, kc_ref, vc_ref, kw_ref, vw_ref, g_ref, ovt_ref, oa_ref, ns_ref, fl_ref):
    i = pl.program_id(1)
    t0 = i * QB
    q = q_ref[0]
    gt = g_ref[0].T
    kc = kc_ref[0]
    vc = vc_ref[0]
    ncp = kc.shape[0]
    wk = WIN_NSA + QB

    n_row = lax.broadcasted_iota(jnp.int32, (ncp, QB), 0)
    q_lane = lax.broadcasted_iota(jnp.int32, (ncp, QB), 1)
    vis = (t0 - (CMP_LEN - 1)) + q_lane - CMP_STRIDE * n_row >= 0
    cend = (CMP_STRIDE * n_row + (CMP_LEN - 1)).astype(F32)

    start = pl.multiple_of(jnp.maximum(t0 - WIN_NSA, 0), QB)
    kw = kw_ref[0, pl.ds(start, wk), :]
    vw = vw_ref[0, pl.ds(start, wk), :]
    j_row = lax.broadcasted_iota(jnp.int32, (wk, QB), 0)
    r_lane = lax.broadcasted_iota(jnp.int32, (wk, QB), 1)
    dw = (t0 - start) + r_lane - j_row
    wmask = (dw >= 0) & (dw < WIN_NSA)
    kposw = (start + j_row).astype(F32)

    psum = jnp.zeros((ncp, QB), F32)
    o_heads = []
    for h in range(H_NSA):
        slab = q[:, (h // 2) * LANES:(h // 2 + 1) * LANES]
        qm = _mask_half(slab, h % 2)
        sl = SLOPES_NSA[h] * LOG2E
        s = _nt_dot(kc, qm) + sl * cend
        s = jnp.where(vis, s, NEG)
        p, l = _softmax_cols(s, vis)
        pn = p * (1.0 / l)
        psum = psum + pn
        oc = _tn_dot(vc, (pn * gt[3 * h:3 * h + 1]).astype(BF16))
        s = _nt_dot(kw, qm) + sl * kposw
        s = jnp.where(wmask, s, NEG)
        p, l = _softmax_cols(s, wmask)
        ow = _tn_dot(vw, (p * (gt[3 * h + 2:3 * h + 3] / l)).astype(BF16))
        o_heads.append((oc + ow)[:HEAD_DIM])
    for hp in range(H_NSA // 2):
        oa_ref[0, :, hp * LANES:(hp + 1) * LANES] = jnp.concatenate(o_heads[2 * hp:2 * hp + 2], axis=0).T

    hi = psum.astype(BF16)
    lo = (psum - hi.astype(F32)).astype(BF16)
    imp = _dot(ovt_ref[...], hi) + _dot(ovt_ref[...], lo)
    j = lax.broadcasted_iota(jnp.int32, (SEL_LANES, QB), 0)
    qi = lax.broadcasted_iota(jnp.int32, (SEL_LANES, QB), 1)
    cur = jnp.right_shift(t0 + qi, int(math.log2(SEL_BLOCK)))
    forced = (j == 0) | (j == cur) | (j == cur - 1)
    imp = jnp.where(forced, -3e38, jnp.where(j <= cur, imp, NEG))
    jf = j.astype(F32)
    notsel = jnp.where(forced, 0.0, 1.0)
    for _ in range(SEL_TOPK - 3):
        mx = jnp.max(imp, axis=0, keepdims=True)
        idx = jnp.min(jnp.where(imp == mx, jf, float(SEL_LANES)), axis=0, keepdims=True)
        hit = jf == idx
        notsel = jnp.where(hit, 0.0, notsel)
        imp = jnp.where(hit, -3e38, imp)
    nst = notsel.T
    ns_ref[0] = nst.astype(BF16)
    used = 1.0 - jnp.min(nst, axis=0, keepdims=True)
    fl_ref[0, 0] = jnp.broadcast_to(used, (8, SEL_LANES)).astype(jnp.int32)


def _nsa_cmp_win(qa, kcr, vcr, kwkw, vwvw, gates, ovt):
    b, s, _ = qa.shape
    ncp = kcr.shape[1]
    return pl.pallas_call(
        _cw_body,
        grid=(b, s // QB),
        in_specs=[pl.BlockSpec((1, QB, 2 * LANES), lambda bi, i: (bi, i, 0)),
                  pl.BlockSpec((1, ncp, LANES), lambda bi, i: (bi, 0, 0)),
                  pl.BlockSpec((1, ncp, LANES), lambda bi, i: (bi, 0, 0)),
                  pl.BlockSpec((1, s, LANES), lambda bi, i: (bi, 0, 0)),
                  pl.BlockSpec((1, s, LANES), lambda bi, i: (bi, 0, 0)),
                  pl.BlockSpec((1, QB, LANES), lambda bi, i: (bi, i, 0)),
                  pl.BlockSpec((SEL_LANES, ncp), lambda bi, i: (0, 0))],
        out_specs=[pl.BlockSpec((1, QB, 2 * LANES), lambda bi, i: (bi, i, 0)),
                   pl.BlockSpec((1, QB, SEL_LANES), lambda bi, i: (bi, i, 0)),
                   pl.BlockSpec((1, 1, 8, SEL_LANES), lambda bi, i: (bi, i, 0, 0))],
        out_shape=[jax.ShapeDtypeStruct((b, s, 2 * LANES), F32),
                   jax.ShapeDtypeStruct((b, s, SEL_LANES), BF16),
                   jax.ShapeDtypeStruct((b, s // QB, 8, SEL_LANES), jnp.int32)],
        compiler_params=_cparams("parallel", "parallel"),
        name="nsa_cmp_win",
    )(qa, kcr, vcr, kwkw, vwvw, gates, ovt)


def _sel_body(fl_ref, q_ref, ns_ref, g_ref, part_ref, ks_ref, vs_ref, kc_ref, sc_ref, oa_ref,
              kaug_ref, vaug_ref, qaug_ref, m_ref, acc_ref, sa_ref, sb_ref, tl_ref, *, tk):
    i = pl.program_id(1)
    t0 = i * QB
    nq = H_NSA * QB
    last_tile = kaug_ref.shape[0] // tk - 1

    @pl.when(i == 0)
    def _():
        lane = lax.broadcasted_iota(jnp.int32, (kaug_ref.shape[0], LANES), 1)
        kaug_ref[:, 0:LANES] = jnp.where(lane < HEAD_DIM, ks_ref[0], kc_ref[:, 0:LANES])
        kaug_ref[:, LANES:2 * LANES] = kc_ref[:, LANES:2 * LANES]
        vaug_ref[...] = jnp.where(lane < HEAD_DIM, vs_ref[0], jnp.ones((), BF16))

    q = q_ref[0]
    ns = ns_ref[0]
    lane = lax.broadcasted_iota(jnp.int32, (QB, LANES), 1)
    for h in range(H_NSA):
        slab = q[:, (h // 2) * LANES:(h // 2 + 1) * LANES].astype(F32)
        if h % 2:
            slab = pltpu.roll(slab, HEAD_DIM, 1)
        qaug_ref[h * QB:(h + 1) * QB, 0:LANES] = jnp.where(lane < HEAD_DIM, slab, sc_ref[h:h + 1, :]).astype(BF16)
        qaug_ref[h * QB:(h + 1) * QB, LANES:2 * LANES] = ns
    m_ref[...] = jnp.full(m_ref.shape, NEG, F32)
    acc_ref[...] = jnp.zeros(acc_ref.shape, F32)

    key_row = lax.broadcasted_iota(jnp.int32, (tk, nq), 0)
    q_lane = lax.broadcasted_iota(jnp.int32, (tk, nq), 1)
    dmat = key_row - q_lane % QB

    def scores(jt):
        k0 = pl.multiple_of(jnp.minimum(jt, last_tile) * tk, tk)
        return _nt_dot(kaug_ref[pl.ds(k0, tk), :], qaug_ref[...])

    def update(s_ref, jt, diagonal):
        k0 = pl.multiple_of(jnp.minimum(jt, last_tile) * tk, tk)
        s = s_ref[...]
        if diagonal:
            s = jnp.where(dmat <= t0 - jt * tk, s, NEG)
        m_old = m_ref[...]
        m_new = jnp.maximum(m_old, jnp.max(s, axis=0, keepdims=True))
        alpha = jnp.exp2(m_old - m_new)
        p = jnp.exp2(s - m_new)
        m_ref[...] = m_new
        pv = _tn_dot(vaug_ref[pl.ds(k0, tk), :], p.astype(BF16))
        acc_ref[...] = alpha * acc_ref[...] + pv[:acc_ref.shape[0]]

    blocks_per_tile = tk // SEL_BLOCK
    diag_tile = t0 // tk
    frow = pl.program_id(0) * pl.num_programs(1) + i

    def collect(jt, cnt):
        used = fl_ref[frow, jt * blocks_per_tile]
        for c in range(1, blocks_per_tile):
            used = used | fl_ref[frow, jt * blocks_per_tile + c]
        tl_ref[cnt] = jt
        return cnt + used

    n_before = lax.fori_loop(0, diag_tile, collect, 0)
    tl_ref[n_before] = diag_tile
    tl_ref[n_before + 1] = 2 * (last_tile + 1)

    def body(jj, carry):
        sb_ref[...] = scores(tl_ref[2 * jj + 1])
        update(sa_ref, tl_ref[2 * jj], False)
        sa_ref[...] = scores(tl_ref[2 * jj + 2])
        update(sb_ref, tl_ref[2 * jj + 1], False)
        return carry

    n_pairs = n_before // 2
    sa_ref[...] = scores(tl_ref[0])
    lax.fori_loop(0, n_pairs, body, 0)
    sb_ref[...] = scores(tl_ref[2 * n_pairs + 1])
    update(sa_ref, tl_ref[2 * n_pairs], True)
    update(sb_ref, tl_ref[2 * n_pairs + 1], True)

    gt = g_ref[0].T
    for hp in range(H_NSA // 2):
        rows = []
        for hh in range(2):
            h = 2 * hp + hh
            cs = slice(h * QB, (h + 1) * QB)
            rows.append(acc_ref[0:HEAD_DIM, cs] * (gt[3 * h + 1:3 * h + 2] / acc_ref[HEAD_DIM:HEAD_DIM + 1, cs]))
        cs = slice(hp * LANES, (hp + 1) * LANES)
        oa_ref[0, :, cs] = part_ref[0, :, cs] + jnp.concatenate(rows, axis=0).T


def _nsa_selected(flags, qa, nsel, gates, part, ksks, vsvs, kconst, sconst, tk=256):
    b, s, _ = qa.shape
    nq = H_NSA * QB
    grid_spec = pltpu.PrefetchScalarGridSpec(
        num_scalar_prefetch=1,
        grid=(b, s // QB),
        in_specs=[pl.BlockSpec((1, QB, 2 * LANES), lambda bi, i, fl: (bi, i, 0)),
                  pl.BlockSpec((1, QB, SEL_LANES), lambda bi, i, fl: (bi, i, 0)),
                  pl.BlockSpec((1, QB, LANES), lambda bi, i, fl: (bi, i, 0)),
                  pl.BlockSpec((1, QB, 2 * LANES), lambda bi, i, fl: (bi, i, 0)),
                  pl.BlockSpec((1, s, LANES), lambda bi, i, fl: (bi, 0, 0)),
                  pl.BlockSpec((1, s, LANES), lambda bi, i, fl: (bi, 0, 0)),
                  pl.BlockSpec((s, 2 * LANES), lambda bi, i, fl: (0, 0)),
                  pl.BlockSpec((8, LANES), lambda bi, i, fl: (0, 0))],
        out_specs=pl.BlockSpec((1, QB, 2 * LANES), lambda bi, i, fl: (bi, i, 0)),
        scratch_shapes=[pltpu.VMEM((s, 2 * LANES), BF16),
                        pltpu.VMEM((s, LANES), BF16),
                        pltpu.VMEM((nq, 2 * LANES), BF16),
                        pltpu.VMEM((1, nq), F32),
                        pltpu.VMEM((HEAD_DIM + 8, nq), F32),
                        pltpu.VMEM((tk, nq), F32),
                        pltpu.VMEM((tk, nq), F32),
                        pltpu.SMEM((s // tk + 8,), jnp.int32)])
    return pl.pallas_call(
        functools.partial(_sel_body, tk=tk),
        grid_spec=grid_spec,
        out_shape=jax.ShapeDtypeStruct((b, s, 2 * LANES), F32),
        compiler_params=_cparams("arbitrary", "arbitrary"),
        name="nsa_selected",
    )(flags, qa, nsel, gates, part, ksks, vsvs, kconst, sconst)


def _dil_body(q_ref, kp_ref, kc_ref, vp_ref, vc_ref, bias_ref, o_ref, lse_ref, *, r, m):
    i = pl.program_id(1)
    key_row = lax.broadcasted_iota(jnp.int32, (2 * QB, QB), 0)
    first = key_row >= jnp.where(i > 0, 0, QB)

    def rows(c, u):
        return pl.ds(u * QB * r + c, QB, stride=r) if r > 1 else pl.ds(u * QB, QB)

    for c in range(r):
        for u in range(m):
            cur = rows(c, u)
            q = q_ref[0, cur, :].astype(BF16)
            if u > 0:
                k_prev, v_prev = kc_ref[0, rows(c, u - 1), :], vc_ref[0, rows(c, u - 1), :]
            else:
                k_prev, v_prev = kp_ref[0, rows(c, m - 1), :], vp_ref[0, rows(c, m - 1), :]
            kk = jnp.concatenate([k_prev, kc_ref[0, cur, :]], axis=0).astype(BF16)
            vv = jnp.concatenate([v_prev, vc_ref[0, cur, :]], axis=0).astype(BF16)
            outs, lses = [], []
            for hh in range(2):
                s = _nt_dot(kk, _mask_half(q, hh)) + bias_ref[hh]
                if u == 0:
                    s = jnp.where(first, s, NEG)
                mx = jnp.max(s, axis=0, keepdims=True)
                p = jnp.exp2(s - mx)
                l = jnp.sum(p, axis=0, keepdims=True)
                o = _tn_dot(vv, p.astype(BF16)) * (1.0 / l)
                outs.append(o[hh * HEAD_DIM:(hh + 1) * HEAD_DIM])
                lses.append(jnp.broadcast_to(mx + jnp.log2(l), (HEAD_DIM, QB)))
            o_ref[0, cur, :] = jnp.concatenate(outs, axis=0).T
            lse_ref[0, cur, :] = jnp.concatenate(lses, axis=0).T


def _dilated(dq, dk, dv, bias, g, r, span=512):
    b, s, _ = dq.shape
    m = max(span // (QB * r), 1)
    span = m * QB * r
    cur = lambda bi, i: (bi, i, g)
    prev = lambda bi, i: (bi, jnp.maximum(i - 1, 0), g)
    return pl.pallas_call(
        functools.partial(_dil_body, r=r, m=m),
        grid=(b, s // span),
        in_specs=[pl.BlockSpec((1, span, LANES), cur),
                  pl.BlockSpec((1, span, LANES), prev), pl.BlockSpec((1, span, LANES), cur),
                  pl.BlockSpec((1, span, LANES), prev), pl.BlockSpec((1, span, LANES), cur),
                  pl.BlockSpec((2, 2 * QB, QB), lambda bi, i: (0, 0, 0))],
        out_specs=[pl.BlockSpec((1, span, LANES), lambda bi, i: (bi, i, 0))] * 2,
        out_shape=[jax.ShapeDtypeStruct((b, s, LANES), F32)] * 2,
        compiler_params=_cparams("parallel", "parallel"),
        name=f"dilated_r{r}",
    )(dq, dk, dk, dv, dv, bias)


def _dil_bias(g):
    w, r = DIL_PAIRS[g]
    assert w // r == QB
    iq = np.arange(QB)[None, :]
    jk = np.arange(2 * QB)[:, None]
    dist = iq + QB - jk
    out = np.empty((2, 2 * QB, QB), np.float32)
    for hh in range(2):
        slope = SLOPES_DIL[2 * g + hh]
        out[hh] = np.where((dist >= 0) & (dist <= QB), -slope * LOG2E * r * dist, NEG)
    return jnp.asarray(out)


def _sb_body(q_ref, k_ref, v_ref, lt_ref, o_ref, acc_ref):
    i = pl.program_id(2)
    tq = q_ref.shape[1]
    q = q_ref[0]
    lt = lt_ref[...]
    dmat = lax.broadcasted_iota(jnp.int32, (tq, tq), 0) - lax.broadcasted_iota(jnp.int32, (tq, tq), 1)
    qms = [_mask_half(q, hh) for hh in range(2)]
    acc_ref[...] = jnp.zeros(acc_ref.shape, F32)

    def cond(c):
        jt, _, _, cmax = c
        return (jt >= 0) & (cmax > SB_UNDERFLOW * LOG2E)

    def body(c):
        jt, carry0, carry1, _ = c
        carries = [carry0, carry1]
        pvs = [[], []]
        for u in range(2):
            ju = jt - u
            k0 = pl.multiple_of(jnp.maximum(ju, 0) * tq, tq)
            kt = k_ref[0, pl.ds(k0, tq), :]
            vt = v_ref[0, pl.ds(k0, tq), :]
            mask = dmat < jnp.where(ju < 0, -tq, jnp.where(ju < i, tq, 0))
            for hh in range(2):
                z = _nt_dot(kt, qms[hh])
                nabs = lax.bitcast_convert_type(lax.bitcast_convert_type(z, jnp.uint32) | jnp.uint32(1 << 31), F32)
                lb = jnp.minimum(z, 0.0) - jnp.log2(1.0 + jnp.exp2(nabs))
                lf = jnp.where(mask, lb - z, 0.0)
                hi = lf.astype(BF16)
                lo = (lf - hi.astype(F32)).astype(BF16)
                later = _dot(lt, hi) + _dot(lt, lo)
                a = jnp.where(mask, jnp.exp2(lb + later + carries[hh]), 0.0)
                pvs[hh].append(_tn_dot(vt, a.astype(BF16))[hh * HEAD_DIM:(hh + 1) * HEAD_DIM])
                carries[hh] = carries[hh] + jnp.sum(lf, axis=0, keepdims=True)
        for hh in range(2):
            acc_ref[hh] += pvs[hh][0] + pvs[hh][1]
        cmax = jnp.maximum(jnp.max(carries[0]), jnp.max(carries[1]))
        return jt - 2, carries[0], carries[1], cmax

    zero = jnp.zeros((1, tq), F32)
    lax.while_loop(cond, body, (i, zero, zero, jnp.float32(0.0)))
    o_ref[0] = jnp.concatenate([acc_ref[0], acc_ref[1]], axis=0).T


def _stick_breaking(sq, sk, sv, lt):
    b, s, _ = sq.shape
    npair = H_SB // 2
    tq = lt.shape[0]
    return pl.pallas_call(
        _sb_body,
        grid=(b, npair, s // tq),
        in_specs=[pl.BlockSpec((1, tq, LANES), lambda bi, hp, i: (bi, i, hp)),
                  pl.BlockSpec((1, s, LANES), lambda bi, hp, i: (bi, 0, hp)),
                  pl.BlockSpec((1, s, LANES), lambda bi, hp, i: (bi, 0, hp)),
                  pl.BlockSpec((tq, tq), lambda bi, hp, i: (0, 0))],
        out_specs=pl.BlockSpec((1, tq, LANES), lambda bi, hp, i: (bi, i, hp)),
        out_shape=jax.ShapeDtypeStruct((b, s, npair * LANES), F32),
        scratch_shapes=[pltpu.VMEM((2, HEAD_DIM, tq), F32)],
        compiler_params=_cparams("parallel", "parallel", "parallel"),
        name="stick_breaking",
    )(sq, sk, sv, lt)


def _out_body(x_ref, oa_ref, d0, l0, d1, l1, d2, l2, oc_ref, w_ref, o_ref):
    lses = [l0[...], l1[...], l2[...]]
    m = jnp.maximum(jnp.maximum(lses[0], lses[1]), lses[2])
    es = [jnp.exp2(l - m) for l in lses]
    den = es[0] + es[1] + es[2]
    ob = (es[0] * d0[...] + es[1] * d1[...] + es[2] * d2[...]) / den
    na = H_NSA * HEAD_DIM
    nb = na + LANES
    acc = _dot(oa_ref[...].astype(BF16), w_ref[0:na, :])
    acc = acc + _dot(ob.astype(BF16), w_ref[na:nb, :])
    acc = acc + _dot(oc_ref[...].astype(BF16), w_ref[nb:, :])
    o_ref[...] = x_ref[...] + acc


def _out_proj(x2d, oa, dil, oc, w_out, tm=512):
    n = x2d.shape[0]
    row = lambda w: pl.BlockSpec((tm, w), lambda i: (i, 0))
    d_cat = w_out.shape[0]
    ins = [x2d, oa]
    specs = [row(D_MODEL), row(2 * LANES)]
    for o, lse in dil:
        ins += [o, lse]
        specs += [row(LANES), row(LANES)]
    ins += [oc, w_out]
    specs += [row(oc.shape[1]), pl.BlockSpec((d_cat, D_MODEL), lambda i: (0, 0))]
    return pl.pallas_call(
        _out_body,
        grid=(n // tm,),
        in_specs=specs,
        out_specs=row(D_MODEL),
        out_shape=jax.ShapeDtypeStruct((n, D_MODEL), F32),
        compiler_params=_cparams("parallel"),
        name="out_proj",
    )(*ins)


def _mlp_body(x_ref, nw_ref, wu_ref, wd_ref, o_ref, *, fc):
    x = x_ref[...]
    ms = jnp.mean(x * x, axis=-1, keepdims=True)
    h = (x * lax.rsqrt(ms + RMS_EPS) * nw_ref[...]).astype(BF16)
    acc = x
    for c in range(D_FF // fc):
        u = jnp.maximum(_dot(h, wu_ref[:, c * fc:(c + 1) * fc]), 0.0)
        acc = acc + _dot((u * u).astype(BF16), wd_ref[c * fc:(c + 1) * fc, :])
    o_ref[...] = acc


def _mlp(x2d, nw, wu, wd, tm=512, fc=1024):
    n = x2d.shape[0]
    const = dict(pipeline_mode=pl.Buffered(1))
    return pl.pallas_call(
        functools.partial(_mlp_body, fc=fc),
        grid=(n // tm,),
        in_specs=[pl.BlockSpec((tm, D_MODEL), lambda i: (i, 0)),
                  pl.BlockSpec((1, D_MODEL), lambda i: (0, 0)),
                  pl.BlockSpec((D_MODEL, D_FF), lambda i: (0, 0), **const),
                  pl.BlockSpec((D_FF, D_MODEL), lambda i: (0, 0), **const)],
        out_specs=pl.BlockSpec((tm, D_MODEL), lambda i: (i, 0)),
        out_shape=jax.ShapeDtypeStruct((n, D_MODEL), F32),
        compiler_params=_cparams("parallel"),
        name="mlp",
    )(x2d, nw, wu, wd)


def _relayout_in_weight(w, g_nsa, g_dil):
    hd = HEAD_DIM
    kv0 = H_NSA * hd
    g0 = kv0 + 6 * hd
    b0 = g0 + 3 * H_NSA
    c0 = b0 + 3 * H_DIL * hd
    kv = lambda c: w[:, kv0 + c * hd:kv0 + (c + 1) * hd]
    gate = jnp.pad(w[:, g0:b0], ((0, 0), (0, LANES - 3 * H_NSA)))
    nd = H_DIL * hd
    w_re = jnp.concatenate([w[:, :kv0], kv(2), kv(2), kv(4), kv(4), w[:, b0:b0 + 2 * nd],
                            kv(0), kv(1), kv(3), kv(3), kv(5), kv(5), gate, w[:, b0 + 2 * nd:c0], w[:, c0:]],
                           axis=1).astype(BF16)
    one = lambda n: jnp.ones((n,), F32)
    cv = jnp.concatenate([
        jnp.tile(g_nsa[0], H_NSA) * (SCALE * LOG2E), jnp.tile(g_nsa[2], 2), jnp.tile(g_nsa[3], 2),
        jnp.tile(g_dil[0], H_DIL) * (SCALE * LOG2E), jnp.tile(g_dil[1], H_DIL),
        one(4 * LANES), one(nd), one(H_SB * hd) * (SCALE * LOG2E), one(2 * H_SB * hd)])
    return w_re, cv.reshape(1, _IN_COLS)


def _relayout_cmp(w1, pe, w2, gk):
    hd, half = HEAD_DIM, CMP_LEN // 2
    w1r = w1.reshape(2, 2, half, hd, CMP_HIDDEN)
    per = pe.reshape(2, 2, half, hd)
    w1i, pei = [], []
    for c in range(2):
        pad = ((0, 0), (0, 0), (0, hd), (0, 0)) if c == 0 else ((0, 0), (0, 0), (hd, 0), (0, 0))
        w1i.append(jnp.pad(w1r[c], pad).reshape(2, half * LANES, CMP_HIDDEN))
        pei.append(jnp.pad(per[c], pad[:3]).reshape(2, half * LANES))
    w1i = jnp.concatenate(w1i, axis=0).astype(BF16)
    pe8 = jnp.pad(jnp.concatenate(pei, axis=0), ((0, 4), (0, 0)))
    w2r = jnp.concatenate([w2, w2], axis=-1).astype(BF16)
    return w1i, pe8, w2r, jnp.tile(gk, 2).reshape(1, LANES)


def _constants(s):
    ncp = s // CMP_STRIDE
    n_cmp = (s - CMP_LEN) // CMP_STRIDE + 1
    n = np.arange(ncp)[:, None]
    j = np.arange(SEL_LANES)[None, :]
    ov = ((CMP_STRIDE * n <= SEL_BLOCK * j + SEL_BLOCK - 1) & (CMP_STRIDE * n + CMP_LEN - 1 >= SEL_BLOCK * j)
          & (n < n_cmp) & (j < s // SEL_BLOCK))
    pos = np.arange(s)
    kconst = np.zeros((s, 2 * LANES), np.float32)
    kconst[:, HEAD_DIM:HEAD_DIM + 3] = (pos // SEL_BLOCK * SEL_BLOCK)[:, None]
    kconst[:, HEAD_DIM + 3:HEAD_DIM + 6] = (pos % SEL_BLOCK)[:, None]
    kconst[:, LANES:] = np.where(pos[:, None] // SEL_BLOCK == j, NEG, 0.0)
    sconst = np.zeros((8, LANES), np.float32)
    for h in range(H_NSA):
        rest = np.float32(SLOPES_NSA[h] * LOG2E)
        for c in range(3):
            piece = rest.astype(BF16).astype(np.float32)
            sconst[h, HEAD_DIM + c] = sconst[h, HEAD_DIM + 3 + c] = piece
            rest = np.float32(rest - piece)
    a2 = np.arange(2 * QB)
    lt = (a2[None, :] > a2[:, None]).astype(np.float32)
    a = np.arange(_SLAB)
    gm = (a[:, None] // HEAD_DIM == a[None, :] // HEAD_DIM).astype(np.float32) / HEAD_DIM
    return dict(ovt=jnp.asarray(ov.T, BF16), kconst=jnp.asarray(kconst, BF16), sconst=jnp.asarray(sconst),
                lt=jnp.asarray(lt, BF16), gm=jnp.asarray(gm, BF16),
                dil_bias=[_dil_bias(g) for g in range(len(DIL_PAIRS))])


def kernel(x, norm_mix, norm_mlp, w_in, qk_gain_nsa, qk_gain_dil, cmp_pe, cmp_w1, cmp_w2, w_out, w_up, w_down):
    b, s, d = x.shape
    assert d == D_MODEL and s % (DIL_PAIRS[-1][1] * QB) == 0 and s // SEL_BLOCK <= SEL_LANES
    assert s >= WIN_NSA + QB
    n = b * s
    cst = _constants(s)
    x2d = x.reshape(n, d)
    for l in range(w_in.shape[0]):
        w_re, cv = _relayout_in_weight(w_in[l], qk_gain_nsa[l], qk_gain_dil[l])
        qa, ksks, kwkw, dq, dk, kvc, vsvs, vwvw, gates, dv, sq, sk, sv = _in_proj(
            x2d, norm_mix[l].reshape(1, d), w_re, cv, cst["gm"])
        tok = lambda a: a.reshape(b, s, a.shape[-1])
        w1i, pe8, w2r, gk = _relayout_cmp(cmp_w1[l], cmp_pe[l], cmp_w2[l], qk_gain_nsa[l, 1])
        kcr, vcr = _compress(tok(kvc), w1i, pe8, w2r, gk)
        part, nsel, used = _nsa_cmp_win(tok(qa), kcr, vcr, tok(kwkw), tok(vwvw), tok(gates), cst["ovt"])
        oa = _nsa_selected(used[:, :, 0, :].reshape(n // QB, SEL_LANES), tok(qa), nsel, tok(gates), part,
                           tok(ksks), tok(vsvs), cst["kconst"], cst["sconst"])
        dil = []
        for g, (_, r) in enumerate(DIL_PAIRS):
            o, lse = _dilated(tok(dq), tok(dk), tok(dv), cst["dil_bias"][g], g, r)
            dil.append((o.reshape(n, LANES), lse.reshape(n, LANES)))
        oc = _stick_breaking(tok(sq), tok(sk), tok(sv), cst["lt"])
        x2d = _out_proj(x2d, oa.reshape(n, 2 * LANES), dil, oc.reshape(n, -1), w_out[l].astype(BF16))
        x2d = _mlp(x2d, norm_mlp[l].reshape(1, d), w_up[l].astype(BF16), w_down[l].astype(BF16))
    return x2d.reshape(b, s, d)
```

```python
import functools
import math

import numpy as np
import jax
import jax.numpy as jnp
from jax import lax
from jax.experimental import pallas as pl
from jax.experimental.pallas import tpu as pltpu

F32 = jnp.float32
BF16 = jnp.bfloat16

D_MODEL = 1024
HEAD_DIM = 64
H_NSA = 4
H_DIL = 6
H_SB = 6
DIL_PAIRS = ((128, 1), (512, 4), (2048, 16))
CMP_LEN = 32
CMP_STRIDE = 16
CMP_HIDDEN = 128
SEL_BLOCK = 64
SEL_TOPK = 16
WIN_NSA = 512
D_FF = 4 * D_MODEL
RMS_EPS = 1e-6
NEG = -1e30
FORCE_BONUS = 1e4
LOG2E = 1.4426950408889634
SCALE = HEAD_DIM ** -0.5
LANES = 128
QB = 128
SEL_LANES = 128
SB_UNDERFLOW = -104.0

_SLOPES = [2.0 ** (-8.0 * i / (H_NSA + H_DIL)) for i in range(1, H_NSA + H_DIL + 1)]
SLOPES_DIL = _SLOPES[:H_DIL]
SLOPES_NSA = _SLOPES[H_DIL:]

_IN_SEGS = (("qa", 2, "norm"), ("ksks", 1, "norm"), ("kwkw", 1, "norm"), ("dq", 3, "norm"), ("dk", 3, "norm"),
            ("kvc", 1, "raw"), ("vsvs", 1, "raw"), ("vwvw", 1, "raw"), ("gate", 1, "gate"),
            ("dv", 3, "raw"), ("sq", 3, "raw"), ("sk", 3, "raw"), ("sv", 3, "raw"))
_IN_COLS = sum(n for _, n, _ in _IN_SEGS) * LANES
_SLAB = 2 * LANES
_F32_SEGS = ("kvc", "dq", "dk", "dv")
_VMEM_LIMIT = 56 * 1024 * 1024


def _cparams(*sem, vmem=_VMEM_LIMIT):
    return pltpu.CompilerParams(dimension_semantics=sem, vmem_limit_bytes=vmem)


def _nt_dot(a, b):
    return lax.dot_general(a, b, (((1,), (1,)), ((), ())), preferred_element_type=F32)


def _dot(a, b):
    return jnp.dot(a, b, preferred_element_type=F32)


def _tn_dot(a, b):
    return lax.dot_general(a, b, (((0,), (0,)), ((), ())), preferred_element_type=F32)


def _split_dot(x, m):
    hi = x.astype(BF16)
    lo = (x - hi.astype(F32)).astype(BF16)
    return _dot(hi, m) + _dot(lo, m)


def _mask_half(x, hh):
    lane = lax.broadcasted_iota(jnp.int32, x.shape, x.ndim - 1)
    keep = (lane % LANES < HEAD_DIM) if hh == 0 else (lane % LANES >= HEAD_DIM)
    return jnp.where(keep, x, jnp.zeros_like(x))


def _in_proj_body(x_ref, nw_ref, w_ref, cv_ref, gm_ref, *out_refs):
    x = x_ref[...]
    ms = jnp.mean(x * x, axis=-1, keepdims=True)
    h = (x * lax.rsqrt(ms + RMS_EPS) * nw_ref[...]).astype(BF16)
    gm = gm_ref[...]
    tiles = [(o_ref, t, kind) for (name, ntile, kind), o_ref in zip(_IN_SEGS, out_refs) for t in range(ntile)]
    for sl in range(_IN_COLS // _SLAB):
        c0 = sl * _SLAB
        y = _nt_dot(h, w_ref[c0:c0 + _SLAB, :])
        if tiles[2 * sl][2] == "norm":
            msq = _split_dot(y * y, gm)
            y = y * lax.rsqrt(msq + RMS_EPS)
        y = y * cv_ref[:, c0:c0 + _SLAB]
        for half in range(2):
            o_ref, t, kind = tiles[2 * sl + half]
            yh = y[:, half * LANES:(half + 1) * LANES]
            if kind == "gate":
                yh = jax.nn.sigmoid(yh)
            o_ref[:, t * LANES:(t + 1) * LANES] = yh.astype(o_ref.dtype)


def _in_proj(x2d, nw, w_re, cv, gm, tm=512):
    n = x2d.shape[0]
    out_shape, out_specs = [], []
    for name, ntile, kind in _IN_SEGS:
        dt = F32 if kind == "gate" or name in _F32_SEGS else BF16
        out_shape.append(jax.ShapeDtypeStruct((n, ntile * LANES), dt))
        out_specs.append(pl.BlockSpec((tm, ntile * LANES), lambda i: (i, 0)))
    return pl.pallas_call(
        _in_proj_body,
        grid=(n // tm,),
        in_specs=[pl.BlockSpec((tm, D_MODEL), lambda i: (i, 0)),
                  pl.BlockSpec((1, D_MODEL), lambda i: (0, 0)),
                  pl.BlockSpec((_IN_COLS, D_MODEL), lambda i: (0, 0)),
                  pl.BlockSpec((1, _IN_COLS), lambda i: (0, 0)),
                  pl.BlockSpec((_SLAB, _SLAB), lambda i: (0, 0))],
        out_specs=out_specs,
        out_shape=out_shape,
        compiler_params=_cparams("parallel"),
        name="in_proj",
    )(x2d, nw, w_re, cv, gm)


def _gelu_tanh(x):
    return 0.5 * x * (1.0 + jnp.tanh(0.7978845608028654 * (x + 0.044715 * (x * x * x))))


def _cmp_body(x_ref, w1_ref, pe_ref, w2_ref, gk_ref, kc_ref, vc_ref):
    ncp = kc_ref.shape[1]
    pe = pe_ref[...].astype(BF16)
    x = jnp.concatenate([x_ref[0, pl.ds(l, ncp, stride=CMP_STRIDE), :].astype(BF16) for l in range(CMP_STRIDE)],
                        axis=1)
    outs = []
    for c in range(2):
        top = _dot(x, w1_ref[2 * c])
        bot = _dot(x, w1_ref[2 * c + 1])
        bias = _dot(pe, w1_ref[2 * c])[2 * c:2 * c + 1] + _dot(pe, w1_ref[2 * c + 1])[2 * c + 1:2 * c + 2]
        hid = top + pltpu.roll(bot, ncp - 1, 0) + bias
        outs.append(_dot(_gelu_tanh(hid).astype(BF16), w2_ref[c]))
    kc = outs[0]
    kc = kc * lax.rsqrt(jnp.mean(kc * kc, axis=-1, keepdims=True) + RMS_EPS) * gk_ref[...]
    kc_ref[0] = kc.astype(BF16)
    vc_ref[0] = outs[1].astype(BF16)


def _compress(kvc, w1i, pe8, w2r, gk):
    b, s, _ = kvc.shape
    ncp, wid = s // CMP_STRIDE, CMP_STRIDE * LANES
    return pl.pallas_call(
        _cmp_body,
        grid=(b,),
        in_specs=[pl.BlockSpec((1, s, LANES), lambda i: (i, 0, 0)),
                  pl.BlockSpec((4, wid, CMP_HIDDEN), lambda i: (0, 0, 0)),
                  pl.BlockSpec((8, wid), lambda i: (0, 0)),
                  pl.BlockSpec((2, CMP_HIDDEN, LANES), lambda i: (0, 0, 0)),
                  pl.BlockSpec((1, LANES), lambda i: (0, 0))],
        out_specs=[pl.BlockSpec((1, ncp, LANES), lambda i: (i, 0, 0)),
                   pl.BlockSpec((1, ncp, LANES), lambda i: (i, 0, 0))],
        out_shape=[jax.ShapeDtypeStruct((b, ncp, LANES), BF16)] * 2,
        compiler_params=_cparams("parallel"),
        name="nsa_compress",
    )(kvc, w1i, pe8, w2r, gk)


def _softmax_cols(s, mask):
    m = jnp.max(s, axis=0, keepdims=True)
    p = jnp.where(mask, jnp.exp2(s - m), 0.0)
    l = jnp.maximum(jnp.sum(p, axis=0, keepdims=True), 1e-30)
    return p, l


def _cw_body(q_ref, kc_ref, vc_ref, kw_ref, vw_ref, g_ref, ovt_ref, oa_ref, ns_ref, fl_ref):
    i = pl.program_id(1)
    t0 = i * QB
    q = q_ref[0]
    gt = g_ref[0].T
    kc = kc_ref[0]
    vc = vc_ref[0]
    ncp = kc.shape[0]
    wk = WIN_NSA + QB

    n_row = lax.broadcasted_iota(jnp.int32, (ncp, QB), 0)
    q_lane = lax.broadcasted_iota(jnp.int32, (ncp, QB), 1)
    vis = (t0 - (CMP_LEN - 1)) + q_lane - CMP_STRIDE * n_row >= 0
    cend = (CMP_STRIDE * n_row + (CMP_LEN - 1)).astype(F32)

    start = pl.multiple_of(jnp.maximum(t0 - WIN_NSA, 0), QB)
    kw = kw_ref[0, pl.ds(start, wk), :]
    vw = vw_ref[0, pl.ds(start, wk), :]
    j_row = lax.broadcasted_iota(jnp.int32, (wk, QB), 0)
    r_lane = lax.broadcasted_iota(jnp.int32, (wk, QB), 1)
    dw = (t0 - start) + r_lane - j_row
    wmask = (dw >= 0) & (dw < WIN_NSA)
    kposw = (start + j_row).astype(F32)

    vlane = lax.broadcasted_iota(jnp.int32, (1, LANES), 1) < HEAD_DIM
    vc = jnp.where(vlane, vc, jnp.ones((), BF16))
    vw = jnp.where(vlane, vw, jnp.ones((), BF16))
    has_key = vis[0:1, :].astype(F32)

    psum = jnp.zeros((ncp, QB), F32)
    o_heads = []
    for h in range(H_NSA):
        slab = q[:, (h // 2) * LANES:(h // 2 + 1) * LANES]
        qm = _mask_half(slab, h % 2)
        sl = SLOPES_NSA[h] * LOG2E
        s = jnp.where(vis, _nt_dot(kc, qm) + sl * cend, NEG)
        p = jnp.exp2(s - jnp.max(s, axis=0, keepdims=True))
        pv = _tn_dot(vc, p.astype(BF16))
        rl = has_key / pv[HEAD_DIM:HEAD_DIM + 1]
        psum = psum + p * rl
        oc = pv[:HEAD_DIM] * (rl * gt[3 * h:3 * h + 1])
        s = jnp.where(wmask, _nt_dot(kw, qm) + sl * kposw, NEG)
        p = jnp.exp2(s - jnp.max(s, axis=0, keepdims=True))
        pv = _tn_dot(vw, p.astype(BF16))
        ow = pv[:HEAD_DIM] * (gt[3 * h + 2:3 * h + 3] / pv[HEAD_DIM:HEAD_DIM + 1])
        o_heads.append(oc + ow)
    for hp in range(H_NSA // 2):
        oa_ref[0, :, hp * LANES:(hp + 1) * LANES] = jnp.concatenate(o_heads[2 * hp:2 * hp + 2], axis=0).T

    hi = psum.astype(BF16)
    lo = (psum - hi.astype(F32)).astype(BF16)
    imp = _dot(ovt_ref[...], hi) + _dot(ovt_ref[...], lo)
    j = lax.broadcasted_iota(jnp.int32, (SEL_LANES, QB), 0)
    qi = lax.broadcasted_iota(jnp.int32, (SEL_LANES, QB), 1)
    cur = jnp.right_shift(t0 + qi, int(math.log2(SEL_BLOCK)))
    forced = (j == 0) | (j == cur) | (j == cur - 1)
    imp = jnp.where(forced, -3e38, jnp.where(j <= cur, imp, NEG))
    jf = j.astype(F32)
    notsel = jnp.where(forced, 0.0, 1.0)
    for _ in range(SEL_TOPK - 3):
        mx = jnp.max(imp, axis=0, keepdims=True)
        idx = jnp.min(jnp.where(imp == mx, jf, float(SEL_LANES)), axis=0, keepdims=True)
        hit = jf == idx
        notsel = jnp.where(hit, 0.0, notsel)
        imp = jnp.where(hit, -3e38, imp)
    nst = notsel.T
    ns_ref[0] = nst.astype(BF16)
    used = 1.0 - jnp.min(nst, axis=0, keepdims=True)
    fl_ref[0, 0] = jnp.broadcast_to(used, (8, SEL_LANES)).astype(jnp.int32)


def _nsa_cmp_win(qa, kcr, vcr, kwkw, vwvw, gates, ovt):
    b, s, _ = qa.shape
    ncp = kcr.shape[1]
    return pl.pallas_call(
        _cw_body,
        grid=(b, s // QB),
        in_specs=[pl.BlockSpec((1, QB, 2 * LANES), lambda bi, i: (bi, i, 0)),
                  pl.BlockSpec((1, ncp, LANES), lambda bi, i: (bi, 0, 0)),
                  pl.BlockSpec((1, ncp, LANES), lambda bi, i: (bi, 0, 0)),
                  pl.BlockSpec((1, s, LANES), lambda bi, i: (bi, 0, 0)),
                  pl.BlockSpec((1, s, LANES), lambda bi, i: (bi, 0, 0)),
                  pl.BlockSpec((1, QB, LANES), lambda bi, i: (bi, i, 0)),
                  pl.BlockSpec((SEL_LANES, ncp), lambda bi, i: (0, 0))],
        out_specs=[pl.BlockSpec((1, QB, 2 * LANES), lambda bi, i: (bi, i, 0)),
                   pl.BlockSpec((1, QB, SEL_LANES), lambda bi, i: (bi, i, 0)),
                   pl.BlockSpec((1, 1, 8, SEL_LANES), lambda bi, i: (bi, i, 0, 0))],
        out_shape=[jax.ShapeDtypeStruct((b, s, 2 * LANES), F32),
                   jax.ShapeDtypeStruct((b, s, SEL_LANES), BF16),
                   jax.ShapeDtypeStruct((b, s // QB, 8, SEL_LANES), jnp.int32)],
        compiler_params=_cparams("parallel", "parallel"),
        name="nsa_cmp_win",
    )(qa, kcr, vcr, kwkw, vwvw, gates, ovt)


def _sel_body(fl_ref, q_ref, ns_ref, g_ref, part_ref, ks_ref, vs_ref, kc_ref, sc_ref, oa_ref,
              kaug_ref, vaug_ref, qaug_ref, m_ref, acc_ref, sa_ref, sb_ref, tl_ref, *, tk):
    i = pl.program_id(1)
    t0 = i * QB
    nq = H_NSA * QB
    last_tile = kaug_ref.shape[0] // tk - 1

    @pl.when(i == 0)
    def _():
        lane = lax.broadcasted_iota(jnp.int32, (kaug_ref.shape[0], LANES), 1)
        kaug_ref[:, 0:LANES] = jnp.where(lane < HEAD_DIM, ks_ref[0], kc_ref[:, 0:LANES])
        kaug_ref[:, LANES:2 * LANES] = kc_ref[:, LANES:2 * LANES]
        vaug_ref[...] = jnp.where(lane < HEAD_DIM, vs_ref[0], jnp.ones((), BF16))

    q = q_ref[0]
    ns = ns_ref[0]
    lane = lax.broadcasted_iota(jnp.int32, (QB, LANES), 1)
    for h in range(H_NSA):
        slab = q[:, (h // 2) * LANES:(h // 2 + 1) * LANES].astype(F32)
        if h % 2:
            slab = pltpu.roll(slab, HEAD_DIM, 1)
        qaug_ref[h * QB:(h + 1) * QB, 0:LANES] = jnp.where(lane < HEAD_DIM, slab, sc_ref[h:h + 1, :]).astype(BF16)
        qaug_ref[h * QB:(h + 1) * QB, LANES:2 * LANES] = ns
    m_ref[...] = jnp.full(m_ref.shape, NEG, F32)
    acc_ref[...] = jnp.zeros(acc_ref.shape, F32)

    key_row = lax.broadcasted_iota(jnp.int32, (tk, nq), 0)
    q_lane = lax.broadcasted_iota(jnp.int32, (tk, nq), 1)
    dmat = key_row - q_lane % QB

    def scores(jt):
        k0 = pl.multiple_of(jnp.minimum(jt, last_tile) * tk, tk)
        return _nt_dot(kaug_ref[pl.ds(k0, tk), :], qaug_ref[...])

    def update(s_ref, jt, diagonal):
        k0 = pl.multiple_of(jnp.minimum(jt, last_tile) * tk, tk)
        s = s_ref[...]
        if diagonal:
            s = jnp.where(dmat <= t0 - jt * tk, s, NEG)
        m_old = m_ref[...]
        m_new = jnp.maximum(m_old, jnp.max(s, axis=0, keepdims=True))
        alpha = jnp.exp2(m_old - m_new)
        p = jnp.exp2(s - m_new)
        m_ref[...] = m_new
        pv = _tn_dot(vaug_ref[pl.ds(k0, tk), :], p.astype(BF16))
        acc_ref[...] = alpha * acc_ref[...] + pv[:acc_ref.shape[0]]

    blocks_per_tile = tk // SEL_BLOCK
    diag_tile = t0 // tk
    frow = pl.program_id(0) * pl.num_programs(1) + i

    def collect(jt, cnt):
        used = fl_ref[frow, jt * blocks_per_tile]
        for c in range(1, blocks_per_tile):
            used = used | fl_ref[frow, jt * blocks_per_tile + c]
        tl_ref[cnt] = jt
        return cnt + used

    n_before = lax.fori_loop(0, diag_tile, collect, 0)
    tl_ref[n_before] = diag_tile
    tl_ref[n_before + 1] = 2 * (last_tile + 1)

    def body(jj, carry):
        sb_ref[...] = scores(tl_ref[2 * jj + 1])
        update(sa_ref, tl_ref[2 * jj], False)
        sa_ref[...] = scores(tl_ref[2 * jj + 2])
        update(sb_ref, tl_ref[2 * jj + 1], False)
        return carry

    n_pairs = n_before // 2
    sa_ref[...] = scores(tl_ref[0])
    lax.fori_loop(0, n_pairs, body, 0)
    sb_ref[...] = scores(tl_ref[2 * n_pairs + 1])
    update(sa_ref, tl_ref[2 * n_pairs], True)
    update(sb_ref, tl_ref[2 * n_pairs + 1], True)

    gt = g_ref[0].T
    for hp in range(H_NSA // 2):
        rows = []
        for hh in range(2):
            h = 2 * hp + hh
            cs = slice(h * QB, (h + 1) * QB)
            rows.append(acc_ref[0:HEAD_DIM, cs] * (gt[3 * h + 1:3 * h + 2] / acc_ref[HEAD_DIM:HEAD_DIM + 1, cs]))
        cs = slice(hp * LANES, (hp + 1) * LANES)
        oa_ref[0, :, cs] = part_ref[0, :, cs] + jnp.concatenate(rows, axis=0).T


def _nsa_selected(flags, qa, nsel, gates, part, ksks, vsvs, kconst, sconst, tk=256):
    b, s, _ = qa.shape
    nq = H_NSA * QB
    grid_spec = pltpu.PrefetchScalarGridSpec(
        num_scalar_prefetch=1,
        grid=(b, s // QB),
        in_specs=[pl.BlockSpec((1, QB, 2 * LANES), lambda bi, i, fl: (bi, i, 0)),
                  pl.BlockSpec((1, QB, SEL_LANES), lambda bi, i, fl: (bi, i, 0)),
                  pl.BlockSpec((1, QB, LANES), lambda bi, i, fl: (bi, i, 0)),
                  pl.BlockSpec((1, QB, 2 * LANES), lambda bi, i, fl: (bi, i, 0)),
                  pl.BlockSpec((1, s, LANES), lambda bi, i, fl: (bi, 0, 0)),
                  pl.BlockSpec((1, s, LANES), lambda bi, i, fl: (bi, 0, 0)),
                  pl.BlockSpec((s, 2 * LANES), lambda bi, i, fl: (0, 0)),
                  pl.BlockSpec((8, LANES), lambda bi, i, fl: (0, 0))],
        out_specs=pl.BlockSpec((1, QB, 2 * LANES), lambda bi, i, fl: (bi, i, 0)),
        scratch_shapes=[pltpu.VMEM((s, 2 * LANES), BF16),
                        pltpu.VMEM((s, LANES), BF16),
                        pltpu.VMEM((nq, 2 * LANES), BF16),
                        pltpu.VMEM((1, nq), F32),
                        pltpu.VMEM((HEAD_DIM + 8, nq), F32),
                        pltpu.VMEM((tk, nq), F32),
                        pltpu.VMEM((tk, nq), F32),
                        pltpu.SMEM((s // tk + 8,), jnp.int32)])
    return pl.pallas_call(
        functools.partial(_sel_body, tk=tk),
        grid_spec=grid_spec,
        out_shape=jax.ShapeDtypeStruct((b, s, 2 * LANES), F32),
        compiler_params=_cparams("arbitrary", "arbitrary"),
        name="nsa_selected",
    )(flags, qa, nsel, gates, part, ksks, vsvs, kconst, sconst)


def _dil_body(q_ref, kp_ref, kc_ref, vp_ref, vc_ref, bias_ref, o_ref, lse_ref, *, r, m):
    i = pl.program_id(1)
    key_row = lax.broadcasted_iota(jnp.int32, (2 * QB, QB), 0)
    first = key_row >= jnp.where(i > 0, 0, QB)

    def rows(c, u):
        return pl.ds(u * QB * r + c, QB, stride=r) if r > 1 else pl.ds(u * QB, QB)

    for c in range(r):
        for u in range(m):
            cur = rows(c, u)
            q = q_ref[0, cur, :].astype(BF16)
            if u > 0:
                k_prev, v_prev = kc_ref[0, rows(c, u - 1), :], vc_ref[0, rows(c, u - 1), :]
            else:
                k_prev, v_prev = kp_ref[0, rows(c, m - 1), :], vp_ref[0, rows(c, m - 1), :]
            kk = jnp.concatenate([k_prev, kc_ref[0, cur, :]], axis=0).astype(BF16)
            vv = jnp.concatenate([v_prev, vc_ref[0, cur, :]], axis=0).astype(BF16)
            outs, lses = [], []
            for hh in range(2):
                s = _nt_dot(kk, _mask_half(q, hh)) + bias_ref[hh]
                if u == 0:
                    s = jnp.where(first, s, NEG)
                mx = jnp.max(s, axis=0, keepdims=True)
                p = jnp.exp2(s - mx)
                l = jnp.sum(p, axis=0, keepdims=True)
                o = _tn_dot(vv, p.astype(BF16)) * (1.0 / l)
                outs.append(o[hh * HEAD_DIM:(hh + 1) * HEAD_DIM])
                lses.append(jnp.broadcast_to(mx + jnp.log2(l), (HEAD_DIM, QB)))
            o_ref[0, cur, :] = jnp.concatenate(outs, axis=0).T
            lse_ref[0, cur, :] = jnp.concatenate(lses, axis=0).T


def _dilated(dq, dk, dv, bias, g, r, span=512):
    b, s, _ = dq.shape
    m = max(span // (QB * r), 1)
    span = m * QB * r
    cur = lambda bi, i: (bi, i, g)
    prev = lambda bi, i: (bi, jnp.maximum(i - 1, 0), g)
    return pl.pallas_call(
        functools.partial(_dil_body, r=r, m=m),
        grid=(b, s // span),
        in_specs=[pl.BlockSpec((1, span, LANES), cur),
                  pl.BlockSpec((1, span, LANES), prev), pl.BlockSpec((1, span, LANES), cur),
                  pl.BlockSpec((1, span, LANES), prev), pl.BlockSpec((1, span, LANES), cur),
                  pl.BlockSpec((2, 2 * QB, QB), lambda bi, i: (0, 0, 0))],
        out_specs=[pl.BlockSpec((1, span, LANES), lambda bi, i: (bi, i, 0))] * 2,
        out_shape=[jax.ShapeDtypeStruct((b, s, LANES), F32)] * 2,
        compiler_params=_cparams("parallel", "parallel"),
        name=f"dilated_r{r}",
    )(dq, dk, dk, dv, dv, bias)


def _dil_bias(g):
    w, r = DIL_PAIRS[g]
    assert w // r == QB
    iq = np.arange(QB)[None, :]
    jk = np.arange(2 * QB)[:, None]
    dist = iq + QB - jk
    out = np.empty((2, 2 * QB, QB), np.float32)
    for hh in range(2):
        slope = SLOPES_DIL[2 * g + hh]
        out[hh] = np.where((dist >= 0) & (dist <= QB), -slope * LOG2E * r * dist, NEG)
    return jnp.asarray(out)


def _sb_body(q_ref, k_ref, v_ref, lt_ref, o_ref, acc_ref):
    i = pl.program_id(2)
    tq = q_ref.shape[1]
    q = q_ref[0]
    lt = lt_ref[...]
    dmat = lax.broadcasted_iota(jnp.int32, (tq, tq), 0) - lax.broadcasted_iota(jnp.int32, (tq, tq), 1)
    qms = [_mask_half(q, hh) for hh in range(2)]
    acc_ref[...] = jnp.zeros(acc_ref.shape, F32)

    def cond(c):
        jt, _, _, cmax = c
        return (jt >= 0) & (cmax > SB_UNDERFLOW * LOG2E)

    def body(c):
        jt, carry0, carry1, _ = c
        carries = [carry0, carry1]
        pvs = [[], []]
        for u in range(2):
            ju = jt - u
            k0 = pl.multiple_of(jnp.maximum(ju, 0) * tq, tq)
            kt = k_ref[0, pl.ds(k0, tq), :]
            vt = v_ref[0, pl.ds(k0, tq), :]
            mask = dmat < jnp.where(ju < 0, -tq, jnp.where(ju < i, tq, 0))
            for hh in range(2):
                z = _nt_dot(kt, qms[hh])
                lb = jnp.minimum(z, 0.0) - jnp.log2(1.0 + jnp.exp2(-jnp.abs(z)))
                lf = jnp.where(mask, lb - z, 0.0)
                hi = lf.astype(BF16)
                lo = (lf - hi.astype(F32)).astype(BF16)
                later = _dot(lt, hi) + _dot(lt, lo)
                a = jnp.where(mask, jnp.exp2(lb + later + carries[hh]), 0.0)
                pvs[hh].append(_tn_dot(vt, a.astype(BF16))[hh * HEAD_DIM:(hh + 1) * HEAD_DIM])
                carries[hh] = carries[hh] + jnp.sum(lf, axis=0, keepdims=True)
        for hh in range(2):
            acc_ref[hh] += pvs[hh][0] + pvs[hh][1]
        cmax = jnp.maximum(jnp.max(carries[0]), jnp.max(carries[1]))
        return jt - 2, carries[0], carries[1], cmax

    zero = jnp.zeros((1, tq), F32)
    lax.while_loop(cond, body, (i, zero, zero, jnp.float32(0.0)))
    o_ref[0] = jnp.concatenate([acc_ref[0], acc_ref[1]], axis=0).T


def _stick_breaking(sq, sk, sv, lt):
    b, s, _ = sq.shape
    npair = H_SB // 2
    tq = lt.shape[0]
    return pl.pallas_call(
        _sb_body,
        grid=(b, npair, s // tq),
        in_specs=[pl.BlockSpec((1, tq, LANES), lambda bi, hp, i: (bi, i, hp)),
                  pl.BlockSpec((1, s, LANES), lambda bi, hp, i: (bi, 0, hp)),
                  pl.BlockSpec((1, s, LANES), lambda bi, hp, i: (bi, 0, hp)),
                  pl.BlockSpec((tq, tq), lambda bi, hp, i: (0, 0))],
        out_specs=pl.BlockSpec((1, tq, LANES), lambda bi, hp, i: (bi, i, hp)),
        out_shape=jax.ShapeDtypeStruct((b, s, npair * LANES), F32),
        scratch_shapes=[pltpu.VMEM((2, HEAD_DIM, tq), F32)],
        compiler_params=_cparams("parallel", "parallel", "parallel"),
        name="stick_breaking",
    )(sq, sk, sv, lt)


def _out_body(x_ref, oa_ref, d0, l0, d1, l1, d2, l2, oc_ref, w_ref, o_ref):
    lses = [l0[...], l1[...], l2[...]]
    m = jnp.maximum(jnp.maximum(lses[0], lses[1]), lses[2])
    es = [jnp.exp2(l - m) for l in lses]
    den = es[0] + es[1] + es[2]
    ob = (es[0] * d0[...] + es[1] * d1[...] + es[2] * d2[...]) / den
    na = H_NSA * HEAD_DIM
    nb = na + LANES
    acc = _dot(oa_ref[...].astype(BF16), w_ref[0:na, :])
    acc = acc + _dot(ob.astype(BF16), w_ref[na:nb, :])
    acc = acc + _dot(oc_ref[...].astype(BF16), w_ref[nb:, :])
    o_ref[...] = x_ref[...] + acc


def _out_proj(x2d, oa, dil, oc, w_out, tm=512):
    n = x2d.shape[0]
    row = lambda w: pl.BlockSpec((tm, w), lambda i: (i, 0))
    d_cat = w_out.shape[0]
    ins = [x2d, oa]
    specs = [row(D_MODEL), row(2 * LANES)]
    for o, lse in dil:
        ins += [o, lse]
        specs += [row(LANES), row(LANES)]
    ins += [oc, w_out]
    specs += [row(oc.shape[1]), pl.BlockSpec((d_cat, D_MODEL), lambda i: (0, 0))]
    return pl.pallas_call(
        _out_body,
        grid=(n // tm,),
        in_specs=specs,
        out_specs=row(D_MODEL),
        out_shape=jax.ShapeDtypeStruct((n, D_MODEL), F32),
        compiler_params=_cparams("parallel"),
        name="out_proj",
    )(*ins)


def _mlp_body(x_ref, nw_ref, wu_ref, wd_ref, o_ref, *, fc):
    x = x_ref[...]
    ms = jnp.mean(x * x, axis=-1, keepdims=True)
    h = (x * lax.rsqrt(ms + RMS_EPS) * nw_ref[...]).astype(BF16)
    acc = x
    for c in range(D_FF // fc):
        u = jnp.maximum(_dot(h, wu_ref[:, c * fc:(c + 1) * fc]), 0.0)
        acc = acc + _dot((u * u).astype(BF16), wd_ref[c * fc:(c + 1) * fc, :])
    o_ref[...] = acc


def _mlp(x2d, nw, wu, wd, tm=512, fc=1024):
    n = x2d.shape[0]
    const = dict(pipeline_mode=pl.Buffered(1))
    return pl.pallas_call(
        functools.partial(_mlp_body, fc=fc),
        grid=(n // tm,),
        in_specs=[pl.BlockSpec((tm, D_MODEL), lambda i: (i, 0)),
                  pl.BlockSpec((1, D_MODEL), lambda i: (0, 0)),
                  pl.BlockSpec((D_MODEL, D_FF), lambda i: (0, 0), **const),
                  pl.BlockSpec((D_FF, D_MODEL), lambda i: (0, 0), **const)],
        out_specs=pl.BlockSpec((tm, D_MODEL), lambda i: (i, 0)),
        out_shape=jax.ShapeDtypeStruct((n, D_MODEL), F32),
        compiler_params=_cparams("parallel"),
        name="mlp",
    )(x2d, nw, wu, wd)


def _relayout_in_weight(w, g_nsa, g_dil):
    hd = HEAD_DIM
    kv0 = H_NSA * hd
    g0 = kv0 + 6 * hd
    b0 = g0 + 3 * H_NSA
    c0 = b0 + 3 * H_DIL * hd
    w = w.T
    kv = lambda c: w[kv0 + c * hd:kv0 + (c + 1) * hd]
    gate = jnp.pad(w[g0:b0], ((0, LANES - 3 * H_NSA), (0, 0)))
    nd = H_DIL * hd
    w_re = jnp.concatenate([w[:kv0], kv(2), kv(2), kv(4), kv(4), w[b0:b0 + 2 * nd],
                            kv(0), kv(1), kv(3), kv(3), kv(5), kv(5), gate, w[b0 + 2 * nd:c0], w[c0:]],
                           axis=0).astype(BF16)
    one = lambda n: jnp.ones((n,), F32)
    cv = jnp.concatenate([
        jnp.tile(g_nsa[0], H_NSA) * (SCALE * LOG2E), jnp.tile(g_nsa[2], 2), jnp.tile(g_nsa[3], 2),
        jnp.tile(g_dil[0], H_DIL) * (SCALE * LOG2E), jnp.tile(g_dil[1], H_DIL),
        one(4 * LANES), one(nd), one(H_SB * hd) * (SCALE * LOG2E), one(2 * H_SB * hd)])
    return w_re, cv.reshape(1, _IN_COLS)


def _relayout_cmp(w1, pe, w2, gk):
    hd, half = HEAD_DIM, CMP_LEN // 2
    w1r = w1.reshape(2, 2, half, hd, CMP_HIDDEN)
    per = pe.reshape(2, 2, half, hd)
    w1i, pei = [], []
    for c in range(2):
        pad = ((0, 0), (0, 0), (0, hd), (0, 0)) if c == 0 else ((0, 0), (0, 0), (hd, 0), (0, 0))
        w1i.append(jnp.pad(w1r[c], pad).reshape(2, half * LANES, CMP_HIDDEN))
        pei.append(jnp.pad(per[c], pad[:3]).reshape(2, half * LANES))
    w1i = jnp.concatenate(w1i, axis=0).astype(BF16)
    pe8 = jnp.pad(jnp.concatenate(pei, axis=0), ((0, 4), (0, 0)))
    w2r = jnp.concatenate([w2, w2], axis=-1).astype(BF16)
    return w1i, pe8, w2r, jnp.tile(gk, 2).reshape(1, LANES)


def _constants(s):
    ncp = s // CMP_STRIDE
    n_cmp = (s - CMP_LEN) // CMP_STRIDE + 1
    n = np.arange(ncp)[:, None]
    j = np.arange(SEL_LANES)[None, :]
    ov = ((CMP_STRIDE * n <= SEL_BLOCK * j + SEL_BLOCK - 1) & (CMP_STRIDE * n + CMP_LEN - 1 >= SEL_BLOCK * j)
          & (n < n_cmp) & (j < s // SEL_BLOCK))
    pos = np.arange(s)
    kconst = np.zeros((s, 2 * LANES), np.float32)
    kconst[:, HEAD_DIM:HEAD_DIM + 3] = (pos // SEL_BLOCK * SEL_BLOCK)[:, None]
    kconst[:, HEAD_DIM + 3:HEAD_DIM + 6] = (pos % SEL_BLOCK)[:, None]
    kconst[:, LANES:] = np.where(pos[:, None] // SEL_BLOCK == j, NEG, 0.0)
    sconst = np.zeros((8, LANES), np.float32)
    for h in range(H_NSA):
        rest = np.float32(SLOPES_NSA[h] * LOG2E)
        for c in range(3):
            piece = rest.astype(BF16).astype(np.float32)
            sconst[h, HEAD_DIM + c] = sconst[h, HEAD_DIM + 3 + c] = piece
            rest = np.float32(rest - piece)
    a2 = np.arange(2 * QB)
    lt = (a2[None, :] > a2[:, None]).astype(np.float32)
    a = np.arange(_SLAB)
    gm = (a[:, None] // HEAD_DIM == a[None, :] // HEAD_DIM).astype(np.float32) / HEAD_DIM
    return dict(ovt=jnp.asarray(ov.T, BF16), kconst=jnp.asarray(kconst, BF16), sconst=jnp.asarray(sconst),
                lt=jnp.asarray(lt, BF16), gm=jnp.asarray(gm, BF16),
                dil_bias=[_dil_bias(g) for g in range(len(DIL_PAIRS))])


def kernel(x, norm_mix, norm_mlp, w_in, qk_gain_nsa, qk_gain_dil, cmp_pe, cmp_w1, cmp_w2, w_out, w_up, w_down):
    b, s, d = x.shape
    assert d == D_MODEL and s % (DIL_PAIRS[-1][1] * QB) == 0 and s // SEL_BLOCK <= SEL_LANES
    assert s >= WIN_NSA + QB
    n = b * s
    cst = _constants(s)
    x2d = x.reshape(n, d)
    for l in range(w_in.shape[0]):
        w_re, cv = _relayout_in_weight(w_in[l], qk_gain_nsa[l], qk_gain_dil[l])
        qa, ksks, kwkw, dq, dk, kvc, vsvs, vwvw, gates, dv, sq, sk, sv = _in_proj(
            x2d, norm_mix[l].reshape(1, d), w_re, cv, cst["gm"])
        tok = lambda a: a.reshape(b, s, a.shape[-1])
        w1i, pe8, w2r, gk = _relayout_cmp(cmp_w1[l], cmp_pe[l], cmp_w2[l], qk_gain_nsa[l, 1])
        kcr, vcr = _compress(tok(kvc), w1i, pe8, w2r, gk)
        part, nsel, used = _nsa_cmp_win(tok(qa), kcr, vcr, tok(kwkw), tok(vwvw), tok(gates), cst["ovt"])
        oa = _nsa_selected(used[:, :, 0, :].reshape(n // QB, SEL_LANES), tok(qa), nsel, tok(gates), part,
                           tok(ksks), tok(vsvs), cst["kconst"], cst["sconst"])
        dil = []
        for g, (_, r) in enumerate(DIL_PAIRS):
            o, lse = _dilated(tok(dq), tok(dk), tok(dv), cst["dil_bias"][g], g, r)
            dil.append((o.reshape(n, LANES), lse.reshape(n, LANES)))
        oc = _stick_breaking(tok(sq), tok(sk), tok(sv), cst["lt"])
        x2d = _out_proj(x2d, oa.reshape(n, 2 * LANES), dil, oc.reshape(n, -1), w_out[l].astype(BF16))
        x2d = _mlp(x2d, norm_mlp[l].reshape(1, d), w_up[l].astype(BF16), w_down[l].astype(BF16))
    return x2d.reshape(b, s, d)
```

```python
import functools
import math

import numpy as np
import jax
import jax.numpy as jnp
from jax import lax
from jax.experimental import pallas as pl
from jax.experimental.pallas import tpu as pltpu

F32 = jnp.float32
BF16 = jnp.bfloat16

D_MODEL = 1024
HEAD_DIM = 64
H_NSA = 4
H_DIL = 6
H_SB = 6
DIL_PAIRS = ((128, 1), (512, 4), (2048, 16))
CMP_LEN = 32
CMP_STRIDE = 16
CMP_HIDDEN = 128
SEL_BLOCK = 64
SEL_TOPK = 16
WIN_NSA = 512
D_FF = 4 * D_MODEL
RMS_EPS = 1e-6
NEG = -1e30
FORCE_BONUS = 1e4
LOG2E = 1.4426950408889634
SCALE = HEAD_DIM ** -0.5
LANES = 128
QB = 128
SEL_LANES = 128
SB_UNDERFLOW = -104.0

_SLOPES = [2.0 ** (-8.0 * i / (H_NSA + H_DIL)) for i in range(1, H_NSA + H_DIL + 1)]
SLOPES_DIL = _SLOPES[:H_DIL]
SLOPES_NSA = _SLOPES[H_DIL:]

_IN_SEGS = (("qa", 2, "norm"), ("ksks", 1, "norm"), ("kwkw", 1, "norm"), ("dq", 3, "norm"), ("dk", 3, "norm"),
            ("kvc", 1, "raw"), ("vsvs", 1, "raw"), ("vwvw", 1, "raw"), ("gate", 1, "gate"),
            ("dv", 3, "raw"), ("sq", 3, "raw"), ("sk", 3, "raw"), ("sv", 3, "raw"))
_IN_COLS = sum(n for _, n, _ in _IN_SEGS) * LANES
_SLAB = 2 * LANES
_F32_SEGS = ("kvc", "dq", "dk", "dv")
_VMEM_LIMIT = 56 * 1024 * 1024


def _cparams(*sem, vmem=_VMEM_LIMIT):
    return pltpu.CompilerParams(dimension_semantics=sem, vmem_limit_bytes=vmem)


def _nt_dot(a, b):
    return lax.dot_general(a, b, (((1,), (1,)), ((), ())), preferred_element_type=F32)


def _dot(a, b):
    return jnp.dot(a, b, preferred_element_type=F32)


def _tn_dot(a, b):
    return lax.dot_general(a, b, (((0,), (0,)), ((), ())), preferred_element_type=F32)


def _split_dot(x, m):
    hi = x.astype(BF16)
    lo = (x - hi.astype(F32)).astype(BF16)
    return _dot(hi, m) + _dot(lo, m)


def _mask_half(x, hh):
    lane = lax.broadcasted_iota(jnp.int32, x.shape, x.ndim - 1)
    keep = (lane % LANES < HEAD_DIM) if hh == 0 else (lane % LANES >= HEAD_DIM)
    return jnp.where(keep, x, jnp.zeros_like(x))


def _in_proj_body(x_ref, nw_ref, w_ref, cv_ref, gm_ref, *out_refs):
    x = x_ref[...]
    ms = jnp.mean(x * x, axis=-1, keepdims=True)
    h = (x * lax.rsqrt(ms + RMS_EPS) * nw_ref[...]).astype(BF16)
    gm = gm_ref[...]
    tiles = [(o_ref, t, kind) for (name, ntile, kind), o_ref in zip(_IN_SEGS, out_refs) for t in range(ntile)]
    for sl in range(_IN_COLS // _SLAB):
        c0 = sl * _SLAB
        y = _nt_dot(h, w_ref[c0:c0 + _SLAB, :])
        if tiles[2 * sl][2] == "norm":
            msq = _dot((y * y).astype(BF16), gm)
            y = y * lax.rsqrt(msq + RMS_EPS)
        y = y * cv_ref[:, c0:c0 + _SLAB]
        for half in range(2):
            o_ref, t, kind = tiles[2 * sl + half]
            yh = y[:, half * LANES:(half + 1) * LANES]
            if kind == "gate":
                yh = jax.nn.sigmoid(yh)
            o_ref[:, t * LANES:(t + 1) * LANES] = yh.astype(o_ref.dtype)


def _in_proj(x2d, nw, w_re, cv, gm, tm=512):
    n = x2d.shape[0]
    out_shape, out_specs = [], []
    for name, ntile, kind in _IN_SEGS:
        dt = F32 if kind == "gate" or name in _F32_SEGS else BF16
        out_shape.append(jax.ShapeDtypeStruct((n, ntile * LANES), dt))
        out_specs.append(pl.BlockSpec((tm, ntile * LANES), lambda i: (i, 0)))
    return pl.pallas_call(
        _in_proj_body,
        grid=(n // tm,),
        in_specs=[pl.BlockSpec((tm, D_MODEL), lambda i: (i, 0)),
                  pl.BlockSpec((1, D_MODEL), lambda i: (0, 0)),
                  pl.BlockSpec((_IN_COLS, D_MODEL), lambda i: (0, 0)),
                  pl.BlockSpec((1, _IN_COLS), lambda i: (0, 0)),
                  pl.BlockSpec((_SLAB, _SLAB), lambda i: (0, 0))],
        out_specs=out_specs,
        out_shape=out_shape,
        compiler_params=_cparams("parallel"),
        name="in_proj",
    )(x2d, nw, w_re, cv, gm)


def _gelu_tanh(x):
    return 0.5 * x * (1.0 + jnp.tanh(0.7978845608028654 * (x + 0.044715 * (x * x * x))))


def _cmp_body(x_ref, w1_ref, pe_ref, w2_ref, gk_ref, kc_ref, vc_ref):
    ncp = kc_ref.shape[1]
    pe = pe_ref[...].astype(BF16)
    x = jnp.concatenate([x_ref[0, pl.ds(l, ncp, stride=CMP_STRIDE), :].astype(BF16) for l in range(CMP_STRIDE)],
                        axis=1)
    outs = []
    for c in range(2):
        top = _dot(x, w1_ref[2 * c])
        bot = _dot(x, w1_ref[2 * c + 1])
        bias = _dot(pe, w1_ref[2 * c])[2 * c:2 * c + 1] + _dot(pe, w1_ref[2 * c + 1])[2 * c + 1:2 * c + 2]
        hid = top + pltpu.roll(bot, ncp - 1, 0) + bias
        outs.append(_dot(_gelu_tanh(hid).astype(BF16), w2_ref[c]))
    kc = outs[0]
    kc = kc * lax.rsqrt(jnp.mean(kc * kc, axis=-1, keepdims=True) + RMS_EPS) * gk_ref[...]
    kc_ref[0] = kc.astype(BF16)
    vc_ref[0] = outs[1].astype(BF16)


def _compress(kvc, w1i, pe8, w2r, gk):
    b, s, _ = kvc.shape
    ncp, wid = s // CMP_STRIDE, CMP_STRIDE * LANES
    return pl.pallas_call(
        _cmp_body,
        grid=(b,),
        in_specs=[pl.BlockSpec((1, s, LANES), lambda i: (i, 0, 0)),
                  pl.BlockSpec((4, wid, CMP_HIDDEN), lambda i: (0, 0, 0)),
                  pl.BlockSpec((8, wid), lambda i: (0, 0)),
                  pl.BlockSpec((2, CMP_HIDDEN, LANES), lambda i: (0, 0, 0)),
                  pl.BlockSpec((1, LANES), lambda i: (0, 0))],
        out_specs=[pl.BlockSpec((1, ncp, LANES), lambda i: (i, 0, 0)),
                   pl.BlockSpec((1, ncp, LANES), lambda i: (i, 0, 0))],
        out_shape=[jax.ShapeDtypeStruct((b, ncp, LANES), BF16)] * 2,
        compiler_params=_cparams("parallel"),
        name="nsa_compress",
    )(kvc, w1i, pe8, w2r, gk)


def _softmax_cols(s, mask):
    m = jnp.max(s, axis=0, keepdims=True)
    p = jnp.where(mask, jnp.exp2(s - m), 0.0)
    l = jnp.maximum(jnp.sum(p, axis=0, keepdims=True), 1e-30)
    return p, l


def _cw_body(q_ref, kc_ref, vc_ref, kw_ref, vw_ref, g_ref, ovt_ref, oa_ref, ns_ref, fl_ref):
    i = pl.program_id(1)
    t0 = i * QB
    q = q_ref[0]
    gt = g_ref[0].T
    kc = kc_ref[0]
    vc = vc_ref[0]
    ncp = kc.shape[0]
    wk = WIN_NSA + QB

    n_row = lax.broadcasted_iota(jnp.int32, (ncp, QB), 0)
    q_lane = lax.broadcasted_iota(jnp.int32, (ncp, QB), 1)
    vis = (t0 - (CMP_LEN - 1)) + q_lane - CMP_STRIDE * n_row >= 0
    cend = (CMP_STRIDE * n_row + (CMP_LEN - 1)).astype(F32)

    start = pl.multiple_of(jnp.maximum(t0 - WIN_NSA, 0), QB)
    kw = kw_ref[0, pl.ds(start, wk), :]
    vw = vw_ref[0, pl.ds(start, wk), :]
    j_row = lax.broadcasted_iota(jnp.int32, (wk, QB), 0)
    r_lane = lax.broadcasted_iota(jnp.int32, (wk, QB), 1)
    dw = (t0 - start) + r_lane - j_row
    wmask = (dw >= 0) & (dw < WIN_NSA)
    kposw = (start + j_row).astype(F32)

    vlane = lax.broadcasted_iota(jnp.int32, (1, LANES), 1) < HEAD_DIM
    vc = jnp.where(vlane, vc, jnp.ones((), BF16))
    vw = jnp.where(vlane, vw, jnp.ones((), BF16))
    has_key = vis[0:1, :].astype(F32)

    psum = jnp.zeros((ncp, QB), F32)
    o_heads = []
    for h in range(H_NSA):
        slab = q[:, (h // 2) * LANES:(h // 2 + 1) * LANES]
        qm = _mask_half(slab, h % 2)
        sl = SLOPES_NSA[h] * LOG2E
        s = jnp.where(vis, _nt_dot(kc, qm) + sl * cend, NEG)
        p = jnp.exp2(s - jnp.max(s, axis=0, keepdims=True))
        pv = _tn_dot(vc, p.astype(BF16))
        rl = has_key / pv[HEAD_DIM:HEAD_DIM + 1]
        psum = psum + p * rl
        oc = pv[:HEAD_DIM] * (rl * gt[3 * h:3 * h + 1])
        s = jnp.where(wmask, _nt_dot(kw, qm) + sl * kposw, NEG)
        p = jnp.exp2(s - jnp.max(s, axis=0, keepdims=True))
        pv = _tn_dot(vw, p.astype(BF16))
        ow = pv[:HEAD_DIM] * (gt[3 * h + 2:3 * h + 3] / pv[HEAD_DIM:HEAD_DIM + 1])
        o_heads.append(oc + ow)
    for hp in range(H_NSA // 2):
        oa_ref[0, :, hp * LANES:(hp + 1) * LANES] = jnp.concatenate(o_heads[2 * hp:2 * hp + 2], axis=0).T

    hi = psum.astype(BF16)
    lo = (psum - hi.astype(F32)).astype(BF16)
    imp = _dot(ovt_ref[...], hi) + _dot(ovt_ref[...], lo)
    j = lax.broadcasted_iota(jnp.int32, (SEL_LANES, QB), 0)
    qi = lax.broadcasted_iota(jnp.int32, (SEL_LANES, QB), 1)
    cur = jnp.right_shift(t0 + qi, int(math.log2(SEL_BLOCK)))
    forced = (j == 0) | (j == cur) | (j == cur - 1)
    imp = jnp.where(forced, -3e38, jnp.where(j <= cur, imp, NEG))
    jf = j.astype(F32)
    notsel = jnp.where(forced, 0.0, 1.0)
    for _ in range(SEL_TOPK - 3):
        mx = jnp.max(imp, axis=0, keepdims=True)
        idx = jnp.min(jnp.where(imp == mx, jf, float(SEL_LANES)), axis=0, keepdims=True)
        hit = jf == idx
        notsel = jnp.where(hit, 0.0, notsel)
        imp = jnp.where(hit, -3e38, imp)
    nst = notsel.T
    ns_ref[0] = nst.astype(BF16)
    used = 1.0 - jnp.min(nst, axis=0, keepdims=True)
    fl_ref[0, 0] = jnp.broadcast_to(used, (8, SEL_LANES)).astype(jnp.int32)


def _nsa_cmp_win(qa, kcr, vcr, kwkw, vwvw, gates, ovt):
    b, s, _ = qa.shape
    ncp = kcr.shape[1]
    return pl.pallas_call(
        _cw_body,
        grid=(b, s // QB),
        in_specs=[pl.BlockSpec((1, QB, 2 * LANES), lambda bi, i: (bi, i, 0)),
                  pl.BlockSpec((1, ncp, LANES), lambda bi, i: (bi, 0, 0)),
                  pl.BlockSpec((1, ncp, LANES), lambda bi, i: (bi, 0, 0)),
                  pl.BlockSpec((1, s, LANES), lambda bi, i: (bi, 0, 0)),
                  pl.BlockSpec((1, s, LANES), lambda bi, i: (bi, 0, 0)),
                  pl.BlockSpec((1, QB, LANES), lambda bi, i: (bi, i, 0)),
                  pl.BlockSpec((SEL_LANES, ncp), lambda bi, i: (0, 0))],
        out_specs=[pl.BlockSpec((1, QB, 2 * LANES), lambda bi, i: (bi, i, 0)),
                   pl.BlockSpec((1, QB, SEL_LANES), lambda bi, i: (bi, i, 0)),
                   pl.BlockSpec((1, 1, 8, SEL_LANES), lambda bi, i: (bi, i, 0, 0))],
        out_shape=[jax.ShapeDtypeStruct((b, s, 2 * LANES), F32),
                   jax.ShapeDtypeStruct((b, s, SEL_LANES), BF16),
                   jax.ShapeDtypeStruct((b, s // QB, 8, SEL_LANES), jnp.int32)],
        compiler_params=_cparams("parallel", "parallel"),
        name="nsa_cmp_win",
    )(qa, kcr, vcr, kwkw, vwvw, gates, ovt)


def _sel_body(fl_ref, q_ref, ns_ref, g_ref, part_ref, ks_ref, vs_ref, kc_ref, sc_ref, oa_ref,
              kaug_ref, vaug_ref, qaug_ref, m_ref, acc_ref, sa_ref, sb_ref, tl_ref, *, tk):
    i = pl.program_id(1)
    t0 = i * QB
    nq = H_NSA * QB
    last_tile = kaug_ref.shape[0] // tk - 1

    @pl.when(i == 0)
    def _():
        lane = lax.broadcasted_iota(jnp.int32, (kaug_ref.shape[0], LANES), 1)
        kaug_ref[:, 0:LANES] = jnp.where(lane < HEAD_DIM, ks_ref[0], kc_ref[:, 0:LANES])
        kaug_ref[:, LANES:2 * LANES] = kc_ref[:, LANES:2 * LANES]
        vaug_ref[...] = jnp.where(lane < HEAD_DIM, vs_ref[0], jnp.ones((), BF16))

    q = q_ref[0]
    ns = ns_ref[0]
    lane = lax.broadcasted_iota(jnp.int32, (QB, LANES), 1)
    for h in range(H_NSA):
        slab = q[:, (h // 2) * LANES:(h // 2 + 1) * LANES].astype(F32)
        if h % 2:
            slab = pltpu.roll(slab, HEAD_DIM, 1)
        qaug_ref[h * QB:(h + 1) * QB, 0:LANES] = jnp.where(lane < HEAD_DIM, slab, sc_ref[h:h + 1, :]).astype(BF16)
        qaug_ref[h * QB:(h + 1) * QB, LANES:2 * LANES] = ns
    m_ref[...] = jnp.full(m_ref.shape, NEG, F32)
    acc_ref[...] = jnp.zeros(acc_ref.shape, F32)

    key_row = lax.broadcasted_iota(jnp.int32, (tk, nq), 0)
    q_lane = lax.broadcasted_iota(jnp.int32, (tk, nq), 1)
    dmat = key_row - q_lane % QB

    def scores(jt):
        k0 = pl.multiple_of(jnp.minimum(jt, last_tile) * tk, tk)
        return _nt_dot(kaug_ref[pl.ds(k0, tk), :], qaug_ref[...])

    def update(s_ref, jt, diagonal):
        k0 = pl.multiple_of(jnp.minimum(jt, last_tile) * tk, tk)
        s = s_ref[...]
        if diagonal:
            s = jnp.where(dmat <= t0 - jt * tk, s, NEG)
        m_old = m_ref[...]
        m_new = jnp.maximum(m_old, jnp.max(s, axis=0, keepdims=True))
        alpha = jnp.exp2(m_old - m_new)
        p = jnp.exp2(s - m_new)
        m_ref[...] = m_new
        pv = _tn_dot(vaug_ref[pl.ds(k0, tk), :], p.astype(BF16))
        acc_ref[...] = alpha * acc_ref[...] + pv[:acc_ref.shape[0]]

    blocks_per_tile = tk // SEL_BLOCK
    diag_tile = t0 // tk
    frow = pl.program_id(0) * pl.num_programs(1) + i

    def collect(jt, cnt):
        used = fl_ref[frow, jt * blocks_per_tile]
        for c in range(1, blocks_per_tile):
            used = used | fl_ref[frow, jt * blocks_per_tile + c]
        tl_ref[cnt] = jt
        return cnt + used

    n_before = lax.fori_loop(0, diag_tile, collect, 0)
    tl_ref[n_before] = diag_tile
    tl_ref[n_before + 1] = 2 * (last_tile + 1)

    def body(jj, carry):
        sb_ref[...] = scores(tl_ref[2 * jj + 1])
        update(sa_ref, tl_ref[2 * jj], False)
        sa_ref[...] = scores(tl_ref[2 * jj + 2])
        update(sb_ref, tl_ref[2 * jj + 1], False)
        return carry

    n_pairs = n_before // 2
    sa_ref[...] = scores(tl_ref[0])
    lax.fori_loop(0, n_pairs, body, 0)
    sb_ref[...] = scores(tl_ref[2 * n_pairs + 1])
    update(sa_ref, tl_ref[2 * n_pairs], True)
    update(sb_ref, tl_ref[2 * n_pairs + 1], True)

    gt = g_ref[0].T
    for hp in range(H_NSA // 2):
        rows = []
        for hh in range(2):
            h = 2 * hp + hh
            cs = slice(h * QB, (h + 1) * QB)
            rows.append(acc_ref[0:HEAD_DIM, cs] * (gt[3 * h + 1:3 * h + 2] / acc_ref[HEAD_DIM:HEAD_DIM + 1, cs]))
        cs = slice(hp * LANES, (hp + 1) * LANES)
        oa_ref[0, :, cs] = part_ref[0, :, cs] + jnp.concatenate(rows, axis=0).T


def _nsa_selected(flags, qa, nsel, gates, part, ksks, vsvs, kconst, sconst, tk=256):
    b, s, _ = qa.shape
    nq = H_NSA * QB
    grid_spec = pltpu.PrefetchScalarGridSpec(
        num_scalar_prefetch=1,
        grid=(b, s // QB),
        in_specs=[pl.BlockSpec((1, QB, 2 * LANES), lambda bi, i, fl: (bi, i, 0)),
                  pl.BlockSpec((1, QB, SEL_LANES), lambda bi, i, fl: (bi, i, 0)),
                  pl.BlockSpec((1, QB, LANES), lambda bi, i, fl: (bi, i, 0)),
                  pl.BlockSpec((1, QB, 2 * LANES), lambda bi, i, fl: (bi, i, 0)),
                  pl.BlockSpec((1, s, LANES), lambda bi, i, fl: (bi, 0, 0)),
                  pl.BlockSpec((1, s, LANES), lambda bi, i, fl: (bi, 0, 0)),
                  pl.BlockSpec((s, 2 * LANES), lambda bi, i, fl: (0, 0)),
                  pl.BlockSpec((8, LANES), lambda bi, i, fl: (0, 0))],
        out_specs=pl.BlockSpec((1, QB, 2 * LANES), lambda bi, i, fl: (bi, i, 0)),
        scratch_shapes=[pltpu.VMEM((s, 2 * LANES), BF16),
                        pltpu.VMEM((s, LANES), BF16),
                        pltpu.VMEM((nq, 2 * LANES), BF16),
                        pltpu.VMEM((1, nq), F32),
                        pltpu.VMEM((HEAD_DIM + 8, nq), F32),
                        pltpu.VMEM((tk, nq), F32),
                        pltpu.VMEM((tk, nq), F32),
                        pltpu.SMEM((s // tk + 8,), jnp.int32)])
    return pl.pallas_call(
        functools.partial(_sel_body, tk=tk),
        grid_spec=grid_spec,
        out_shape=jax.ShapeDtypeStruct((b, s, 2 * LANES), F32),
        compiler_params=_cparams("arbitrary", "arbitrary"),
        name="nsa_selected",
    )(flags, qa, nsel, gates, part, ksks, vsvs, kconst, sconst)


def _dil_body(q_ref, kp_ref, kc_ref, vp_ref, vc_ref, bias_ref, o_ref, lse_ref, *, r, m):
    i = pl.program_id(1)
    key_row = lax.broadcasted_iota(jnp.int32, (2 * QB, QB), 0)
    first = key_row >= jnp.where(i > 0, 0, QB)

    def rows(c, u):
        return pl.ds(u * QB * r + c, QB, stride=r) if r > 1 else pl.ds(u * QB, QB)

    for c in range(r):
        for u in range(m):
            cur = rows(c, u)
            q = q_ref[0, cur, :].astype(BF16)
            if u > 0:
                k_prev, v_prev = kc_ref[0, rows(c, u - 1), :], vc_ref[0, rows(c, u - 1), :]
            else:
                k_prev, v_prev = kp_ref[0, rows(c, m - 1), :], vp_ref[0, rows(c, m - 1), :]
            kk = jnp.concatenate([k_prev, kc_ref[0, cur, :]], axis=0).astype(BF16)
            vv = jnp.concatenate([v_prev, vc_ref[0, cur, :]], axis=0).astype(BF16)
            outs, lses = [], []
            for hh in range(2):
                s = _nt_dot(kk, _mask_half(q, hh)) + bias_ref[hh]
                if u == 0:
                    s = jnp.where(first, s, NEG)
                mx = jnp.max(s, axis=0, keepdims=True)
                p = jnp.exp2(s - mx)
                l = jnp.sum(p, axis=0, keepdims=True)
                o = _tn_dot(vv, p.astype(BF16)) * (1.0 / l)
                outs.append(o[hh * HEAD_DIM:(hh + 1) * HEAD_DIM])
                lses.append(jnp.broadcast_to(mx + jnp.log2(l), (HEAD_DIM, QB)))
            o_ref[0, cur, :] = jnp.concatenate(outs, axis=0).T
            lse_ref[0, cur, :] = jnp.concatenate(lses, axis=0).T


def _dilated(dq, dk, dv, bias, g, r, span=512):
    b, s, _ = dq.shape
    m = max(span // (QB * r), 1)
    span = m * QB * r
    cur = lambda bi, i: (bi, i, g)
    prev = lambda bi, i: (bi, jnp.maximum(i - 1, 0), g)
    return pl.pallas_call(
        functools.partial(_dil_body, r=r, m=m),
        grid=(b, s // span),
        in_specs=[pl.BlockSpec((1, span, LANES), cur),
                  pl.BlockSpec((1, span, LANES), prev), pl.BlockSpec((1, span, LANES), cur),
                  pl.BlockSpec((1, span, LANES), prev), pl.BlockSpec((1, span, LANES), cur),
                  pl.BlockSpec((2, 2 * QB, QB), lambda bi, i: (0, 0, 0))],
        out_specs=[pl.BlockSpec((1, span, LANES), lambda bi, i: (bi, i, 0))] * 2,
        out_shape=[jax.ShapeDtypeStruct((b, s, LANES), F32)] * 2,
        compiler_params=_cparams("parallel", "parallel"),
        name=f"dilated_r{r}",
    )(dq, dk, dk, dv, dv, bias)


def _dil_bias(g):
    w, r = DIL_PAIRS[g]
    assert w // r == QB
    iq = np.arange(QB)[None, :]
    jk = np.arange(2 * QB)[:, None]
    dist = iq + QB - jk
    out = np.empty((2, 2 * QB, QB), np.float32)
    for hh in range(2):
        slope = SLOPES_DIL[2 * g + hh]
        out[hh] = np.where((dist >= 0) & (dist <= QB), -slope * LOG2E * r * dist, NEG)
    return jnp.asarray(out)


def _sb_body(q_ref, k_ref, v_ref, lt_ref, o_ref, acc_ref):
    i = pl.program_id(2)
    tq = q_ref.shape[1]
    q = q_ref[0]
    lt = lt_ref[...]
    dmat = lax.broadcasted_iota(jnp.int32, (tq, tq), 0) - lax.broadcasted_iota(jnp.int32, (tq, tq), 1)
    qms = [_mask_half(q, hh) for hh in range(2)]
    acc_ref[...] = jnp.zeros(acc_ref.shape, F32)

    def cond(c):
        jt, _, _, cmax = c
        return (jt >= 0) & (cmax > SB_UNDERFLOW * LOG2E)

    def body(c):
        jt, carry0, carry1, _ = c
        carries = [carry0, carry1]
        pvs = [[], []]
        for u in range(2):
            ju = jt - u
            k0 = pl.multiple_of(jnp.maximum(ju, 0) * tq, tq)
            kt = k_ref[0, pl.ds(k0, tq), :]
            vt = v_ref[0, pl.ds(k0, tq), :]
            mask = dmat < jnp.where(ju < 0, -tq, jnp.where(ju < i, tq, 0))
            for hh in range(2):
                z = _nt_dot(kt, qms[hh])
                lb = jnp.minimum(z, 0.0) - jnp.log2(1.0 + jnp.exp2(-jnp.abs(z)))
                lf = jnp.where(mask, lb - z, 0.0)
                later = _dot(lt, lf.astype(BF16))
                a = jnp.where(mask, jnp.exp2(lb + later + carries[hh]), 0.0)
                pvs[hh].append(_tn_dot(vt, a.astype(BF16))[hh * HEAD_DIM:(hh + 1) * HEAD_DIM])
                carries[hh] = carries[hh] + jnp.sum(lf, axis=0, keepdims=True)
        for hh in range(2):
            acc_ref[hh] += pvs[hh][0] + pvs[hh][1]
        cmax = jnp.maximum(jnp.max(carries[0]), jnp.max(carries[1]))
        return jt - 2, carries[0], carries[1], cmax

    zero = jnp.zeros((1, tq), F32)
    lax.while_loop(cond, body, (i, zero, zero, jnp.float32(0.0)))
    o_ref[0] = jnp.concatenate([acc_ref[0], acc_ref[1]], axis=0).T


def _stick_breaking(sq, sk, sv, lt):
    b, s, _ = sq.shape
    npair = H_SB // 2
    tq = lt.shape[0]
    return pl.pallas_call(
        _sb_body,
        grid=(b, npair, s // tq),
        in_specs=[pl.BlockSpec((1, tq, LANES), lambda bi, hp, i: (bi, i, hp)),
                  pl.BlockSpec((1, s, LANES), lambda bi, hp, i: (bi, 0, hp)),
                  pl.BlockSpec((1, s, LANES), lambda bi, hp, i: (bi, 0, hp)),
                  pl.BlockSpec((tq, tq), lambda bi, hp, i: (0, 0))],
        out_specs=pl.BlockSpec((1, tq, LANES), lambda bi, hp, i: (bi, i, hp)),
        out_shape=jax.ShapeDtypeStruct((b, s, npair * LANES), F32),
        scratch_shapes=[pltpu.VMEM((2, HEAD_DIM, tq), F32)],
        compiler_params=_cparams("parallel", "parallel", "parallel"),
        name="stick_breaking",
    )(sq, sk, sv, lt)


def _out_body(x_ref, oa_ref, d0, l0, d1, l1, d2, l2, oc_ref, w_ref, o_ref):
    lses = [l0[...], l1[...], l2[...]]
    m = jnp.maximum(jnp.maximum(lses[0], lses[1]), lses[2])
    es = [jnp.exp2(l - m) for l in lses]
    den = es[0] + es[1] + es[2]
    ob = (es[0] * d0[...] + es[1] * d1[...] + es[2] * d2[...]) / den
    na = H_NSA * HEAD_DIM
    nb = na + LANES
    acc = _dot(oa_ref[...].astype(BF16), w_ref[0:na, :])
    acc = acc + _dot(ob.astype(BF16), w_ref[na:nb, :])
    acc = acc + _dot(oc_ref[...].astype(BF16), w_ref[nb:, :])
    o_ref[...] = x_ref[...] + acc


def _out_proj(x2d, oa, dil, oc, w_out, tm=512):
    n = x2d.shape[0]
    row = lambda w: pl.BlockSpec((tm, w), lambda i: (i, 0))
    d_cat = w_out.shape[0]
    ins = [x2d, oa]
    specs = [row(D_MODEL), row(2 * LANES)]
    for o, lse in dil:
        ins += [o, lse]
        specs += [row(LANES), row(LANES)]
    ins += [oc, w_out]
    specs += [row(oc.shape[1]), pl.BlockSpec((d_cat, D_MODEL), lambda i: (0, 0))]
    return pl.pallas_call(
        _out_body,
        grid=(n // tm,),
        in_specs=specs,
        out_specs=row(D_MODEL),
        out_shape=jax.ShapeDtypeStruct((n, D_MODEL), F32),
        compiler_params=_cparams("parallel"),
        name="out_proj",
    )(*ins)


def _mlp_body(x_ref, nw_ref, wu_ref, wd_ref, o_ref, *, fc):
    x = x_ref[...]
    ms = jnp.mean(x * x, axis=-1, keepdims=True)
    h = (x * lax.rsqrt(ms + RMS_EPS) * nw_ref[...]).astype(BF16)
    acc = x
    for c in range(D_FF // fc):
        u = jnp.maximum(_dot(h, wu_ref[:, c * fc:(c + 1) * fc]), 0.0)
        acc = acc + _dot((u * u).astype(BF16), wd_ref[c * fc:(c + 1) * fc, :])
    o_ref[...] = acc


def _mlp(x2d, nw, wu, wd, tm=512, fc=1024):
    n = x2d.shape[0]
    const = dict(pipeline_mode=pl.Buffered(1))
    return pl.pallas_call(
        functools.partial(_mlp_body, fc=fc),
        grid=(n // tm,),
        in_specs=[pl.BlockSpec((tm, D_MODEL), lambda i: (i, 0)),
                  pl.BlockSpec((1, D_MODEL), lambda i: (0, 0)),
                  pl.BlockSpec((D_MODEL, D_FF), lambda i: (0, 0), **const),
                  pl.BlockSpec((D_FF, D_MODEL), lambda i: (0, 0), **const)],
        out_specs=pl.BlockSpec((tm, D_MODEL), lambda i: (i, 0)),
        out_shape=jax.ShapeDtypeStruct((n, D_MODEL), F32),
        compiler_params=_cparams("parallel"),
        name="mlp",
    )(x2d, nw, wu, wd)


def _relayout_in_weight(w, g_nsa, g_dil):
    hd = HEAD_DIM
    kv0 = H_NSA * hd
    g0 = kv0 + 6 * hd
    b0 = g0 + 3 * H_NSA
    c0 = b0 + 3 * H_DIL * hd
    w = w.T
    kv = lambda c: w[kv0 + c * hd:kv0 + (c + 1) * hd]
    gate = jnp.pad(w[g0:b0], ((0, LANES - 3 * H_NSA), (0, 0)))
    nd = H_DIL * hd
    w_re = jnp.concatenate([w[:kv0], kv(2), kv(2), kv(4), kv(4), w[b0:b0 + 2 * nd],
                            kv(0), kv(1), kv(3), kv(3), kv(5), kv(5), gate, w[b0 + 2 * nd:c0], w[c0:]],
                           axis=0).astype(BF16)
    one = lambda n: jnp.ones((n,), F32)
    cv = jnp.concatenate([
        jnp.tile(g_nsa[0], H_NSA) * (SCALE * LOG2E), jnp.tile(g_nsa[2], 2), jnp.tile(g_nsa[3], 2),
        jnp.tile(g_dil[0], H_DIL) * (SCALE * LOG2E), jnp.tile(g_dil[1], H_DIL),
        one(4 * LANES), one(nd), one(H_SB * hd) * (SCALE * LOG2E), one(2 * H_SB * hd)])
    return w_re, cv.reshape(1, _IN_COLS)


def _relayout_cmp(w1, pe, w2, gk):
    hd, half = HEAD_DIM, CMP_LEN // 2
    w1r = w1.reshape(2, 2, half, hd, CMP_HIDDEN)
    per = pe.reshape(2, 2, half, hd)
    w1i, pei = [], []
    for c in range(2):
        pad = ((0, 0), (0, 0), (0, hd), (0, 0)) if c == 0 else ((0, 0), (0, 0), (hd, 0), (0, 0))
        w1i.append(jnp.pad(w1r[c], pad).reshape(2, half * LANES, CMP_HIDDEN))
        pei.append(jnp.pad(per[c], pad[:3]).reshape(2, half * LANES))
    w1i = jnp.concatenate(w1i, axis=0).astype(BF16)
    pe8 = jnp.pad(jnp.concatenate(pei, axis=0), ((0, 4), (0, 0)))
    w2r = jnp.concatenate([w2, w2], axis=-1).astype(BF16)
    return w1i, pe8, w2r, jnp.tile(gk, 2).reshape(1, LANES)


def _constants(s):
    ncp = s // CMP_STRIDE
    n_cmp = (s - CMP_LEN) // CMP_STRIDE + 1
    n = np.arange(ncp)[:, None]
    j = np.arange(SEL_LANES)[None, :]
    ov = ((CMP_STRIDE * n <= SEL_BLOCK * j + SEL_BLOCK - 1) & (CMP_STRIDE * n + CMP_LEN - 1 >= SEL_BLOCK * j)
          & (n < n_cmp) & (j < s // SEL_BLOCK))
    pos = np.arange(s)
    kconst = np.zeros((s, 2 * LANES), np.float32)
    kconst[:, HEAD_DIM:HEAD_DIM + 3] = (pos // SEL_BLOCK * SEL_BLOCK)[:, None]
    kconst[:, HEAD_DIM + 3:HEAD_DIM + 6] = (pos % SEL_BLOCK)[:, None]
    kconst[:, LANES:] = np.where(pos[:, None] // SEL_BLOCK == j, NEG, 0.0)
    sconst = np.zeros((8, LANES), np.float32)
    for h in range(H_NSA):
        rest = np.float32(SLOPES_NSA[h] * LOG2E)
        for c in range(3):
            piece = rest.astype(BF16).astype(np.float32)
            sconst[h, HEAD_DIM + c] = sconst[h, HEAD_DIM + 3 + c] = piece
            rest = np.float32(rest - piece)
    a2 = np.arange(2 * QB)
    lt = (a2[None, :] > a2[:, None]).astype(np.float32)
    a = np.arange(_SLAB)
    gm = (a[:, None] // HEAD_DIM == a[None, :] // HEAD_DIM).astype(np.float32) / HEAD_DIM
    return dict(ovt=jnp.asarray(ov.T, BF16), kconst=jnp.asarray(kconst, BF16), sconst=jnp.asarray(sconst),
                lt=jnp.asarray(lt, BF16), gm=jnp.asarray(gm, BF16),
                dil_bias=[_dil_bias(g) for g in range(len(DIL_PAIRS))])


def kernel(x, norm_mix, norm_mlp, w_in, qk_gain_nsa, qk_gain_dil, cmp_pe, cmp_w1, cmp_w2, w_out, w_up, w_down):
    b, s, d = x.shape
    assert d == D_MODEL and s % (DIL_PAIRS[-1][1] * QB) == 0 and s // SEL_BLOCK <= SEL_LANES
    assert s >= WIN_NSA + QB
    n = b * s
    cst = _constants(s)
    x2d = x.reshape(n, d)
    for l in range(w_in.shape[0]):
        w_re, cv = _relayout_in_weight(w_in[l], qk_gain_nsa[l], qk_gain_dil[l])
        qa, ksks, kwkw, dq, dk, kvc, vsvs, vwvw, gates, dv, sq, sk, sv = _in_proj(
            x2d, norm_mix[l].reshape(1, d), w_re, cv, cst["gm"])
        tok = lambda a: a.reshape(b, s, a.shape[-1])
        w1i, pe8, w2r, gk = _relayout_cmp(cmp_w1[l], cmp_pe[l], cmp_w2[l], qk_gain_nsa[l, 1])
        kcr, vcr = _compress(tok(kvc), w1i, pe8, w2r, gk)
        part, nsel, used = _nsa_cmp_win(tok(qa), kcr, vcr, tok(kwkw), tok(vwvw), tok(gates), cst["ovt"])
        oa = _nsa_selected(used[:, :, 0, :].reshape(n // QB, SEL_LANES), tok(qa), nsel, tok(gates), part,
                           tok(ksks), tok(vsvs), cst["kconst"], cst["sconst"])
        dil = []
        for g, (_, r) in enumerate(DIL_PAIRS):
            o, lse = _dilated(tok(dq), tok(dk), tok(dv), cst["dil_bias"][g], g, r)
            dil.append((o.reshape(n, LANES), lse.reshape(n, LANES)))
        oc = _stick_breaking(tok(sq), tok(sk), tok(sv), cst["lt"])
        x2d = _out_proj(x2d, oa.reshape(n, 2 * LANES), dil, oc.reshape(n, -1), w_out[l].astype(BF16))
        x2d = _mlp(x2d, norm_mlp[l].reshape(1, d), w_up[l].astype(BF16), w_down[l].astype(BF16))
    return x2d.reshape(b, s, d)
```

```python
import functools
import math

import numpy as np
import jax
import jax.numpy as jnp
from jax import lax
from jax.experimental import pallas as pl
from jax.experimental.pallas import tpu as pltpu

F32 = jnp.float32
BF16 = jnp.bfloat16

D_MODEL = 1024
HEAD_DIM = 64
H_NSA = 4
H_DIL = 6
H_SB = 6
DIL_PAIRS = ((128, 1), (512, 4), (2048, 16))
CMP_LEN = 32
CMP_STRIDE = 16
CMP_HIDDEN = 128
SEL_BLOCK = 64
SEL_TOPK = 16
WIN_NSA = 512
D_FF = 4 * D_MODEL
RMS_EPS = 1e-6
NEG = -1e30
FORCE_BONUS = 1e4
LOG2E = 1.4426950408889634
SCALE = HEAD_DIM ** -0.5
LANES = 128
QB = 128
SEL_LANES = 128
SB_UNDERFLOW = -104.0

_SLOPES = [2.0 ** (-8.0 * i / (H_NSA + H_DIL)) for i in range(1, H_NSA + H_DIL + 1)]
SLOPES_DIL = _SLOPES[:H_DIL]
SLOPES_NSA = _SLOPES[H_DIL:]

_IN_SEGS = (("qa", 2, "norm"), ("ksks", 1, "norm"), ("kwkw", 1, "norm"), ("dq", 3, "norm"), ("dk", 3, "norm"),
            ("kvc", 1, "raw"), ("vsvs", 1, "raw"), ("vwvw", 1, "raw"), ("gate", 1, "gate"),
            ("dv", 3, "raw"), ("sq", 3, "raw"), ("sk", 3, "raw"), ("sv", 3, "raw"))
_IN_COLS = sum(n for _, n, _ in _IN_SEGS) * LANES
_SLAB = 2 * LANES
_F32_SEGS = ("kvc", "dq", "dk", "dv")
_VMEM_LIMIT = 56 * 1024 * 1024


def _cparams(*sem, vmem=_VMEM_LIMIT):
    return pltpu.CompilerParams(dimension_semantics=sem, vmem_limit_bytes=vmem)


def _nt_dot(a, b):
    return lax.dot_general(a, b, (((1,), (1,)), ((), ())), preferred_element_type=F32)


def _dot(a, b):
    return jnp.dot(a, b, preferred_element_type=F32)


def _tn_dot(a, b):
    return lax.dot_general(a, b, (((0,), (0,)), ((), ())), preferred_element_type=F32)


def _split_dot(x, m):
    hi = x.astype(BF16)
    lo = (x - hi.astype(F32)).astype(BF16)
    return _dot(hi, m) + _dot(lo, m)


def _mask_half(x, hh):
    lane = lax.broadcasted_iota(jnp.int32, x.shape, x.ndim - 1)
    keep = (lane % LANES < HEAD_DIM) if hh == 0 else (lane % LANES >= HEAD_DIM)
    return jnp.where(keep, x, jnp.zeros_like(x))


def _in_proj_body(x_ref, nw_ref, w_ref, cv_ref, gm_ref, *out_refs):
    x = x_ref[...]
    ms = jnp.mean(x * x, axis=-1, keepdims=True)
    h = (x * lax.rsqrt(ms + RMS_EPS) * nw_ref[...]).astype(BF16)
    gm = gm_ref[...]
    tiles = [(o_ref, t, kind) for (name, ntile, kind), o_ref in zip(_IN_SEGS, out_refs) for t in range(ntile)]
    for sl in range(_IN_COLS // _SLAB):
        c0 = sl * _SLAB
        y = _nt_dot(h, w_ref[c0:c0 + _SLAB, :])
        if tiles[2 * sl][2] == "norm":
            msq = _dot((y * y).astype(BF16), gm)
            y = y * lax.rsqrt(msq + RMS_EPS)
        y = y * cv_ref[:, c0:c0 + _SLAB]
        for half in range(2):
            o_ref, t, kind = tiles[2 * sl + half]
            yh = y[:, half * LANES:(half + 1) * LANES]
            if kind == "gate":
                yh = jax.nn.sigmoid(yh)
            o_ref[:, t * LANES:(t + 1) * LANES] = yh.astype(o_ref.dtype)


def _in_proj(x2d, nw, w_re, cv, gm, tm=512):
    n = x2d.shape[0]
    out_shape, out_specs = [], []
    for name, ntile, kind in _IN_SEGS:
        dt = F32 if kind == "gate" or name in _F32_SEGS else BF16
        out_shape.append(jax.ShapeDtypeStruct((n, ntile * LANES), dt))
        out_specs.append(pl.BlockSpec((tm, ntile * LANES), lambda i: (i, 0)))
    return pl.pallas_call(
        _in_proj_body,
        grid=(n // tm,),
        in_specs=[pl.BlockSpec((tm, D_MODEL), lambda i: (i, 0)),
                  pl.BlockSpec((1, D_MODEL), lambda i: (0, 0)),
                  pl.BlockSpec((_IN_COLS, D_MODEL), lambda i: (0, 0)),
                  pl.BlockSpec((1, _IN_COLS), lambda i: (0, 0)),
                  pl.BlockSpec((_SLAB, _SLAB), lambda i: (0, 0))],
        out_specs=out_specs,
        out_shape=out_shape,
        compiler_params=_cparams("parallel"),
        name="in_proj",
    )(x2d, nw, w_re, cv, gm)


def _gelu_tanh(x):
    return 0.5 * x * (1.0 + jnp.tanh(0.7978845608028654 * (x + 0.044715 * (x * x * x))))


def _cmp_body(x_ref, w1_ref, pe_ref, w2_ref, gk_ref, kc_ref, vc_ref):
    ncp = kc_ref.shape[1]
    pe = pe_ref[...].astype(BF16)
    x = jnp.concatenate([x_ref[0, pl.ds(l, ncp, stride=CMP_STRIDE), :].astype(BF16) for l in range(CMP_STRIDE)],
                        axis=1)
    outs = []
    for c in range(2):
        top = _dot(x, w1_ref[2 * c])
        bot = _dot(x, w1_ref[2 * c + 1])
        bias = _dot(pe, w1_ref[2 * c])[2 * c:2 * c + 1] + _dot(pe, w1_ref[2 * c + 1])[2 * c + 1:2 * c + 2]
        hid = top + pltpu.roll(bot, ncp - 1, 0) + bias
        outs.append(_dot(_gelu_tanh(hid).astype(BF16), w2_ref[c]))
    kc = outs[0]
    kc = kc * lax.rsqrt(jnp.mean(kc * kc, axis=-1, keepdims=True) + RMS_EPS) * gk_ref[...]
    kc_ref[0] = kc.astype(BF16)
    vc_ref[0] = outs[1].astype(BF16)


def _compress(kvc, w1i, pe8, w2r, gk):
    b, s, _ = kvc.shape
    ncp, wid = s // CMP_STRIDE, CMP_STRIDE * LANES
    return pl.pallas_call(
        _cmp_body,
        grid=(b,),
        in_specs=[pl.BlockSpec((1, s, LANES), lambda i: (i, 0, 0)),
                  pl.BlockSpec((4, wid, CMP_HIDDEN), lambda i: (0, 0, 0)),
                  pl.BlockSpec((8, wid), lambda i: (0, 0)),
                  pl.BlockSpec((2, CMP_HIDDEN, LANES), lambda i: (0, 0, 0)),
                  pl.BlockSpec((1, LANES), lambda i: (0, 0))],
        out_specs=[pl.BlockSpec((1, ncp, LANES), lambda i: (i, 0, 0)),
                   pl.BlockSpec((1, ncp, LANES), lambda i: (i, 0, 0))],
        out_shape=[jax.ShapeDtypeStruct((b, ncp, LANES), BF16)] * 2,
        compiler_params=_cparams("parallel"),
        name="nsa_compress",
    )(kvc, w1i, pe8, w2r, gk)


def _softmax_cols(s, mask):
    m = jnp.max(s, axis=0, keepdims=True)
    p = jnp.where(mask, jnp.exp2(s - m), 0.0)
    l = jnp.maximum(jnp.sum(p, axis=0, keepdims=True), 1e-30)
    return p, l


def _cw_body(q_ref, kc_ref, vc_ref, kw_ref, vw_ref, g_ref, ovt_ref, oa_ref, ns_ref, fl_ref):
    i = pl.program_id(1)
    t0 = i * QB
    q = q_ref[0]
    gt = g_ref[0].T
    kc = kc_ref[0]
    vc = vc_ref[0]
    ncp = kc.shape[0]
    wk = WIN_NSA + QB

    nq = H_NSA * QB
    heads = range(H_NSA)
    q4 = jnp.concatenate([_mask_half(q[:, (h // 2) * LANES:(h // 2 + 1) * LANES], h % 2) for h in heads], axis=0)
    head = lax.broadcasted_iota(jnp.int32, (1, nq), 1) // QB
    slope = jnp.zeros((1, nq), F32)
    for h in heads:
        slope = jnp.where(head == h, SLOPES_NSA[h] * LOG2E, slope)
    gate = lambda br: jnp.concatenate([gt[3 * h + br:3 * h + br + 1] for h in heads], axis=1)

    per_head = lambda a: jnp.concatenate([a] * H_NSA, axis=1)

    n_row = lax.broadcasted_iota(jnp.int32, (ncp, QB), 0)
    q_lane = lax.broadcasted_iota(jnp.int32, (ncp, QB), 1)
    vis = per_head((t0 - (CMP_LEN - 1)) + q_lane - CMP_STRIDE * n_row) >= 0
    cend = per_head((CMP_STRIDE * n_row + (CMP_LEN - 1)).astype(F32))

    start = pl.multiple_of(jnp.maximum(t0 - WIN_NSA, 0), QB)
    kw = kw_ref[0, pl.ds(start, wk), :]
    vw = vw_ref[0, pl.ds(start, wk), :]
    j_row = lax.broadcasted_iota(jnp.int32, (wk, QB), 0)
    r_lane = lax.broadcasted_iota(jnp.int32, (wk, QB), 1)
    dw = per_head((t0 - start) + r_lane - j_row)
    wmask = (dw >= 0) & (dw < WIN_NSA)
    kposw = per_head((start + j_row).astype(F32))

    vlane = lax.broadcasted_iota(jnp.int32, (1, LANES), 1) < HEAD_DIM
    vc = jnp.where(vlane, vc, jnp.ones((), BF16))
    vw = jnp.where(vlane, vw, jnp.ones((), BF16))
    has_key = vis[0:1, :].astype(F32)

    s = jnp.where(vis, _nt_dot(kc, q4) + slope * cend, NEG)
    p = jnp.exp2(s - jnp.max(s, axis=0, keepdims=True))
    pv = _tn_dot(vc, p.astype(BF16))
    rl = has_key / pv[HEAD_DIM:HEAD_DIM + 1]
    pn = p * rl
    psum = pn[:, 0:QB]
    for h in range(1, H_NSA):
        psum = psum + pn[:, h * QB:(h + 1) * QB]
    o = pv[:HEAD_DIM] * (rl * gate(0))
    s = jnp.where(wmask, _nt_dot(kw, q4) + slope * kposw, NEG)
    p = jnp.exp2(s - jnp.max(s, axis=0, keepdims=True))
    pv = _tn_dot(vw, p.astype(BF16))
    o = o + pv[:HEAD_DIM] * (gate(2) / pv[HEAD_DIM:HEAD_DIM + 1])
    for hp in range(H_NSA // 2):
        pair = jnp.concatenate([o[:, (2 * hp) * QB:(2 * hp + 1) * QB], o[:, (2 * hp + 1) * QB:(2 * hp + 2) * QB]], axis=0)
        oa_ref[0, :, hp * LANES:(hp + 1) * LANES] = pair.T

    hi = psum.astype(BF16)
    lo = (psum - hi.astype(F32)).astype(BF16)
    imp = _dot(ovt_ref[...], hi) + _dot(ovt_ref[...], lo)
    j = lax.broadcasted_iota(jnp.int32, (SEL_LANES, QB), 0)
    qi = lax.broadcasted_iota(jnp.int32, (SEL_LANES, QB), 1)
    cur = jnp.right_shift(t0 + qi, int(math.log2(SEL_BLOCK)))
    forced = (j == 0) | (j == cur) | (j == cur - 1)
    imp = jnp.where(forced, -3e38, jnp.where(j <= cur, imp, NEG))
    jf = j.astype(F32)
    notsel = jnp.where(forced, 0.0, 1.0)
    for _ in range(SEL_TOPK - 3):
        mx = jnp.max(imp, axis=0, keepdims=True)
        idx = jnp.min(jnp.where(imp == mx, jf, float(SEL_LANES)), axis=0, keepdims=True)
        hit = jf == idx
        notsel = jnp.where(hit, 0.0, notsel)
        imp = jnp.where(hit, -3e38, imp)
    nst = notsel.T
    ns_ref[0] = nst.astype(BF16)
    used = 1.0 - jnp.min(nst, axis=0, keepdims=True)
    fl_ref[0, 0] = jnp.broadcast_to(used, (8, SEL_LANES)).astype(jnp.int32)


def _nsa_cmp_win(qa, kcr, vcr, kwkw, vwvw, gates, ovt):
    b, s, _ = qa.shape
    ncp = kcr.shape[1]
    return pl.pallas_call(
        _cw_body,
        grid=(b, s // QB),
        in_specs=[pl.BlockSpec((1, QB, 2 * LANES), lambda bi, i: (bi, i, 0)),
                  pl.BlockSpec((1, ncp, LANES), lambda bi, i: (bi, 0, 0)),
                  pl.BlockSpec((1, ncp, LANES), lambda bi, i: (bi, 0, 0)),
                  pl.BlockSpec((1, s, LANES), lambda bi, i: (bi, 0, 0)),
                  pl.BlockSpec((1, s, LANES), lambda bi, i: (bi, 0, 0)),
                  pl.BlockSpec((1, QB, LANES), lambda bi, i: (bi, i, 0)),
                  pl.BlockSpec((SEL_LANES, ncp), lambda bi, i: (0, 0))],
        out_specs=[pl.BlockSpec((1, QB, 2 * LANES), lambda bi, i: (bi, i, 0)),
                   pl.BlockSpec((1, QB, SEL_LANES), lambda bi, i: (bi, i, 0)),
                   pl.BlockSpec((1, 1, 8, SEL_LANES), lambda bi, i: (bi, i, 0, 0))],
        out_shape=[jax.ShapeDtypeStruct((b, s, 2 * LANES), F32),
                   jax.ShapeDtypeStruct((b, s, SEL_LANES), BF16),
                   jax.ShapeDtypeStruct((b, s // QB, 8, SEL_LANES), jnp.int32)],
        compiler_params=_cparams("parallel", "parallel"),
        name="nsa_cmp_win",
    )(qa, kcr, vcr, kwkw, vwvw, gates, ovt)


def _sel_body(fl_ref, q_ref, ns_ref, g_ref, part_ref, ks_ref, vs_ref, kc_ref, sc_ref, oa_ref,
              kaug_ref, vaug_ref, qaug_ref, m_ref, acc_ref, sa_ref, sb_ref, tl_ref, *, tk):
    i = pl.program_id(1)
    t0 = i * QB
    nq = H_NSA * QB
    last_tile = kaug_ref.shape[0] // tk - 1

    @pl.when(i == 0)
    def _():
        lane = lax.broadcasted_iota(jnp.int32, (kaug_ref.shape[0], LANES), 1)
        kaug_ref[:, 0:LANES] = jnp.where(lane < HEAD_DIM, ks_ref[0], kc_ref[:, 0:LANES])
        kaug_ref[:, LANES:2 * LANES] = kc_ref[:, LANES:2 * LANES]
        vaug_ref[...] = jnp.where(lane < HEAD_DIM, vs_ref[0], jnp.ones((), BF16))

    q = q_ref[0]
    ns = ns_ref[0]
    lane = lax.broadcasted_iota(jnp.int32, (QB, LANES), 1)
    for h in range(H_NSA):
        slab = q[:, (h // 2) * LANES:(h // 2 + 1) * LANES].astype(F32)
        if h % 2:
            slab = pltpu.roll(slab, HEAD_DIM, 1)
        qaug_ref[h * QB:(h + 1) * QB, 0:LANES] = jnp.where(lane < HEAD_DIM, slab, sc_ref[h:h + 1, :]).astype(BF16)
        qaug_ref[h * QB:(h + 1) * QB, LANES:2 * LANES] = ns
    m_ref[...] = jnp.full(m_ref.shape, NEG, F32)
    acc_ref[...] = jnp.zeros(acc_ref.shape, F32)

    key_row = lax.broadcasted_iota(jnp.int32, (tk, nq), 0)
    q_lane = lax.broadcasted_iota(jnp.int32, (tk, nq), 1)
    dmat = key_row - q_lane % QB

    def scores(jt):
        k0 = pl.multiple_of(jnp.minimum(jt, last_tile) * tk, tk)
        return _nt_dot(kaug_ref[pl.ds(k0, tk), :], qaug_ref[...])

    def update(s_ref, jt, diagonal):
        k0 = pl.multiple_of(jnp.minimum(jt, last_tile) * tk, tk)
        s = s_ref[...]
        if diagonal:
            s = jnp.where(dmat <= t0 - jt * tk, s, NEG)
        m_old = m_ref[...]
        m_new = jnp.maximum(m_old, jnp.max(s, axis=0, keepdims=True))
        alpha = jnp.exp2(m_old - m_new)
        p = jnp.exp2(s - m_new)
        m_ref[...] = m_new
        pv = _tn_dot(vaug_ref[pl.ds(k0, tk), :], p.astype(BF16))
        acc_ref[...] = alpha * acc_ref[...] + pv[:acc_ref.shape[0]]

    blocks_per_tile = tk // SEL_BLOCK
    diag_tile = t0 // tk
    frow = pl.program_id(0) * pl.num_programs(1) + i

    def collect(jt, cnt):
        used = fl_ref[frow, jt * blocks_per_tile]
        for c in range(1, blocks_per_tile):
            used = used | fl_ref[frow, jt * blocks_per_tile + c]
        tl_ref[cnt] = jt
        return cnt + used

    n_before = lax.fori_loop(0, diag_tile, collect, 0)
    tl_ref[n_before] = diag_tile
    tl_ref[n_before + 1] = 2 * (last_tile + 1)

    def body(jj, carry):
        sb_ref[...] = scores(tl_ref[2 * jj + 1])
        update(sa_ref, tl_ref[2 * jj], False)
        sa_ref[...] = scores(tl_ref[2 * jj + 2])
        update(sb_ref, tl_ref[2 * jj + 1], False)
        return carry

    n_pairs = n_before // 2
    sa_ref[...] = scores(tl_ref[0])
    lax.fori_loop(0, n_pairs, body, 0)
    sb_ref[...] = scores(tl_ref[2 * n_pairs + 1])
    update(sa_ref, tl_ref[2 * n_pairs], True)
    update(sb_ref, tl_ref[2 * n_pairs + 1], True)

    gt = g_ref[0].T
    for hp in range(H_NSA // 2):
        rows = []
        for hh in range(2):
            h = 2 * hp + hh
            cs = slice(h * QB, (h + 1) * QB)
            rows.append(acc_ref[0:HEAD_DIM, cs] * (gt[3 * h + 1:3 * h + 2] / acc_ref[HEAD_DIM:HEAD_DIM + 1, cs]))
        cs = slice(hp * LANES, (hp + 1) * LANES)
        oa_ref[0, :, cs] = part_ref[0, :, cs] + jnp.concatenate(rows, axis=0).T


def _nsa_selected(flags, qa, nsel, gates, part, ksks, vsvs, kconst, sconst, tk=256):
    b, s, _ = qa.shape
    nq = H_NSA * QB
    grid_spec = pltpu.PrefetchScalarGridSpec(
        num_scalar_prefetch=1,
        grid=(b, s // QB),
        in_specs=[pl.BlockSpec((1, QB, 2 * LANES), lambda bi, i, fl: (bi, i, 0)),
                  pl.BlockSpec((1, QB, SEL_LANES), lambda bi, i, fl: (bi, i, 0)),
                  pl.BlockSpec((1, QB, LANES), lambda bi, i, fl: (bi, i, 0)),
                  pl.BlockSpec((1, QB, 2 * LANES), lambda bi, i, fl: (bi, i, 0)),
                  pl.BlockSpec((1, s, LANES), lambda bi, i, fl: (bi, 0, 0)),
                  pl.BlockSpec((1, s, LANES), lambda bi, i, fl: (bi, 0, 0)),
                  pl.BlockSpec((s, 2 * LANES), lambda bi, i, fl: (0, 0)),
                  pl.BlockSpec((8, LANES), lambda bi, i, fl: (0, 0))],
        out_specs=pl.BlockSpec((1, QB, 2 * LANES), lambda bi, i, fl: (bi, i, 0)),
        scratch_shapes=[pltpu.VMEM((s, 2 * LANES), BF16),
                        pltpu.VMEM((s, LANES), BF16),
                        pltpu.VMEM((nq, 2 * LANES), BF16),
                        pltpu.VMEM((1, nq), F32),
                        pltpu.VMEM((HEAD_DIM + 8, nq), F32),
                        pltpu.VMEM((tk, nq), F32),
                        pltpu.VMEM((tk, nq), F32),
                        pltpu.SMEM((s // tk + 8,), jnp.int32)])
    return pl.pallas_call(
        functools.partial(_sel_body, tk=tk),
        grid_spec=grid_spec,
        out_shape=jax.ShapeDtypeStruct((b, s, 2 * LANES), F32),
        compiler_params=_cparams("arbitrary", "arbitrary"),
        name="nsa_selected",
    )(flags, qa, nsel, gates, part, ksks, vsvs, kconst, sconst)


def _dil_body(q_ref, kp_ref, kc_ref, vp_ref, vc_ref, bias_ref, o_ref, lse_ref, *, r, m):
    i = pl.program_id(1)
    key_row = lax.broadcasted_iota(jnp.int32, (2 * QB, QB), 0)
    first = key_row >= jnp.where(i > 0, 0, QB)

    def rows(c, u):
        return pl.ds(u * QB * r + c, QB, stride=r) if r > 1 else pl.ds(u * QB, QB)

    for c in range(r):
        for u in range(m):
            cur = rows(c, u)
            q = q_ref[0, cur, :].astype(BF16)
            if u > 0:
                k_prev, v_prev = kc_ref[0, rows(c, u - 1), :], vc_ref[0, rows(c, u - 1), :]
            else:
                k_prev, v_prev = kp_ref[0, rows(c, m - 1), :], vp_ref[0, rows(c, m - 1), :]
            kk = jnp.concatenate([k_prev, kc_ref[0, cur, :]], axis=0).astype(BF16)
            vv = jnp.concatenate([v_prev, vc_ref[0, cur, :]], axis=0).astype(BF16)
            outs, lses = [], []
            for hh in range(2):
                s = _nt_dot(kk, _mask_half(q, hh)) + bias_ref[hh]
                if u == 0:
                    s = jnp.where(first, s, NEG)
                mx = jnp.max(s, axis=0, keepdims=True)
                p = jnp.exp2(s - mx)
                l = jnp.sum(p, axis=0, keepdims=True)
                o = _tn_dot(vv, p.astype(BF16)) * (1.0 / l)
                outs.append(o[hh * HEAD_DIM:(hh + 1) * HEAD_DIM])
                lses.append(jnp.broadcast_to(mx + jnp.log2(l), (HEAD_DIM, QB)))
            o_ref[0, cur, :] = jnp.concatenate(outs, axis=0).T
            lse_ref[0, cur, :] = jnp.concatenate(lses, axis=0).T


def _dilated(dq, dk, dv, bias, g, r, span=1024):
    b, s, _ = dq.shape
    m = max(span // (QB * r), 1)
    span = m * QB * r
    cur = lambda bi, i: (bi, i, g)
    prev = lambda bi, i: (bi, jnp.maximum(i - 1, 0), g)
    return pl.pallas_call(
        functools.partial(_dil_body, r=r, m=m),
        grid=(b, s // span),
        in_specs=[pl.BlockSpec((1, span, LANES), cur),
                  pl.BlockSpec((1, span, LANES), prev), pl.BlockSpec((1, span, LANES), cur),
                  pl.BlockSpec((1, span, LANES), prev), pl.BlockSpec((1, span, LANES), cur),
                  pl.BlockSpec((2, 2 * QB, QB), lambda bi, i: (0, 0, 0))],
        out_specs=[pl.BlockSpec((1, span, LANES), lambda bi, i: (bi, i, 0))] * 2,
        out_shape=[jax.ShapeDtypeStruct((b, s, LANES), F32)] * 2,
        compiler_params=_cparams("parallel", "parallel"),
        name=f"dilated_r{r}",
    )(dq, dk, dk, dv, dv, bias)


def _dil_bias(g):
    w, r = DIL_PAIRS[g]
    assert w // r == QB
    iq = np.arange(QB)[None, :]
    jk = np.arange(2 * QB)[:, None]
    dist = iq + QB - jk
    out = np.empty((2, 2 * QB, QB), np.float32)
    for hh in range(2):
        slope = SLOPES_DIL[2 * g + hh]
        out[hh] = np.where((dist >= 0) & (dist <= QB), -slope * LOG2E * r * dist, NEG)
    return jnp.asarray(out)


def _sb_body(q_ref, k_ref, v_ref, lt_ref, o_ref, acc_ref):
    i = pl.program_id(2)
    tq = q_ref.shape[1]
    q = q_ref[0]
    lt = lt_ref[...]
    lane = lax.broadcasted_iota(jnp.int32, (tq, 2 * tq), 1)
    dmat = lax.broadcasted_iota(jnp.int32, (tq, 2 * tq), 0) - lane % tq
    q2 = jnp.concatenate([_mask_half(q, 0), _mask_half(q, 1)], axis=0)
    acc_ref[...] = jnp.zeros(acc_ref.shape, F32)

    def cond(c):
        jt, _, cmax = c
        return (jt >= 0) & (cmax > SB_UNDERFLOW * LOG2E)

    def body(c):
        jt, carry, _ = c
        pv = None
        for u in range(2):
            ju = jt - u
            k0 = pl.multiple_of(jnp.maximum(ju, 0) * tq, tq)
            kt = k_ref[0, pl.ds(k0, tq), :]
            vt = v_ref[0, pl.ds(k0, tq), :]
            z = _nt_dot(kt, q2)
            lb = jnp.minimum(z, 0.0) - jnp.log2(1.0 + jnp.exp2(-jnp.abs(z)))
            lf = lb - z
            if u == 0:
                mask = dmat < jnp.where(jt < i, tq, 0)
                lf = jnp.where(mask, lf, 0.0)
            else:
                vt = jnp.where(ju >= 0, vt, jnp.zeros_like(vt))
            agg = _dot(lt, lf.astype(BF16))
            a = jnp.exp2(lb + agg[:tq] + carry)
            if u == 0:
                a = jnp.where(mask, a, 0.0)
            pvu = _tn_dot(vt, a.astype(BF16))
            pv = pvu if pv is None else pv + pvu
            carry = carry + agg[tq:tq + 1]
        acc_ref[0] += pv[:HEAD_DIM, :tq]
        acc_ref[1] += pv[HEAD_DIM:, tq:]
        return jt - 2, carry, jnp.max(carry)

    lax.while_loop(cond, body, (i, jnp.zeros((1, 2 * tq), F32), jnp.float32(0.0)))
    o_ref[0] = jnp.concatenate([acc_ref[0], acc_ref[1]], axis=0).T


def _stick_breaking(sq, sk, sv, lt):
    b, s, _ = sq.shape
    npair = H_SB // 2
    tq = lt.shape[1]
    return pl.pallas_call(
        _sb_body,
        grid=(b, npair, s // tq),
        in_specs=[pl.BlockSpec((1, tq, LANES), lambda bi, hp, i: (bi, i, hp)),
                  pl.BlockSpec((1, s, LANES), lambda bi, hp, i: (bi, 0, hp)),
                  pl.BlockSpec((1, s, LANES), lambda bi, hp, i: (bi, 0, hp)),
                  pl.BlockSpec(lt.shape, lambda bi, hp, i: (0, 0))],
        out_specs=pl.BlockSpec((1, tq, LANES), lambda bi, hp, i: (bi, i, hp)),
        out_shape=jax.ShapeDtypeStruct((b, s, npair * LANES), F32),
        scratch_shapes=[pltpu.VMEM((2, HEAD_DIM, tq), F32)],
        compiler_params=_cparams("parallel", "parallel", "parallel"),
        name="stick_breaking",
    )(sq, sk, sv, lt)


def _out_body(x_ref, oa_ref, d0, l0, d1, l1, d2, l2, oc_ref, w_ref, o_ref):
    lses = [l0[...], l1[...], l2[...]]
    m = jnp.maximum(jnp.maximum(lses[0], lses[1]), lses[2])
    es = [jnp.exp2(l - m) for l in lses]
    den = es[0] + es[1] + es[2]
    ob = (es[0] * d0[...] + es[1] * d1[...] + es[2] * d2[...]) / den
    na = H_NSA * HEAD_DIM
    nb = na + LANES
    acc = _dot(oa_ref[...].astype(BF16), w_ref[0:na, :])
    acc = acc + _dot(ob.astype(BF16), w_ref[na:nb, :])
    acc = acc + _dot(oc_ref[...].astype(BF16), w_ref[nb:, :])
    o_ref[...] = x_ref[...] + acc


def _out_proj(x2d, oa, dil, oc, w_out, tm=512):
    n = x2d.shape[0]
    row = lambda w: pl.BlockSpec((tm, w), lambda i: (i, 0))
    d_cat = w_out.shape[0]
    ins = [x2d, oa]
    specs = [row(D_MODEL), row(2 * LANES)]
    for o, lse in dil:
        ins += [o, lse]
        specs += [row(LANES), row(LANES)]
    ins += [oc, w_out]
    specs += [row(oc.shape[1]), pl.BlockSpec((d_cat, D_MODEL), lambda i: (0, 0))]
    return pl.pallas_call(
        _out_body,
        grid=(n // tm,),
        in_specs=specs,
        out_specs=row(D_MODEL),
        out_shape=jax.ShapeDtypeStruct((n, D_MODEL), F32),
        compiler_params=_cparams("parallel"),
        name="out_proj",
    )(*ins)


def _mlp_body(x_ref, nw_ref, wu_ref, wd_ref, o_ref, *, fc):
    x = x_ref[...]
    ms = jnp.mean(x * x, axis=-1, keepdims=True)
    h = (x * lax.rsqrt(ms + RMS_EPS) * nw_ref[...]).astype(BF16)
    acc = x
    for c in range(D_FF // fc):
        u = jnp.maximum(_dot(h, wu_ref[:, c * fc:(c + 1) * fc]), 0.0)
        acc = acc + _dot((u * u).astype(BF16), wd_ref[c * fc:(c + 1) * fc, :])
    o_ref[...] = acc


def _mlp(x2d, nw, wu, wd, tm=512, fc=1024):
    n = x2d.shape[0]
    const = dict(pipeline_mode=pl.Buffered(1))
    return pl.pallas_call(
        functools.partial(_mlp_body, fc=fc),
        grid=(n // tm,),
        in_specs=[pl.BlockSpec((tm, D_MODEL), lambda i: (i, 0)),
                  pl.BlockSpec((1, D_MODEL), lambda i: (0, 0)),
                  pl.BlockSpec((D_MODEL, D_FF), lambda i: (0, 0), **const),
                  pl.BlockSpec((D_FF, D_MODEL), lambda i: (0, 0), **const)],
        out_specs=pl.BlockSpec((tm, D_MODEL), lambda i: (i, 0)),
        out_shape=jax.ShapeDtypeStruct((n, D_MODEL), F32),
        compiler_params=_cparams("parallel"),
        name="mlp",
    )(x2d, nw, wu, wd)


def _relayout_in_weight(w, g_nsa, g_dil):
    hd = HEAD_DIM
    kv0 = H_NSA * hd
    g0 = kv0 + 6 * hd
    b0 = g0 + 3 * H_NSA
    c0 = b0 + 3 * H_DIL * hd
    w = w.T
    kv = lambda c: w[kv0 + c * hd:kv0 + (c + 1) * hd]
    gate = jnp.pad(w[g0:b0], ((0, LANES - 3 * H_NSA), (0, 0)))
    nd = H_DIL * hd
    w_re = jnp.concatenate([w[:kv0], kv(2), kv(2), kv(4), kv(4), w[b0:b0 + 2 * nd],
                            kv(0), kv(1), kv(3), kv(3), kv(5), kv(5), gate, w[b0 + 2 * nd:c0], w[c0:]],
                           axis=0).astype(BF16)
    one = lambda n: jnp.ones((n,), F32)
    cv = jnp.concatenate([
        jnp.tile(g_nsa[0], H_NSA) * (SCALE * LOG2E), jnp.tile(g_nsa[2], 2), jnp.tile(g_nsa[3], 2),
        jnp.tile(g_dil[0], H_DIL) * (SCALE * LOG2E), jnp.tile(g_dil[1], H_DIL),
        one(4 * LANES), one(nd), one(H_SB * hd) * (SCALE * LOG2E), one(2 * H_SB * hd)])
    return w_re, cv.reshape(1, _IN_COLS)


def _relayout_cmp(w1, pe, w2, gk):
    hd, half = HEAD_DIM, CMP_LEN // 2
    w1r = w1.reshape(2, 2, half, hd, CMP_HIDDEN)
    per = pe.reshape(2, 2, half, hd)
    w1i, pei = [], []
    for c in range(2):
        pad = ((0, 0), (0, 0), (0, hd), (0, 0)) if c == 0 else ((0, 0), (0, 0), (hd, 0), (0, 0))
        w1i.append(jnp.pad(w1r[c], pad).reshape(2, half * LANES, CMP_HIDDEN))
        pei.append(jnp.pad(per[c], pad[:3]).reshape(2, half * LANES))
    w1i = jnp.concatenate(w1i, axis=0).astype(BF16)
    pe8 = jnp.pad(jnp.concatenate(pei, axis=0), ((0, 4), (0, 0)))
    w2r = jnp.concatenate([w2, w2], axis=-1).astype(BF16)
    return w1i, pe8, w2r, jnp.tile(gk, 2).reshape(1, LANES)


def _constants(s):
    ncp = s // CMP_STRIDE
    n_cmp = (s - CMP_LEN) // CMP_STRIDE + 1
    n = np.arange(ncp)[:, None]
    j = np.arange(SEL_LANES)[None, :]
    ov = ((CMP_STRIDE * n <= SEL_BLOCK * j + SEL_BLOCK - 1) & (CMP_STRIDE * n + CMP_LEN - 1 >= SEL_BLOCK * j)
          & (n < n_cmp) & (j < s // SEL_BLOCK))
    pos = np.arange(s)
    kconst = np.zeros((s, 2 * LANES), np.float32)
    kconst[:, HEAD_DIM:HEAD_DIM + 3] = (pos // SEL_BLOCK * SEL_BLOCK)[:, None]
    kconst[:, HEAD_DIM + 3:HEAD_DIM + 6] = (pos % SEL_BLOCK)[:, None]
    kconst[:, LANES:] = np.where(pos[:, None] // SEL_BLOCK == j, NEG, 0.0)
    sconst = np.zeros((8, LANES), np.float32)
    for h in range(H_NSA):
        rest = np.float32(SLOPES_NSA[h] * LOG2E)
        for c in range(3):
            piece = rest.astype(BF16).astype(np.float32)
            sconst[h, HEAD_DIM + c] = sconst[h, HEAD_DIM + 3 + c] = piece
            rest = np.float32(rest - piece)
    a2 = np.arange(2 * QB)
    lt = np.concatenate([(a2[None, :] > a2[:, None]).astype(np.float32), np.ones((8, 2 * QB), np.float32)])
    a = np.arange(_SLAB)
    gm = (a[:, None] // HEAD_DIM == a[None, :] // HEAD_DIM).astype(np.float32) / HEAD_DIM
    return dict(ovt=jnp.asarray(ov.T, BF16), kconst=jnp.asarray(kconst, BF16), sconst=jnp.asarray(sconst),
                lt=jnp.asarray(lt, BF16), gm=jnp.asarray(gm, BF16),
                dil_bias=[_dil_bias(g) for g in range(len(DIL_PAIRS))])


def kernel(x, norm_mix, norm_mlp, w_in, qk_gain_nsa, qk_gain_dil, cmp_pe, cmp_w1, cmp_w2, w_out, w_up, w_down):
    b, s, d = x.shape
    assert d == D_MODEL and s % (DIL_PAIRS[-1][1] * QB) == 0 and s // SEL_BLOCK <= SEL_LANES
    assert s >= WIN_NSA + QB
    n = b * s
    cst = _constants(s)
    x2d = x.reshape(n, d)
    for l in range(w_in.shape[0]):
        w_re, cv = _relayout_in_weight(w_in[l], qk_gain_nsa[l], qk_gain_dil[l])
        qa, ksks, kwkw, dq, dk, kvc, vsvs, vwvw, gates, dv, sq, sk, sv = _in_proj(
            x2d, norm_mix[l].reshape(1, d), w_re, cv, cst["gm"])
        tok = lambda a: a.reshape(b, s, a.shape[-1])
        w1i, pe8, w2r, gk = _relayout_cmp(cmp_w1[l], cmp_pe[l], cmp_w2[l], qk_gain_nsa[l, 1])
        kcr, vcr = _compress(tok(kvc), w1i, pe8, w2r, gk)
        part, nsel, used = _nsa_cmp_win(tok(qa), kcr, vcr, tok(kwkw), tok(vwvw), tok(gates), cst["ovt"])
        oa = _nsa_selected(used[:, :, 0, :].reshape(n // QB, SEL_LANES), tok(qa), nsel, tok(gates), part,
                           tok(ksks), tok(vsvs), cst["kconst"], cst["sconst"])
        dil = []
        for g, (_, r) in enumerate(DIL_PAIRS):
            o, lse = _dilated(tok(dq), tok(dk), tok(dv), cst["dil_bias"][g], g, r)
            dil.append((o.reshape(n, LANES), lse.reshape(n, LANES)))
        oc = _stick_breaking(tok(sq), tok(sk), tok(sv), cst["lt"])
        x2d = _out_proj(x2d, oa.reshape(n, 2 * LANES), dil, oc.reshape(n, -1), w_out[l].astype(BF16))
        x2d = _mlp(x2d, norm_mlp[l].reshape(1, d), w_up[l].astype(BF16), w_down[l].astype(BF16))
    return x2d.reshape(b, s, d)
```

```python
import functools
import math

import numpy as np
import jax
import jax.numpy as jnp
from jax import lax
from jax.experimental import pallas as pl
from jax.experimental.pallas import tpu as pltpu

F32 = jnp.float32
BF16 = jnp.bfloat16

D_MODEL = 1024
HEAD_DIM = 64
H_NSA = 4
H_DIL = 6
H_SB = 6
DIL_PAIRS = ((128, 1), (512, 4), (2048, 16))
CMP_LEN = 32
CMP_STRIDE = 16
CMP_HIDDEN = 128
SEL_BLOCK = 64
SEL_TOPK = 16
WIN_NSA = 512
D_FF = 4 * D_MODEL
RMS_EPS = 1e-6
NEG = -1e30
FORCE_BONUS = 1e4
LOG2E = 1.4426950408889634
SCALE = HEAD_DIM ** -0.5
LANES = 128
QB = 128
SEL_LANES = 128
SB_UNDERFLOW = -104.0

_SLOPES = [2.0 ** (-8.0 * i / (H_NSA + H_DIL)) for i in range(1, H_NSA + H_DIL + 1)]
SLOPES_DIL = _SLOPES[:H_DIL]
SLOPES_NSA = _SLOPES[H_DIL:]

_IN_SEGS = (("qa", 2, "norm"), ("ksks", 1, "norm"), ("kwkw", 1, "norm"), ("dq", 3, "norm"), ("dk", 3, "norm"),
            ("kvc", 1, "raw"), ("vsvs", 1, "raw"), ("vwvw", 1, "raw"), ("gate", 1, "gate"),
            ("dv", 3, "raw"), ("sq", 3, "raw"), ("sk", 3, "raw"), ("sv", 3, "raw"))
_IN_COLS = sum(n for _, n, _ in _IN_SEGS) * LANES
_SLAB = 2 * LANES
_F32_SEGS = ("kvc", "dq", "dk", "dv")
_VMEM_LIMIT = 56 * 1024 * 1024


def _cparams(*sem, vmem=_VMEM_LIMIT):
    return pltpu.CompilerParams(dimension_semantics=sem, vmem_limit_bytes=vmem)


def _nt_dot(a, b):
    return lax.dot_general(a, b, (((1,), (1,)), ((), ())), preferred_element_type=F32)


def _dot(a, b):
    return jnp.dot(a, b, preferred_element_type=F32)


def _tn_dot(a, b):
    return lax.dot_general(a, b, (((0,), (0,)), ((), ())), preferred_element_type=F32)


def _split_dot(x, m):
    hi = x.astype(BF16)
    lo = (x - hi.astype(F32)).astype(BF16)
    return _dot(hi, m) + _dot(lo, m)


def _mask_half(x, hh):
    lane = lax.broadcasted_iota(jnp.int32, x.shape, x.ndim - 1)
    keep = (lane % LANES < HEAD_DIM) if hh == 0 else (lane % LANES >= HEAD_DIM)
    return jnp.where(keep, x, jnp.zeros_like(x))


def _in_proj_body(x_ref, nw_ref, w_ref, cv_ref, gm_ref, *out_refs):
    x = x_ref[...]
    ms = jnp.mean(x * x, axis=-1, keepdims=True)
    h = (x * lax.rsqrt(ms + RMS_EPS) * nw_ref[...]).astype(BF16)
    gm = gm_ref[...]
    tiles = [(o_ref, t, kind) for (name, ntile, kind), o_ref in zip(_IN_SEGS, out_refs) for t in range(ntile)]
    for sl in range(_IN_COLS // _SLAB):
        c0 = sl * _SLAB
        y = _nt_dot(h, w_ref[c0:c0 + _SLAB, :])
        if tiles[2 * sl][2] == "norm":
            msq = _dot((y * y).astype(BF16), gm)
            y = y * lax.rsqrt(msq + RMS_EPS)
        y = y * cv_ref[:, c0:c0 + _SLAB]
        for half in range(2):
            o_ref, t, kind = tiles[2 * sl + half]
            yh = y[:, half * LANES:(half + 1) * LANES]
            if kind == "gate":
                yh = jax.nn.sigmoid(yh)
            o_ref[:, t * LANES:(t + 1) * LANES] = yh.astype(o_ref.dtype)


def _in_proj(x2d, nw, w_re, cv, gm, tm=512):
    n = x2d.shape[0]
    out_shape, out_specs = [], []
    for name, ntile, kind in _IN_SEGS:
        dt = F32 if kind == "gate" or name in _F32_SEGS else BF16
        out_shape.append(jax.ShapeDtypeStruct((n, ntile * LANES), dt))
        out_specs.append(pl.BlockSpec((tm, ntile * LANES), lambda i: (i, 0)))
    return pl.pallas_call(
        _in_proj_body,
        grid=(n // tm,),
        in_specs=[pl.BlockSpec((tm, D_MODEL), lambda i: (i, 0)),
                  pl.BlockSpec((1, D_MODEL), lambda i: (0, 0)),
                  pl.BlockSpec((_IN_COLS, D_MODEL), lambda i: (0, 0)),
                  pl.BlockSpec((1, _IN_COLS), lambda i: (0, 0)),
                  pl.BlockSpec((_SLAB, _SLAB), lambda i: (0, 0))],
        out_specs=out_specs,
        out_shape=out_shape,
        compiler_params=_cparams("parallel"),
        name="in_proj",
    )(x2d, nw, w_re, cv, gm)


def _gelu_tanh(x):
    return 0.5 * x * (1.0 + jnp.tanh(0.7978845608028654 * (x + 0.044715 * (x * x * x))))


def _cmp_body(x_ref, w1_ref, pe_ref, w2_ref, gk_ref, kc_ref, vc_ref):
    ncp = kc_ref.shape[1]
    pe = pe_ref[...].astype(BF16)
    x = jnp.concatenate([x_ref[0, pl.ds(l, ncp, stride=CMP_STRIDE), :].astype(BF16) for l in range(CMP_STRIDE)],
                        axis=1)
    outs = []
    for c in range(2):
        top = _dot(x, w1_ref[2 * c])
        bot = _dot(x, w1_ref[2 * c + 1])
        bias = _dot(pe, w1_ref[2 * c])[2 * c:2 * c + 1] + _dot(pe, w1_ref[2 * c + 1])[2 * c + 1:2 * c + 2]
        hid = top + pltpu.roll(bot, ncp - 1, 0) + bias
        outs.append(_dot(_gelu_tanh(hid).astype(BF16), w2_ref[c]))
    kc = outs[0]
    kc = kc * lax.rsqrt(jnp.mean(kc * kc, axis=-1, keepdims=True) + RMS_EPS) * gk_ref[...]
    kc_ref[0] = kc.astype(BF16)
    vc_ref[0] = outs[1].astype(BF16)


def _compress(kvc, w1i, pe8, w2r, gk):
    b, s, _ = kvc.shape
    ncp, wid = s // CMP_STRIDE, CMP_STRIDE * LANES
    return pl.pallas_call(
        _cmp_body,
        grid=(b,),
        in_specs=[pl.BlockSpec((1, s, LANES), lambda i: (i, 0, 0)),
                  pl.BlockSpec((4, wid, CMP_HIDDEN), lambda i: (0, 0, 0)),
                  pl.BlockSpec((8, wid), lambda i: (0, 0)),
                  pl.BlockSpec((2, CMP_HIDDEN, LANES), lambda i: (0, 0, 0)),
                  pl.BlockSpec((1, LANES), lambda i: (0, 0))],
        out_specs=[pl.BlockSpec((1, ncp, LANES), lambda i: (i, 0, 0)),
                   pl.BlockSpec((1, ncp, LANES), lambda i: (i, 0, 0))],
        out_shape=[jax.ShapeDtypeStruct((b, ncp, LANES), BF16)] * 2,
        compiler_params=_cparams("parallel"),
        name="nsa_compress",
    )(kvc, w1i, pe8, w2r, gk)


def _softmax_cols(s, mask):
    m = jnp.max(s, axis=0, keepdims=True)
    p = jnp.where(mask, jnp.exp2(s - m), 0.0)
    l = jnp.maximum(jnp.sum(p, axis=0, keepdims=True), 1e-30)
    return p, l


def _cw_body(q_ref, kc_ref, vc_ref, kw_ref, vw_ref, g_ref, ovt_ref, oa_ref, ns_ref, fl_ref):
    i = pl.program_id(1)
    t0 = i * QB
    q = q_ref[0]
    gt = g_ref[0].T
    kc = kc_ref[0]
    vc = vc_ref[0]
    ncp = kc.shape[0]
    wk = WIN_NSA + QB

    nq = H_NSA * QB
    heads = range(H_NSA)
    q4 = jnp.concatenate([_mask_half(q[:, (h // 2) * LANES:(h // 2 + 1) * LANES], h % 2) for h in heads], axis=0)
    head = lax.broadcasted_iota(jnp.int32, (1, nq), 1) // QB
    slope = jnp.zeros((1, nq), F32)
    for h in heads:
        slope = jnp.where(head == h, SLOPES_NSA[h] * LOG2E, slope)
    gate = lambda br: jnp.concatenate([gt[3 * h + br:3 * h + br + 1] for h in heads], axis=1)

    per_head = lambda a: jnp.concatenate([a] * H_NSA, axis=1)

    n_row = lax.broadcasted_iota(jnp.int32, (ncp, QB), 0)
    q_lane = lax.broadcasted_iota(jnp.int32, (ncp, QB), 1)
    vis = per_head((t0 - (CMP_LEN - 1)) + q_lane - CMP_STRIDE * n_row) >= 0
    cend = per_head((CMP_STRIDE * n_row + (CMP_LEN - 1)).astype(F32))

    start = pl.multiple_of(jnp.maximum(t0 - WIN_NSA, 0), QB)
    kw = kw_ref[0, pl.ds(start, wk), :]
    vw = vw_ref[0, pl.ds(start, wk), :]
    j_row = lax.broadcasted_iota(jnp.int32, (wk, QB), 0)
    r_lane = lax.broadcasted_iota(jnp.int32, (wk, QB), 1)
    dw = per_head((t0 - start) + r_lane - j_row)
    wmask = (dw >= 0) & (dw < WIN_NSA)
    kposw = per_head((start + j_row).astype(F32))

    vlane = lax.broadcasted_iota(jnp.int32, (1, LANES), 1) < HEAD_DIM
    vc = jnp.where(vlane, vc, jnp.ones((), BF16))
    vw = jnp.where(vlane, vw, jnp.ones((), BF16))
    has_key = vis[0:1, :].astype(F32)

    s = jnp.where(vis, _nt_dot(kc, q4) + slope * cend, NEG)
    p = jnp.exp2(s - jnp.max(s, axis=0, keepdims=True))
    pv = _tn_dot(vc, p.astype(BF16))
    rl = has_key / pv[HEAD_DIM:HEAD_DIM + 1]
    pn = p * rl
    psum = pn[:, 0:QB]
    for h in range(1, H_NSA):
        psum = psum + pn[:, h * QB:(h + 1) * QB]
    o = pv[:HEAD_DIM] * (rl * gate(0))
    s = jnp.where(wmask, _nt_dot(kw, q4) + slope * kposw, NEG)
    p = jnp.exp2(s - jnp.max(s, axis=0, keepdims=True))
    pv = _tn_dot(vw, p.astype(BF16))
    o = o + pv[:HEAD_DIM] * (gate(2) / pv[HEAD_DIM:HEAD_DIM + 1])
    for hp in range(H_NSA // 2):
        pair = jnp.concatenate([o[:, (2 * hp) * QB:(2 * hp + 1) * QB], o[:, (2 * hp + 1) * QB:(2 * hp + 2) * QB]], axis=0)
        oa_ref[0, :, hp * LANES:(hp + 1) * LANES] = pair.T

    hi = psum.astype(BF16)
    lo = (psum - hi.astype(F32)).astype(BF16)
    imp = _dot(ovt_ref[...], hi) + _dot(ovt_ref[...], lo)
    j = lax.broadcasted_iota(jnp.int32, (SEL_LANES, QB), 0)
    qi = lax.broadcasted_iota(jnp.int32, (SEL_LANES, QB), 1)
    cur = jnp.right_shift(t0 + qi, int(math.log2(SEL_BLOCK)))
    forced = (j == 0) | (j == cur) | (j == cur - 1)
    imp = jnp.where(forced, -3e38, jnp.where(j <= cur, imp, NEG))
    jf = j.astype(F32)
    notsel = jnp.where(forced, 0.0, 1.0)
    for _ in range(SEL_TOPK - 3):
        mx = jnp.max(imp, axis=0, keepdims=True)
        idx = jnp.min(jnp.where(imp == mx, jf, float(SEL_LANES)), axis=0, keepdims=True)
        hit = jf == idx
        notsel = jnp.where(hit, 0.0, notsel)
        imp = jnp.where(hit, -3e38, imp)
    nst = notsel.T
    ns_ref[0] = nst.astype(BF16)
    used = 1.0 - jnp.min(nst, axis=0, keepdims=True)
    fl_ref[0, 0] = jnp.broadcast_to(used, (8, SEL_LANES)).astype(jnp.int32)


def _nsa_cmp_win(qa, kcr, vcr, kwkw, vwvw, gates, ovt):
    b, s, _ = qa.shape
    ncp = kcr.shape[1]
    return pl.pallas_call(
        _cw_body,
        grid=(b, s // QB),
        in_specs=[pl.BlockSpec((1, QB, 2 * LANES), lambda bi, i: (bi, i, 0)),
                  pl.BlockSpec((1, ncp, LANES), lambda bi, i: (bi, 0, 0)),
                  pl.BlockSpec((1, ncp, LANES), lambda bi, i: (bi, 0, 0)),
                  pl.BlockSpec((1, s, LANES), lambda bi, i: (bi, 0, 0)),
                  pl.BlockSpec((1, s, LANES), lambda bi, i: (bi, 0, 0)),
                  pl.BlockSpec((1, QB, LANES), lambda bi, i: (bi, i, 0)),
                  pl.BlockSpec((SEL_LANES, ncp), lambda bi, i: (0, 0))],
        out_specs=[pl.BlockSpec((1, QB, 2 * LANES), lambda bi, i: (bi, i, 0)),
                   pl.BlockSpec((1, QB, SEL_LANES), lambda bi, i: (bi, i, 0)),
                   pl.BlockSpec((1, 1, 8, SEL_LANES), lambda bi, i: (bi, i, 0, 0))],
        out_shape=[jax.ShapeDtypeStruct((b, s, 2 * LANES), F32),
                   jax.ShapeDtypeStruct((b, s, SEL_LANES), BF16),
                   jax.ShapeDtypeStruct((b, s // QB, 8, SEL_LANES), jnp.int32)],
        compiler_params=_cparams("parallel", "parallel"),
        name="nsa_cmp_win",
    )(qa, kcr, vcr, kwkw, vwvw, gates, ovt)


def _sel_body(fl_ref, q_ref, ns_ref, g_ref, part_ref, ks_ref, vs_ref, kc_ref, sc_ref, oa_ref,
              kaug_ref, vaug_ref, qaug_ref, m_ref, acc_ref, sa_ref, sb_ref, tl_ref, *, tk):
    i = pl.program_id(1)
    t0 = i * QB
    nq = H_NSA * QB
    last_tile = kaug_ref.shape[0] // tk - 1

    @pl.when(i == 0)
    def _():
        lane = lax.broadcasted_iota(jnp.int32, (kaug_ref.shape[0], LANES), 1)
        kaug_ref[:, 0:LANES] = jnp.where(lane < HEAD_DIM, ks_ref[0], kc_ref[:, 0:LANES])
        kaug_ref[:, LANES:2 * LANES] = kc_ref[:, LANES:2 * LANES]
        vaug_ref[...] = jnp.where(lane < HEAD_DIM, vs_ref[0], jnp.ones((), BF16))

    q = q_ref[0]
    ns = ns_ref[0]
    lane = lax.broadcasted_iota(jnp.int32, (QB, LANES), 1)
    for h in range(H_NSA):
        slab = q[:, (h // 2) * LANES:(h // 2 + 1) * LANES].astype(F32)
        if h % 2:
            slab = pltpu.roll(slab, HEAD_DIM, 1)
        qaug_ref[h * QB:(h + 1) * QB, 0:LANES] = jnp.where(lane < HEAD_DIM, slab, sc_ref[h:h + 1, :]).astype(BF16)
        qaug_ref[h * QB:(h + 1) * QB, LANES:2 * LANES] = ns
    m_ref[...] = jnp.full(m_ref.shape, NEG, F32)
    acc_ref[...] = jnp.zeros(acc_ref.shape, F32)

    key_row = lax.broadcasted_iota(jnp.int32, (tk, nq), 0)
    q_lane = lax.broadcasted_iota(jnp.int32, (tk, nq), 1)
    dmat = key_row - q_lane % QB

    def scores(jt):
        k0 = pl.multiple_of(jnp.minimum(jt, last_tile) * tk, tk)
        return _nt_dot(kaug_ref[pl.ds(k0, tk), :], qaug_ref[...])

    def update(s_ref, jt, diagonal):
        k0 = pl.multiple_of(jnp.minimum(jt, last_tile) * tk, tk)
        s = s_ref[...]
        if diagonal:
            s = jnp.where(dmat <= t0 - jt * tk, s, NEG)
        m_old = m_ref[...]
        m_new = jnp.maximum(m_old, jnp.max(s, axis=0, keepdims=True))
        alpha = jnp.exp2(m_old - m_new)
        p = jnp.exp2(s - m_new)
        m_ref[...] = m_new
        pv = _tn_dot(vaug_ref[pl.ds(k0, tk), :], p.astype(BF16))
        acc_ref[...] = alpha * acc_ref[...] + pv[:acc_ref.shape[0]]

    blocks_per_tile = tk // SEL_BLOCK
    diag_tile = t0 // tk
    frow = pl.program_id(0) * pl.num_programs(1) + i

    def collect(jt, cnt):
        used = fl_ref[frow, jt * blocks_per_tile]
        for c in range(1, blocks_per_tile):
            used = used | fl_ref[frow, jt * blocks_per_tile + c]
        tl_ref[cnt] = jt
        return cnt + used

    n_before = lax.fori_loop(0, diag_tile, collect, 0)
    tl_ref[n_before] = diag_tile
    tl_ref[n_before + 1] = 2 * (last_tile + 1)

    def body(jj, carry):
        sb_ref[...] = scores(tl_ref[2 * jj + 1])
        update(sa_ref, tl_ref[2 * jj], False)
        sa_ref[...] = scores(tl_ref[2 * jj + 2])
        update(sb_ref, tl_ref[2 * jj + 1], False)
        return carry

    n_pairs = n_before // 2
    sa_ref[...] = scores(tl_ref[0])
    lax.fori_loop(0, n_pairs, body, 0)
    sb_ref[...] = scores(tl_ref[2 * n_pairs + 1])
    update(sa_ref, tl_ref[2 * n_pairs], True)
    update(sb_ref, tl_ref[2 * n_pairs + 1], True)

    gt = g_ref[0].T
    for hp in range(H_NSA // 2):
        rows = []
        for hh in range(2):
            h = 2 * hp + hh
            cs = slice(h * QB, (h + 1) * QB)
            rows.append(acc_ref[0:HEAD_DIM, cs] * (gt[3 * h + 1:3 * h + 2] / acc_ref[HEAD_DIM:HEAD_DIM + 1, cs]))
        cs = slice(hp * LANES, (hp + 1) * LANES)
        oa_ref[0, :, cs] = part_ref[0, :, cs] + jnp.concatenate(rows, axis=0).T


def _nsa_selected(flags, qa, nsel, gates, part, ksks, vsvs, kconst, sconst, tk=256):
    b, s, _ = qa.shape
    nq = H_NSA * QB
    grid_spec = pltpu.PrefetchScalarGridSpec(
        num_scalar_prefetch=1,
        grid=(b, s // QB),
        in_specs=[pl.BlockSpec((1, QB, 2 * LANES), lambda bi, i, fl: (bi, i, 0)),
                  pl.BlockSpec((1, QB, SEL_LANES), lambda bi, i, fl: (bi, i, 0)),
                  pl.BlockSpec((1, QB, LANES), lambda bi, i, fl: (bi, i, 0)),
                  pl.BlockSpec((1, QB, 2 * LANES), lambda bi, i, fl: (bi, i, 0)),
                  pl.BlockSpec((1, s, LANES), lambda bi, i, fl: (bi, 0, 0)),
                  pl.BlockSpec((1, s, LANES), lambda bi, i, fl: (bi, 0, 0)),
                  pl.BlockSpec((s, 2 * LANES), lambda bi, i, fl: (0, 0)),
                  pl.BlockSpec((8, LANES), lambda bi, i, fl: (0, 0))],
        out_specs=pl.BlockSpec((1, QB, 2 * LANES), lambda bi, i, fl: (bi, i, 0)),
        scratch_shapes=[pltpu.VMEM((s, 2 * LANES), BF16),
                        pltpu.VMEM((s, LANES), BF16),
                        pltpu.VMEM((nq, 2 * LANES), BF16),
                        pltpu.VMEM((1, nq), F32),
                        pltpu.VMEM((HEAD_DIM + 8, nq), F32),
                        pltpu.VMEM((tk, nq), F32),
                        pltpu.VMEM((tk, nq), F32),
                        pltpu.SMEM((s // tk + 8,), jnp.int32)])
    return pl.pallas_call(
        functools.partial(_sel_body, tk=tk),
        grid_spec=grid_spec,
        out_shape=jax.ShapeDtypeStruct((b, s, 2 * LANES), F32),
        compiler_params=_cparams("arbitrary", "arbitrary"),
        name="nsa_selected",
    )(flags, qa, nsel, gates, part, ksks, vsvs, kconst, sconst)


def _dil_body(q_ref, kp_ref, kc_ref, vp_ref, vc_ref, bias_ref, o_ref, lse_ref, *, r, m):
    i = pl.program_id(1)
    key_row = lax.broadcasted_iota(jnp.int32, (2 * QB, QB), 0)
    first = key_row >= jnp.where(i > 0, 0, QB)

    def rows(c, u):
        return pl.ds(u * QB * r + c, QB, stride=r) if r > 1 else pl.ds(u * QB, QB)

    for c in range(r):
        for u in range(m):
            cur = rows(c, u)
            q = q_ref[0, cur, :].astype(BF16)
            if u > 0:
                k_prev, v_prev = kc_ref[0, rows(c, u - 1), :], vc_ref[0, rows(c, u - 1), :]
            else:
                k_prev, v_prev = kp_ref[0, rows(c, m - 1), :], vp_ref[0, rows(c, m - 1), :]
            kk = jnp.concatenate([k_prev, kc_ref[0, cur, :]], axis=0).astype(BF16)
            vv = jnp.concatenate([v_prev, vc_ref[0, cur, :]], axis=0).astype(BF16)
            outs, lses = [], []
            for hh in range(2):
                s = _nt_dot(kk, _mask_half(q, hh)) + bias_ref[hh]
                if u == 0:
                    s = jnp.where(first, s, NEG)
                mx = jnp.max(s, axis=0, keepdims=True)
                p = jnp.exp2(s - mx)
                l = jnp.sum(p, axis=0, keepdims=True)
                o = _tn_dot(vv, p.astype(BF16)) * (1.0 / l)
                outs.append(o[hh * HEAD_DIM:(hh + 1) * HEAD_DIM])
                lses.append(jnp.broadcast_to(mx + jnp.log2(l), (HEAD_DIM, QB)))
            o_ref[0, cur, :] = jnp.concatenate(outs, axis=0).T
            lse_ref[0, cur, :] = jnp.concatenate(lses, axis=0).T


def _dilated(dq, dk, dv, bias, g, r, span=1024):
    b, s, _ = dq.shape
    m = max(span // (QB * r), 1)
    span = m * QB * r
    cur = lambda bi, i: (bi, i, g)
    prev = lambda bi, i: (bi, jnp.maximum(i - 1, 0), g)
    return pl.pallas_call(
        functools.partial(_dil_body, r=r, m=m),
        grid=(b, s // span),
        in_specs=[pl.BlockSpec((1, span, LANES), cur),
                  pl.BlockSpec((1, span, LANES), prev), pl.BlockSpec((1, span, LANES), cur),
                  pl.BlockSpec((1, span, LANES), prev), pl.BlockSpec((1, span, LANES), cur),
                  pl.BlockSpec((2, 2 * QB, QB), lambda bi, i: (0, 0, 0))],
        out_specs=[pl.BlockSpec((1, span, LANES), lambda bi, i: (bi, i, 0))] * 2,
        out_shape=[jax.ShapeDtypeStruct((b, s, LANES), F32)] * 2,
        compiler_params=_cparams("parallel", "parallel"),
        name=f"dilated_r{r}",
    )(dq, dk, dk, dv, dv, bias)


def _dil_bias(g):
    w, r = DIL_PAIRS[g]
    assert w // r == QB
    iq = np.arange(QB)[None, :]
    jk = np.arange(2 * QB)[:, None]
    dist = iq + QB - jk
    out = np.empty((2, 2 * QB, QB), np.float32)
    for hh in range(2):
        slope = SLOPES_DIL[2 * g + hh]
        out[hh] = np.where((dist >= 0) & (dist <= QB), -slope * LOG2E * r * dist, NEG)
    return jnp.asarray(out)


def _sb_body(q_ref, k_ref, v_ref, lt_ref, o_ref, acc_ref):
    i = pl.program_id(2)
    tq = q_ref.shape[1]
    q = q_ref[0]
    lt = lt_ref[...]
    lane = lax.broadcasted_iota(jnp.int32, (tq, 2 * tq), 1)
    dmat = lax.broadcasted_iota(jnp.int32, (tq, 2 * tq), 0) - lane % tq
    q2 = jnp.concatenate([_mask_half(q, 0), _mask_half(q, 1)], axis=0)
    acc_ref[...] = jnp.zeros(acc_ref.shape, F32)

    def cond(c):
        jt, _, cmax = c
        return (jt >= 0) & (cmax > SB_UNDERFLOW * LOG2E)

    def body(c):
        jt, carry, _ = c
        pv = None
        for u in range(2):
            ju = jt - u
            k0 = pl.multiple_of(jnp.maximum(ju, 0) * tq, tq)
            kt = k_ref[0, pl.ds(k0, tq), :]
            vt = v_ref[0, pl.ds(k0, tq), :]
            z = _nt_dot(kt, q2)
            lb = jnp.minimum(z, 0.0) - jnp.log2(1.0 + jnp.exp2(-jnp.abs(z)))
            lf = lb - z
            if u == 0:
                mask = dmat < jnp.where(jt < i, tq, 0)
                lf = jnp.where(mask, lf, 0.0)
            else:
                vt = jnp.where(ju >= 0, vt, jnp.zeros_like(vt))
            agg = _dot(lt, lf.astype(BF16))
            a = jnp.exp2(lb + agg[:tq] + carry)
            if u == 0:
                a = jnp.where(mask, a, 0.0)
            pvu = _tn_dot(vt, a.astype(BF16))
            pv = pvu if pv is None else pv + pvu
            carry = carry + agg[tq:tq + 1]
        acc_ref[0] += pv[:HEAD_DIM, :tq]
        acc_ref[1] += pv[HEAD_DIM:, tq:]
        return jt - 2, carry, jnp.max(carry)

    lax.while_loop(cond, body, (i, jnp.zeros((1, 2 * tq), F32), jnp.float32(0.0)))
    o_ref[0] = jnp.concatenate([acc_ref[0], acc_ref[1]], axis=0).T


def _stick_breaking(sq, sk, sv, lt):
    b, s, _ = sq.shape
    npair = H_SB // 2
    tq = lt.shape[1]
    return pl.pallas_call(
        _sb_body,
        grid=(b, npair, s // tq),
        in_specs=[pl.BlockSpec((1, tq, LANES), lambda bi, hp, i: (bi, i, hp)),
                  pl.BlockSpec((1, s, LANES), lambda bi, hp, i: (bi, 0, hp)),
                  pl.BlockSpec((1, s, LANES), lambda bi, hp, i: (bi, 0, hp)),
                  pl.BlockSpec(lt.shape, lambda bi, hp, i: (0, 0))],
        out_specs=pl.BlockSpec((1, tq, LANES), lambda bi, hp, i: (bi, i, hp)),
        out_shape=jax.ShapeDtypeStruct((b, s, npair * LANES), F32),
        scratch_shapes=[pltpu.VMEM((2, HEAD_DIM, tq), F32)],
        compiler_params=_cparams("parallel", "parallel", "parallel"),
        name="stick_breaking",
    )(sq, sk, sv, lt)


def _out_body(x_ref, oa_ref, d0, l0, d1, l1, d2, l2, oc_ref, w_ref, o_ref):
    lses = [l0[...], l1[...], l2[...]]
    m = jnp.maximum(jnp.maximum(lses[0], lses[1]), lses[2])
    es = [jnp.exp2(l - m) for l in lses]
    den = es[0] + es[1] + es[2]
    ob = (es[0] * d0[...] + es[1] * d1[...] + es[2] * d2[...]) / den
    na = H_NSA * HEAD_DIM
    nb = na + LANES
    acc = _dot(oa_ref[...].astype(BF16), w_ref[0:na, :])
    acc = acc + _dot(ob.astype(BF16), w_ref[na:nb, :])
    acc = acc + _dot(oc_ref[...].astype(BF16), w_ref[nb:, :])
    o_ref[...] = x_ref[...] + acc


def _out_mlp_body(x_ref, oa_ref, d0, l0, d1, l1, d2, l2, oc_ref, wo_ref, nw_ref, wu_ref, wd_ref, o_ref, *, fc):
    _out_body(x_ref, oa_ref, d0, l0, d1, l1, d2, l2, oc_ref, wo_ref, o_ref)
    x = o_ref[...]
    ms = jnp.mean(x * x, axis=-1, keepdims=True)
    h = (x * lax.rsqrt(ms + RMS_EPS) * nw_ref[...]).astype(BF16)
    for c in range(D_FF // fc):
        u = jnp.maximum(_dot(h, wu_ref[:, c * fc:(c + 1) * fc]), 0.0)
        o_ref[...] += _dot((u * u).astype(BF16), wd_ref[c * fc:(c + 1) * fc, :])


def _out_mlp(x2d, oa, dil, oc, w_out, nw, wu, wd, tm=512, fc=1024):
    n = x2d.shape[0]
    row = lambda w: pl.BlockSpec((tm, w), lambda i: (i, 0))
    whole = lambda a: pl.BlockSpec(a.shape, lambda i: (0, 0))
    single = lambda a: pl.BlockSpec(a.shape, lambda i: (0, 0), pipeline_mode=pl.Buffered(1))
    ins = [x2d, oa]
    specs = [row(D_MODEL), row(2 * LANES)]
    for o, lse in dil:
        ins += [o, lse]
        specs += [row(LANES), row(LANES)]
    ins += [oc, w_out, nw, wu, wd]
    specs += [row(oc.shape[1]), whole(w_out), whole(nw), single(wu), single(wd)]
    return pl.pallas_call(
        functools.partial(_out_mlp_body, fc=fc),
        grid=(n // tm,),
        in_specs=specs,
        out_specs=row(D_MODEL),
        out_shape=jax.ShapeDtypeStruct((n, D_MODEL), F32),
        compiler_params=_cparams("parallel"),
        name="out_mlp",
    )(*ins)


def _relayout_in_weight(w, g_nsa, g_dil):
    hd = HEAD_DIM
    kv0 = H_NSA * hd
    g0 = kv0 + 6 * hd
    b0 = g0 + 3 * H_NSA
    c0 = b0 + 3 * H_DIL * hd
    w = w.T
    kv = lambda c: w[kv0 + c * hd:kv0 + (c + 1) * hd]
    gate = jnp.pad(w[g0:b0], ((0, LANES - 3 * H_NSA), (0, 0)))
    nd = H_DIL * hd
    w_re = jnp.concatenate([w[:kv0], kv(2), kv(2), kv(4), kv(4), w[b0:b0 + 2 * nd],
                            kv(0), kv(1), kv(3), kv(3), kv(5), kv(5), gate, w[b0 + 2 * nd:c0], w[c0:]],
                           axis=0).astype(BF16)
    one = lambda n: jnp.ones((n,), F32)
    cv = jnp.concatenate([
        jnp.tile(g_nsa[0], H_NSA) * (SCALE * LOG2E), jnp.tile(g_nsa[2], 2), jnp.tile(g_nsa[3], 2),
        jnp.tile(g_dil[0], H_DIL) * (SCALE * LOG2E), jnp.tile(g_dil[1], H_DIL),
        one(4 * LANES), one(nd), one(H_SB * hd) * (SCALE * LOG2E), one(2 * H_SB * hd)])
    return w_re, cv.reshape(1, _IN_COLS)


def _relayout_cmp(w1, pe, w2, gk):
    hd, half = HEAD_DIM, CMP_LEN // 2
    w1r = w1.reshape(2, 2, half, hd, CMP_HIDDEN)
    per = pe.reshape(2, 2, half, hd)
    w1i, pei = [], []
    for c in range(2):
        pad = ((0, 0), (0, 0), (0, hd), (0, 0)) if c == 0 else ((0, 0), (0, 0), (hd, 0), (0, 0))
        w1i.append(jnp.pad(w1r[c], pad).reshape(2, half * LANES, CMP_HIDDEN))
        pei.append(jnp.pad(per[c], pad[:3]).reshape(2, half * LANES))
    w1i = jnp.concatenate(w1i, axis=0).astype(BF16)
    pe8 = jnp.pad(jnp.concatenate(pei, axis=0), ((0, 4), (0, 0)))
    w2r = jnp.concatenate([w2, w2], axis=-1).astype(BF16)
    return w1i, pe8, w2r, jnp.tile(gk, 2).reshape(1, LANES)


def _constants(s):
    ncp = s // CMP_STRIDE
    n_cmp = (s - CMP_LEN) // CMP_STRIDE + 1
    n = np.arange(ncp)[:, None]
    j = np.arange(SEL_LANES)[None, :]
    ov = ((CMP_STRIDE * n <= SEL_BLOCK * j + SEL_BLOCK - 1) & (CMP_STRIDE * n + CMP_LEN - 1 >= SEL_BLOCK * j)
          & (n < n_cmp) & (j < s // SEL_BLOCK))
    pos = np.arange(s)
    kconst = np.zeros((s, 2 * LANES), np.float32)
    kconst[:, HEAD_DIM:HEAD_DIM + 3] = (pos // SEL_BLOCK * SEL_BLOCK)[:, None]
    kconst[:, HEAD_DIM + 3:HEAD_DIM + 6] = (pos % SEL_BLOCK)[:, None]
    kconst[:, LANES:] = np.where(pos[:, None] // SEL_BLOCK == j, NEG, 0.0)
    sconst = np.zeros((8, LANES), np.float32)
    for h in range(H_NSA):
        rest = np.float32(SLOPES_NSA[h] * LOG2E)
        for c in range(3):
            piece = rest.astype(BF16).astype(np.float32)
            sconst[h, HEAD_DIM + c] = sconst[h, HEAD_DIM + 3 + c] = piece
            rest = np.float32(rest - piece)
    a2 = np.arange(2 * QB)
    lt = np.concatenate([(a2[None, :] > a2[:, None]).astype(np.float32), np.ones((8, 2 * QB), np.float32)])
    a = np.arange(_SLAB)
    gm = (a[:, None] // HEAD_DIM == a[None, :] // HEAD_DIM).astype(np.float32) / HEAD_DIM
    return dict(ovt=jnp.asarray(ov.T, BF16), kconst=jnp.asarray(kconst, BF16), sconst=jnp.asarray(sconst),
                lt=jnp.asarray(lt, BF16), gm=jnp.asarray(gm, BF16),
                dil_bias=[_dil_bias(g) for g in range(len(DIL_PAIRS))])


def kernel(x, norm_mix, norm_mlp, w_in, qk_gain_nsa, qk_gain_dil, cmp_pe, cmp_w1, cmp_w2, w_out, w_up, w_down):
    b, s, d = x.shape
    assert d == D_MODEL and s % (DIL_PAIRS[-1][1] * QB) == 0 and s // SEL_BLOCK <= SEL_LANES
    assert s >= WIN_NSA + QB
    n = b * s
    cst = _constants(s)
    x2d = x.reshape(n, d)
    for l in range(w_in.shape[0]):
        w_re, cv = _relayout_in_weight(w_in[l], qk_gain_nsa[l], qk_gain_dil[l])
        qa, ksks, kwkw, dq, dk, kvc, vsvs, vwvw, gates, dv, sq, sk, sv = _in_proj(
            x2d, norm_mix[l].reshape(1, d), w_re, cv, cst["gm"])
        tok = lambda a: a.reshape(b, s, a.shape[-1])
        w1i, pe8, w2r, gk = _relayout_cmp(cmp_w1[l], cmp_pe[l], cmp_w2[l], qk_gain_nsa[l, 1])
        kcr, vcr = _compress(tok(kvc), w1i, pe8, w2r, gk)
        part, nsel, used = _nsa_cmp_win(tok(qa), kcr, vcr, tok(kwkw), tok(vwvw), tok(gates), cst["ovt"])
        oa = _nsa_selected(used[:, :, 0, :].reshape(n // QB, SEL_LANES), tok(qa), nsel, tok(gates), part,
                           tok(ksks), tok(vsvs), cst["kconst"], cst["sconst"])
        dil = []
        for g, (_, r) in enumerate(DIL_PAIRS):
            o, lse = _dilated(tok(dq), tok(dk), tok(dv), cst["dil_bias"][g], g, r)
            dil.append((o.reshape(n, LANES), lse.reshape(n, LANES)))
        oc = _stick_breaking(tok(sq), tok(sk), tok(sv), cst["lt"])
        x2d = _out_mlp(x2d, oa.reshape(n, 2 * LANES), dil, oc.reshape(n, -1), w_out[l].astype(BF16),
                       norm_mlp[l].reshape(1, d), w_up[l].astype(BF16), w_down[l].astype(BF16))
    return x2d.reshape(b, s, d)
```

```python
import functools
import math

import numpy as np
import jax
import jax.numpy as jnp
from jax import lax
from jax.experimental import pallas as pl
from jax.experimental.pallas import tpu as pltpu

F32 = jnp.float32
BF16 = jnp.bfloat16

D_MODEL = 1024
HEAD_DIM = 64
H_NSA = 4
H_DIL = 6
H_SB = 6
DIL_PAIRS = ((128, 1), (512, 4), (2048, 16))
CMP_LEN = 32
CMP_STRIDE = 16
CMP_HIDDEN = 128
SEL_BLOCK = 64
SEL_TOPK = 16
WIN_NSA = 512
D_FF = 4 * D_MODEL
RMS_EPS = 1e-6
NEG = -1e30
FORCE_BONUS = 1e4
LOG2E = 1.4426950408889634
SCALE = HEAD_DIM ** -0.5
LANES = 128
QB = 128
SEL_LANES = 128
SB_UNDERFLOW = -104.0

_SLOPES = [2.0 ** (-8.0 * i / (H_NSA + H_DIL)) for i in range(1, H_NSA + H_DIL + 1)]
SLOPES_DIL = _SLOPES[:H_DIL]
SLOPES_NSA = _SLOPES[H_DIL:]

_IN_SEGS = (("qa", 2, "norm"), ("ksks", 1, "norm"), ("kwkw", 1, "norm"), ("dq", 3, "norm"), ("dk", 3, "norm"),
            ("kvc", 1, "raw"), ("vsvs", 1, "raw"), ("vwvw", 1, "raw"), ("gate", 1, "gate"),
            ("dv", 3, "raw"), ("sq", 3, "raw"), ("sk", 3, "raw"), ("sv", 3, "raw"))
_IN_COLS = sum(n for _, n, _ in _IN_SEGS) * LANES
_SLAB = 2 * LANES
_F32_SEGS = ("kvc", "dq", "dk", "dv")
_VMEM_LIMIT = 56 * 1024 * 1024


def _cparams(*sem, vmem=_VMEM_LIMIT):
    return pltpu.CompilerParams(dimension_semantics=sem, vmem_limit_bytes=vmem)


def _nt_dot(a, b):
    return lax.dot_general(a, b, (((1,), (1,)), ((), ())), preferred_element_type=F32)


def _dot(a, b):
    return jnp.dot(a, b, preferred_element_type=F32)


def _tn_dot(a, b):
    return lax.dot_general(a, b, (((0,), (0,)), ((), ())), preferred_element_type=F32)


def _split_dot(x, m):
    hi = x.astype(BF16)
    lo = (x - hi.astype(F32)).astype(BF16)
    return _dot(hi, m) + _dot(lo, m)


def _mask_half(x, hh):
    lane = lax.broadcasted_iota(jnp.int32, x.shape, x.ndim - 1)
    keep = (lane % LANES < HEAD_DIM) if hh == 0 else (lane % LANES >= HEAD_DIM)
    return jnp.where(keep, x, jnp.zeros_like(x))


def _in_proj_body(x_ref, nw_ref, w_ref, cv_ref, gm_ref, *out_refs):
    x = x_ref[...]
    ms = jnp.mean(x * x, axis=-1, keepdims=True)
    h = (x * lax.rsqrt(ms + RMS_EPS) * nw_ref[...]).astype(BF16)
    gm = gm_ref[...]
    tiles = [(o_ref, t, kind) for (name, ntile, kind), o_ref in zip(_IN_SEGS, out_refs) for t in range(ntile)]
    for sl in range(_IN_COLS // _SLAB):
        c0 = sl * _SLAB
        y = _nt_dot(h, w_ref[c0:c0 + _SLAB, :])
        if tiles[2 * sl][2] == "norm":
            msq = _dot((y * y).astype(BF16), gm)
            y = y * lax.rsqrt(msq + RMS_EPS)
        y = y * cv_ref[:, c0:c0 + _SLAB]
        for half in range(2):
            o_ref, t, kind = tiles[2 * sl + half]
            yh = y[:, half * LANES:(half + 1) * LANES]
            if kind == "gate":
                yh = jax.nn.sigmoid(yh)
            o_ref[:, t * LANES:(t + 1) * LANES] = yh.astype(o_ref.dtype)


def _in_proj(x2d, nw, w_re, cv, gm, tm=512):
    n = x2d.shape[0]
    out_shape, out_specs = [], []
    for name, ntile, kind in _IN_SEGS:
        dt = F32 if kind == "gate" or name in _F32_SEGS else BF16
        out_shape.append(jax.ShapeDtypeStruct((n, ntile * LANES), dt))
        out_specs.append(pl.BlockSpec((tm, ntile * LANES), lambda i: (i, 0)))
    return pl.pallas_call(
        _in_proj_body,
        grid=(n // tm,),
        in_specs=[pl.BlockSpec((tm, D_MODEL), lambda i: (i, 0)),
                  pl.BlockSpec((1, D_MODEL), lambda i: (0, 0)),
                  pl.BlockSpec((_IN_COLS, D_MODEL), lambda i: (0, 0)),
                  pl.BlockSpec((1, _IN_COLS), lambda i: (0, 0)),
                  pl.BlockSpec((_SLAB, _SLAB), lambda i: (0, 0))],
        out_specs=out_specs,
        out_shape=out_shape,
        compiler_params=_cparams("parallel"),
        name="in_proj",
    )(x2d, nw, w_re, cv, gm)


def _gelu_tanh(x):
    return 0.5 * x * (1.0 + jnp.tanh(0.7978845608028654 * (x + 0.044715 * (x * x * x))))


def _cmp_body(x_ref, w1_ref, pe_ref, w2_ref, gk_ref, kc_ref, vc_ref):
    ncp = kc_ref.shape[1]
    pe = pe_ref[...].astype(BF16)
    x = jnp.concatenate([x_ref[0, pl.ds(l, ncp, stride=CMP_STRIDE), :].astype(BF16) for l in range(CMP_STRIDE)],
                        axis=1)
    outs = []
    for c in range(2):
        top = _dot(x, w1_ref[2 * c])
        bot = _dot(x, w1_ref[2 * c + 1])
        bias = _dot(pe, w1_ref[2 * c])[2 * c:2 * c + 1] + _dot(pe, w1_ref[2 * c + 1])[2 * c + 1:2 * c + 2]
        hid = top + pltpu.roll(bot, ncp - 1, 0) + bias
        outs.append(_dot(_gelu_tanh(hid).astype(BF16), w2_ref[c]))
    kc = outs[0]
    kc = kc * lax.rsqrt(jnp.mean(kc * kc, axis=-1, keepdims=True) + RMS_EPS) * gk_ref[...]
    kc_ref[0] = kc.astype(BF16)
    vc_ref[0] = outs[1].astype(BF16)


def _compress(kvc, w1i, pe8, w2r, gk):
    b, s, _ = kvc.shape
    ncp, wid = s // CMP_STRIDE, CMP_STRIDE * LANES
    return pl.pallas_call(
        _cmp_body,
        grid=(b,),
        in_specs=[pl.BlockSpec((1, s, LANES), lambda i: (i, 0, 0)),
                  pl.BlockSpec((4, wid, CMP_HIDDEN), lambda i: (0, 0, 0)),
                  pl.BlockSpec((8, wid), lambda i: (0, 0)),
                  pl.BlockSpec((2, CMP_HIDDEN, LANES), lambda i: (0, 0, 0)),
                  pl.BlockSpec((1, LANES), lambda i: (0, 0))],
        out_specs=[pl.BlockSpec((1, ncp, LANES), lambda i: (i, 0, 0)),
                   pl.BlockSpec((1, ncp, LANES), lambda i: (i, 0, 0))],
        out_shape=[jax.ShapeDtypeStruct((b, ncp, LANES), BF16)] * 2,
        compiler_params=_cparams("parallel"),
        name="nsa_compress",
    )(kvc, w1i, pe8, w2r, gk)


def _softmax_cols(s, mask):
    m = jnp.max(s, axis=0, keepdims=True)
    p = jnp.where(mask, jnp.exp2(s - m), 0.0)
    l = jnp.maximum(jnp.sum(p, axis=0, keepdims=True), 1e-30)
    return p, l


def _cw_body(q_ref, kc_ref, vc_ref, kw_ref, vw_ref, g_ref, ovt_ref, oa_ref, ns_ref, fl_ref):
    i = pl.program_id(1)
    t0 = i * QB
    q = q_ref[0]
    gt = g_ref[0].T
    kc = kc_ref[0]
    vc = vc_ref[0]
    ncp = kc.shape[0]
    wk = WIN_NSA + QB

    nq = H_NSA * QB
    heads = range(H_NSA)
    q4 = jnp.concatenate([_mask_half(q[:, (h // 2) * LANES:(h // 2 + 1) * LANES], h % 2) for h in heads], axis=0)
    head = lax.broadcasted_iota(jnp.int32, (1, nq), 1) // QB
    slope = jnp.zeros((1, nq), F32)
    for h in heads:
        slope = jnp.where(head == h, SLOPES_NSA[h] * LOG2E, slope)
    gate = lambda br: jnp.concatenate([gt[3 * h + br:3 * h + br + 1] for h in heads], axis=1)

    per_head = lambda a: jnp.concatenate([a] * H_NSA, axis=1)

    n_row = lax.broadcasted_iota(jnp.int32, (ncp, QB), 0)
    q_lane = lax.broadcasted_iota(jnp.int32, (ncp, QB), 1)
    vis = per_head((t0 - (CMP_LEN - 1)) + q_lane - CMP_STRIDE * n_row) >= 0
    cend = per_head((CMP_STRIDE * n_row + (CMP_LEN - 1)).astype(F32))

    start = pl.multiple_of(jnp.maximum(t0 - WIN_NSA, 0), QB)
    kw = kw_ref[0, pl.ds(start, wk), :]
    vw = vw_ref[0, pl.ds(start, wk), :]
    j_row = lax.broadcasted_iota(jnp.int32, (wk, QB), 0)
    r_lane = lax.broadcasted_iota(jnp.int32, (wk, QB), 1)
    dw = per_head((t0 - start) + r_lane - j_row)
    wmask = (dw >= 0) & (dw < WIN_NSA)
    kposw = per_head((start + j_row).astype(F32))

    vlane = lax.broadcasted_iota(jnp.int32, (1, LANES), 1) < HEAD_DIM
    vc = jnp.where(vlane, vc, jnp.ones((), BF16))
    vw = jnp.where(vlane, vw, jnp.ones((), BF16))
    has_key = vis[0:1, :].astype(F32)

    s = jnp.where(vis, _nt_dot(kc, q4) + slope * cend, NEG)
    p = jnp.exp2(s - jnp.max(s, axis=0, keepdims=True))
    pv = _tn_dot(vc, p.astype(BF16))
    rl = has_key / pv[HEAD_DIM:HEAD_DIM + 1]
    pn = p * rl
    psum = pn[:, 0:QB]
    for h in range(1, H_NSA):
        psum = psum + pn[:, h * QB:(h + 1) * QB]
    o = pv[:HEAD_DIM] * (rl * gate(0))
    s = jnp.where(wmask, _nt_dot(kw, q4) + slope * kposw, NEG)
    p = jnp.exp2(s - jnp.max(s, axis=0, keepdims=True))
    pv = _tn_dot(vw, p.astype(BF16))
    o = o + pv[:HEAD_DIM] * (gate(2) / pv[HEAD_DIM:HEAD_DIM + 1])
    for hp in range(H_NSA // 2):
        pair = jnp.concatenate([o[:, (2 * hp) * QB:(2 * hp + 1) * QB], o[:, (2 * hp + 1) * QB:(2 * hp + 2) * QB]], axis=0)
        oa_ref[0, :, hp * LANES:(hp + 1) * LANES] = pair.T

    hi = psum.astype(BF16)
    lo = (psum - hi.astype(F32)).astype(BF16)
    imp = _dot(ovt_ref[...], hi) + _dot(ovt_ref[...], lo)
    j = lax.broadcasted_iota(jnp.int32, (SEL_LANES, QB), 0)
    qi = lax.broadcasted_iota(jnp.int32, (SEL_LANES, QB), 1)
    cur = jnp.right_shift(t0 + qi, int(math.log2(SEL_BLOCK)))
    forced = (j == 0) | (j == cur) | (j == cur - 1)
    imp = jnp.where(forced, -3e38, jnp.where(j <= cur, imp, NEG))
    jf = j.astype(F32)
    notsel = jnp.where(forced, 0.0, 1.0)
    for _ in range(SEL_TOPK - 3):
        mx = jnp.max(imp, axis=0, keepdims=True)
        idx = jnp.min(jnp.where(imp == mx, jf, float(SEL_LANES)), axis=0, keepdims=True)
        hit = jf == idx
        notsel = jnp.where(hit, 0.0, notsel)
        imp = jnp.where(hit, -3e38, imp)
    nst = notsel.T
    ns_ref[0] = nst.astype(BF16)
    used = 1.0 - jnp.min(nst, axis=0, keepdims=True)
    fl_ref[0, 0] = jnp.broadcast_to(used, (8, SEL_LANES)).astype(jnp.int32)


def _nsa_cmp_win(qa, kcr, vcr, kwkw, vwvw, gates, ovt):
    b, s, _ = qa.shape
    ncp = kcr.shape[1]
    return pl.pallas_call(
        _cw_body,
        grid=(b, s // QB),
        in_specs=[pl.BlockSpec((1, QB, 2 * LANES), lambda bi, i: (bi, i, 0)),
                  pl.BlockSpec((1, ncp, LANES), lambda bi, i: (bi, 0, 0)),
                  pl.BlockSpec((1, ncp, LANES), lambda bi, i: (bi, 0, 0)),
                  pl.BlockSpec((1, s, LANES), lambda bi, i: (bi, 0, 0)),
                  pl.BlockSpec((1, s, LANES), lambda bi, i: (bi, 0, 0)),
                  pl.BlockSpec((1, QB, LANES), lambda bi, i: (bi, i, 0)),
                  pl.BlockSpec((SEL_LANES, ncp), lambda bi, i: (0, 0))],
        out_specs=[pl.BlockSpec((1, QB, 2 * LANES), lambda bi, i: (bi, i, 0)),
                   pl.BlockSpec((1, QB, SEL_LANES), lambda bi, i: (bi, i, 0)),
                   pl.BlockSpec((1, 1, 8, SEL_LANES), lambda bi, i: (bi, i, 0, 0))],
        out_shape=[jax.ShapeDtypeStruct((b, s, 2 * LANES), F32),
                   jax.ShapeDtypeStruct((b, s, SEL_LANES), BF16),
                   jax.ShapeDtypeStruct((b, s // QB, 8, SEL_LANES), jnp.int32)],
        compiler_params=_cparams("parallel", "parallel"),
        name="nsa_cmp_win",
    )(qa, kcr, vcr, kwkw, vwvw, gates, ovt)


def _sel_body(fl_ref, q_ref, ns_ref, g_ref, part_ref, ks_ref, vs_ref, kc_ref, sc_ref, oa_ref,
              kaug_ref, vaug_ref, qaug_ref, m_ref, acc_ref, sa_ref, sb_ref, tl_ref, *, tk, qs):
    i = pl.program_id(1)
    t0 = i * qs
    nq = H_NSA * qs
    last_tile = kaug_ref.shape[0] // tk - 1

    @pl.when(i == 0)
    def _():
        lane = lax.broadcasted_iota(jnp.int32, (kaug_ref.shape[0], LANES), 1)
        kaug_ref[:, 0:LANES] = jnp.where(lane < HEAD_DIM, ks_ref[0], kc_ref[:, 0:LANES])
        kaug_ref[:, LANES:2 * LANES] = kc_ref[:, LANES:2 * LANES]
        vaug_ref[...] = jnp.where(lane < HEAD_DIM, vs_ref[0], jnp.ones((), BF16))

    q = q_ref[0]
    ns = ns_ref[0]
    lane = lax.broadcasted_iota(jnp.int32, (qs, LANES), 1)
    for h in range(H_NSA):
        slab = q[:, (h // 2) * LANES:(h // 2 + 1) * LANES].astype(F32)
        if h % 2:
            slab = pltpu.roll(slab, HEAD_DIM, 1)
        qaug_ref[h * qs:(h + 1) * qs, 0:LANES] = jnp.where(lane < HEAD_DIM, slab, sc_ref[h:h + 1, :]).astype(BF16)
        qaug_ref[h * qs:(h + 1) * qs, LANES:2 * LANES] = ns
    m_ref[...] = jnp.full(m_ref.shape, NEG, F32)
    acc_ref[...] = jnp.zeros(acc_ref.shape, F32)

    key_row = lax.broadcasted_iota(jnp.int32, (tk, nq), 0)
    q_lane = lax.broadcasted_iota(jnp.int32, (tk, nq), 1)
    dmat = key_row - q_lane % qs

    def scores(jt):
        k0 = pl.multiple_of(jnp.minimum(jt, last_tile) * tk, tk)
        return _nt_dot(kaug_ref[pl.ds(k0, tk), :], qaug_ref[...])

    def update(s_ref, jt, diagonal):
        k0 = pl.multiple_of(jnp.minimum(jt, last_tile) * tk, tk)
        s = s_ref[...]
        if diagonal:
            s = jnp.where(dmat <= t0 - jt * tk, s, NEG)
        m_old = m_ref[...]
        m_new = jnp.maximum(m_old, jnp.max(s, axis=0, keepdims=True))
        alpha = jnp.exp2(m_old - m_new)
        p = jnp.exp2(s - m_new)
        m_ref[...] = m_new
        pv = _tn_dot(vaug_ref[pl.ds(k0, tk), :], p.astype(BF16))
        acc_ref[...] = alpha * acc_ref[...] + pv[:acc_ref.shape[0]]

    blocks_per_tile = tk // SEL_BLOCK
    diag_tile = t0 // tk
    flag_rows = qs // QB
    frow = (pl.program_id(0) * pl.num_programs(1) + i) * flag_rows

    def collect(jt, cnt):
        used = 0
        for r in range(flag_rows):
            for c in range(blocks_per_tile):
                used = used | fl_ref[frow + r, jt * blocks_per_tile + c]
        tl_ref[cnt] = jt
        return cnt + used

    n_before = lax.fori_loop(0, diag_tile, collect, 0)
    tl_ref[n_before] = diag_tile
    tl_ref[n_before + 1] = 2 * (last_tile + 1)

    def body(jj, carry):
        sb_ref[...] = scores(tl_ref[2 * jj + 1])
        update(sa_ref, tl_ref[2 * jj], False)
        sa_ref[...] = scores(tl_ref[2 * jj + 2])
        update(sb_ref, tl_ref[2 * jj + 1], False)
        return carry

    n_pairs = n_before // 2
    sa_ref[...] = scores(tl_ref[0])
    lax.fori_loop(0, n_pairs, body, 0)
    sb_ref[...] = scores(tl_ref[2 * n_pairs + 1])
    update(sa_ref, tl_ref[2 * n_pairs], True)
    update(sb_ref, tl_ref[2 * n_pairs + 1], True)

    gt = g_ref[0].T
    for hp in range(H_NSA // 2):
        rows = []
        for hh in range(2):
            h = 2 * hp + hh
            cs = slice(h * qs, (h + 1) * qs)
            rows.append(acc_ref[0:HEAD_DIM, cs] * (gt[3 * h + 1:3 * h + 2] / acc_ref[HEAD_DIM:HEAD_DIM + 1, cs]))
        cs = slice(hp * LANES, (hp + 1) * LANES)
        oa_ref[0, :, cs] = part_ref[0, :, cs] + jnp.concatenate(rows, axis=0).T


def _nsa_selected(flags, qa, nsel, gates, part, ksks, vsvs, kconst, sconst, tk=256, qs=256):
    b, s, _ = qa.shape
    assert tk % qs == 0 and qs % QB == 0
    nq = H_NSA * qs
    grid_spec = pltpu.PrefetchScalarGridSpec(
        num_scalar_prefetch=1,
        grid=(b, s // qs),
        in_specs=[pl.BlockSpec((1, qs, 2 * LANES), lambda bi, i, fl: (bi, i, 0)),
                  pl.BlockSpec((1, qs, SEL_LANES), lambda bi, i, fl: (bi, i, 0)),
                  pl.BlockSpec((1, qs, LANES), lambda bi, i, fl: (bi, i, 0)),
                  pl.BlockSpec((1, qs, 2 * LANES), lambda bi, i, fl: (bi, i, 0)),
                  pl.BlockSpec((1, s, LANES), lambda bi, i, fl: (bi, 0, 0)),
                  pl.BlockSpec((1, s, LANES), lambda bi, i, fl: (bi, 0, 0)),
                  pl.BlockSpec((s, 2 * LANES), lambda bi, i, fl: (0, 0)),
                  pl.BlockSpec((8, LANES), lambda bi, i, fl: (0, 0))],
        out_specs=pl.BlockSpec((1, qs, 2 * LANES), lambda bi, i, fl: (bi, i, 0)),
        scratch_shapes=[pltpu.VMEM((s, 2 * LANES), BF16),
                        pltpu.VMEM((s, LANES), BF16),
                        pltpu.VMEM((nq, 2 * LANES), BF16),
                        pltpu.VMEM((1, nq), F32),
                        pltpu.VMEM((HEAD_DIM + 8, nq), F32),
                        pltpu.VMEM((tk, nq), F32),
                        pltpu.VMEM((tk, nq), F32),
                        pltpu.SMEM((s // tk + 8,), jnp.int32)])
    return pl.pallas_call(
        functools.partial(_sel_body, tk=tk, qs=qs),
        grid_spec=grid_spec,
        out_shape=jax.ShapeDtypeStruct((b, s, 2 * LANES), F32),
        compiler_params=_cparams("arbitrary", "arbitrary"),
        name="nsa_selected",
    )(flags, qa, nsel, gates, part, ksks, vsvs, kconst, sconst)


def _dil_body(q_ref, kp_ref, kc_ref, vp_ref, vc_ref, bias_ref, o_ref, lse_ref, *, r, m):
    i = pl.program_id(1)
    key_row = lax.broadcasted_iota(jnp.int32, (2 * QB, QB), 0)
    first = key_row >= jnp.where(i > 0, 0, QB)

    def rows(c, u):
        return pl.ds(u * QB * r + c, QB, stride=r) if r > 1 else pl.ds(u * QB, QB)

    for c in range(r):
        for u in range(m):
            cur = rows(c, u)
            q = q_ref[0, cur, :].astype(BF16)
            if u > 0:
                k_prev, v_prev = kc_ref[0, rows(c, u - 1), :], vc_ref[0, rows(c, u - 1), :]
            else:
                k_prev, v_prev = kp_ref[0, rows(c, m - 1), :], vp_ref[0, rows(c, m - 1), :]
            kk = jnp.concatenate([k_prev, kc_ref[0, cur, :]], axis=0).astype(BF16)
            vv = jnp.concatenate([v_prev, vc_ref[0, cur, :]], axis=0).astype(BF16)
            outs, lses = [], []
            for hh in range(2):
                s = _nt_dot(kk, _mask_half(q, hh)) + bias_ref[hh]
                if u == 0:
                    s = jnp.where(first, s, NEG)
                mx = jnp.max(s, axis=0, keepdims=True)
                p = jnp.exp2(s - mx)
                l = jnp.sum(p, axis=0, keepdims=True)
                o = _tn_dot(vv, p.astype(BF16)) * (1.0 / l)
                outs.append(o[hh * HEAD_DIM:(hh + 1) * HEAD_DIM])
                lses.append(jnp.broadcast_to(mx + jnp.log2(l), (HEAD_DIM, QB)))
            o_ref[0, cur, :] = jnp.concatenate(outs, axis=0).T
            lse_ref[0, cur, :] = jnp.concatenate(lses, axis=0).T


def _dilated(dq, dk, dv, bias, g, r, span=1024):
    b, s, _ = dq.shape
    m = max(span // (QB * r), 1)
    span = m * QB * r
    cur = lambda bi, i: (bi, i, g)
    prev = lambda bi, i: (bi, jnp.maximum(i - 1, 0), g)
    return pl.pallas_call(
        functools.partial(_dil_body, r=r, m=m),
        grid=(b, s // span),
        in_specs=[pl.BlockSpec((1, span, LANES), cur),
                  pl.BlockSpec((1, span, LANES), prev), pl.BlockSpec((1, span, LANES), cur),
                  pl.BlockSpec((1, span, LANES), prev), pl.BlockSpec((1, span, LANES), cur),
                  pl.BlockSpec((2, 2 * QB, QB), lambda bi, i: (0, 0, 0))],
        out_specs=[pl.BlockSpec((1, span, LANES), lambda bi, i: (bi, i, 0))] * 2,
        out_shape=[jax.ShapeDtypeStruct((b, s, LANES), F32)] * 2,
        compiler_params=_cparams("parallel", "parallel"),
        name=f"dilated_r{r}",
    )(dq, dk, dk, dv, dv, bias)


def _dil_bias(g):
    w, r = DIL_PAIRS[g]
    assert w // r == QB
    iq = np.arange(QB)[None, :]
    jk = np.arange(2 * QB)[:, None]
    dist = iq + QB - jk
    out = np.empty((2, 2 * QB, QB), np.float32)
    for hh in range(2):
        slope = SLOPES_DIL[2 * g + hh]
        out[hh] = np.where((dist >= 0) & (dist <= QB), -slope * LOG2E * r * dist, NEG)
    return jnp.asarray(out)


def _sb_body(q_ref, k_ref, v_ref, lt_ref, o_ref, acc_ref):
    i = pl.program_id(2)
    tq = q_ref.shape[1]
    q = q_ref[0]
    lt = lt_ref[...]
    lane = lax.broadcasted_iota(jnp.int32, (tq, 2 * tq), 1)
    dmat = lax.broadcasted_iota(jnp.int32, (tq, 2 * tq), 0) - lane % tq
    q2 = jnp.concatenate([_mask_half(q, 0), _mask_half(q, 1)], axis=0)
    acc_ref[...] = jnp.zeros(acc_ref.shape, F32)

    def cond(c):
        jt, _, cmax = c
        return (jt >= 0) & (cmax > SB_UNDERFLOW * LOG2E)

    def body(c):
        jt, carry, _ = c
        pv = None
        for u in range(2):
            ju = jt - u
            k0 = pl.multiple_of(jnp.maximum(ju, 0) * tq, tq)
            kt = k_ref[0, pl.ds(k0, tq), :]
            vt = v_ref[0, pl.ds(k0, tq), :]
            z = _nt_dot(kt, q2)
            lb = jnp.minimum(z, 0.0) - jnp.log2(1.0 + jnp.exp2(-jnp.abs(z)))
            lf = lb - z
            if u == 0:
                mask = dmat < jnp.where(jt < i, tq, 0)
                lf = jnp.where(mask, lf, 0.0)
            else:
                vt = jnp.where(ju >= 0, vt, jnp.zeros_like(vt))
            agg = _dot(lt, lf.astype(BF16))
            a = jnp.exp2(lb + agg[:tq] + carry)
            if u == 0:
                a = jnp.where(mask, a, 0.0)
            pvu = _tn_dot(vt, a.astype(BF16))
            pv = pvu if pv is None else pv + pvu
            carry = carry + agg[tq:tq + 1]
        acc_ref[0] += pv[:HEAD_DIM, :tq]
        acc_ref[1] += pv[HEAD_DIM:, tq:]
        return jt - 2, carry, jnp.max(carry)

    lax.while_loop(cond, body, (i, jnp.zeros((1, 2 * tq), F32), jnp.float32(0.0)))
    o_ref[0] = jnp.concatenate([acc_ref[0], acc_ref[1]], axis=0).T


def _stick_breaking(sq, sk, sv, lt):
    b, s, _ = sq.shape
    npair = H_SB // 2
    tq = lt.shape[1]
    return pl.pallas_call(
        _sb_body,
        grid=(b, npair, s // tq),
        in_specs=[pl.BlockSpec((1, tq, LANES), lambda bi, hp, i: (bi, i, hp)),
                  pl.BlockSpec((1, s, LANES), lambda bi, hp, i: (bi, 0, hp)),
                  pl.BlockSpec((1, s, LANES), lambda bi, hp, i: (bi, 0, hp)),
                  pl.BlockSpec(lt.shape, lambda bi, hp, i: (0, 0))],
        out_specs=pl.BlockSpec((1, tq, LANES), lambda bi, hp, i: (bi, i, hp)),
        out_shape=jax.ShapeDtypeStruct((b, s, npair * LANES), F32),
        scratch_shapes=[pltpu.VMEM((2, HEAD_DIM, tq), F32)],
        compiler_params=_cparams("parallel", "parallel", "parallel"),
        name="stick_breaking",
    )(sq, sk, sv, lt)


def _out_body(x_ref, oa_ref, d0, l0, d1, l1, d2, l2, oc_ref, w_ref, o_ref):
    lses = [l0[...], l1[...], l2[...]]
    m = jnp.maximum(jnp.maximum(lses[0], lses[1]), lses[2])
    es = [jnp.exp2(l - m) for l in lses]
    den = es[0] + es[1] + es[2]
    ob = (es[0] * d0[...] + es[1] * d1[...] + es[2] * d2[...]) / den
    na = H_NSA * HEAD_DIM
    nb = na + LANES
    acc = _dot(oa_ref[...].astype(BF16), w_ref[0:na, :])
    acc = acc + _dot(ob.astype(BF16), w_ref[na:nb, :])
    acc = acc + _dot(oc_ref[...].astype(BF16), w_ref[nb:, :])
    o_ref[...] = x_ref[...] + acc


def _out_mlp_body(x_ref, oa_ref, d0, l0, d1, l1, d2, l2, oc_ref, wo_ref, nw_ref, wu_ref, wd_ref, o_ref, *, fc):
    _out_body(x_ref, oa_ref, d0, l0, d1, l1, d2, l2, oc_ref, wo_ref, o_ref)
    x = o_ref[...]
    ms = jnp.mean(x * x, axis=-1, keepdims=True)
    h = (x * lax.rsqrt(ms + RMS_EPS) * nw_ref[...]).astype(BF16)
    for c in range(D_FF // fc):
        u = jnp.maximum(_dot(h, wu_ref[:, c * fc:(c + 1) * fc]), 0.0)
        o_ref[...] += _dot((u * u).astype(BF16), wd_ref[c * fc:(c + 1) * fc, :])


def _out_mlp(x2d, oa, dil, oc, w_out, nw, wu, wd, tm=512, fc=1024):
    n = x2d.shape[0]
    row = lambda w: pl.BlockSpec((tm, w), lambda i: (i, 0))
    whole = lambda a: pl.BlockSpec(a.shape, lambda i: (0, 0))
    single = lambda a: pl.BlockSpec(a.shape, lambda i: (0, 0), pipeline_mode=pl.Buffered(1))
    ins = [x2d, oa]
    specs = [row(D_MODEL), row(2 * LANES)]
    for o, lse in dil:
        ins += [o, lse]
        specs += [row(LANES), row(LANES)]
    ins += [oc, w_out, nw, wu, wd]
    specs += [row(oc.shape[1]), whole(w_out), whole(nw), single(wu), single(wd)]
    return pl.pallas_call(
        functools.partial(_out_mlp_body, fc=fc),
        grid=(n // tm,),
        in_specs=specs,
        out_specs=row(D_MODEL),
        out_shape=jax.ShapeDtypeStruct((n, D_MODEL), F32),
        compiler_params=_cparams("parallel"),
        name="out_mlp",
    )(*ins)


def _relayout_in_weight(w, g_nsa, g_dil):
    hd = HEAD_DIM
    kv0 = H_NSA * hd
    g0 = kv0 + 6 * hd
    b0 = g0 + 3 * H_NSA
    c0 = b0 + 3 * H_DIL * hd
    w = w.T
    kv = lambda c: w[kv0 + c * hd:kv0 + (c + 1) * hd]
    gate = jnp.pad(w[g0:b0], ((0, LANES - 3 * H_NSA), (0, 0)))
    nd = H_DIL * hd
    w_re = jnp.concatenate([w[:kv0], kv(2), kv(2), kv(4), kv(4), w[b0:b0 + 2 * nd],
                            kv(0), kv(1), kv(3), kv(3), kv(5), kv(5), gate, w[b0 + 2 * nd:c0], w[c0:]],
                           axis=0).astype(BF16)
    one = lambda n: jnp.ones((n,), F32)
    cv = jnp.concatenate([
        jnp.tile(g_nsa[0], H_NSA) * (SCALE * LOG2E), jnp.tile(g_nsa[2], 2), jnp.tile(g_nsa[3], 2),
        jnp.tile(g_dil[0], H_DIL) * (SCALE * LOG2E), jnp.tile(g_dil[1], H_DIL),
        one(4 * LANES), one(nd), one(H_SB * hd) * (SCALE * LOG2E), one(2 * H_SB * hd)])
    return w_re, cv.reshape(1, _IN_COLS)


def _relayout_cmp(w1, pe, w2, gk):
    hd, half = HEAD_DIM, CMP_LEN // 2
    w1r = w1.reshape(2, 2, half, hd, CMP_HIDDEN)
    per = pe.reshape(2, 2, half, hd)
    w1i, pei = [], []
    for c in range(2):
        pad = ((0, 0), (0, 0), (0, hd), (0, 0)) if c == 0 else ((0, 0), (0, 0), (hd, 0), (0, 0))
        w1i.append(jnp.pad(w1r[c], pad).reshape(2, half * LANES, CMP_HIDDEN))
        pei.append(jnp.pad(per[c], pad[:3]).reshape(2, half * LANES))
    w1i = jnp.concatenate(w1i, axis=0).astype(BF16)
    pe8 = jnp.pad(jnp.concatenate(pei, axis=0), ((0, 4), (0, 0)))
    w2r = jnp.concatenate([w2, w2], axis=-1).astype(BF16)
    return w1i, pe8, w2r, jnp.tile(gk, 2).reshape(1, LANES)


def _constants(s):
    ncp = s // CMP_STRIDE
    n_cmp = (s - CMP_LEN) // CMP_STRIDE + 1
    n = np.arange(ncp)[:, None]
    j = np.arange(SEL_LANES)[None, :]
    ov = ((CMP_STRIDE * n <= SEL_BLOCK * j + SEL_BLOCK - 1) & (CMP_STRIDE * n + CMP_LEN - 1 >= SEL_BLOCK * j)
          & (n < n_cmp) & (j < s // SEL_BLOCK))
    pos = np.arange(s)
    kconst = np.zeros((s, 2 * LANES), np.float32)
    kconst[:, HEAD_DIM:HEAD_DIM + 3] = (pos // SEL_BLOCK * SEL_BLOCK)[:, None]
    kconst[:, HEAD_DIM + 3:HEAD_DIM + 6] = (pos % SEL_BLOCK)[:, None]
    kconst[:, LANES:] = np.where(pos[:, None] // SEL_BLOCK == j, NEG, 0.0)
    sconst = np.zeros((8, LANES), np.float32)
    for h in range(H_NSA):
        rest = np.float32(SLOPES_NSA[h] * LOG2E)
        for c in range(3):
            piece = rest.astype(BF16).astype(np.float32)
            sconst[h, HEAD_DIM + c] = sconst[h, HEAD_DIM + 3 + c] = piece
            rest = np.float32(rest - piece)
    a2 = np.arange(2 * QB)
    lt = np.concatenate([(a2[None, :] > a2[:, None]).astype(np.float32), np.ones((8, 2 * QB), np.float32)])
    a = np.arange(_SLAB)
    gm = (a[:, None] // HEAD_DIM == a[None, :] // HEAD_DIM).astype(np.float32) / HEAD_DIM
    return dict(ovt=jnp.asarray(ov.T, BF16), kconst=jnp.asarray(kconst, BF16), sconst=jnp.asarray(sconst),
                lt=jnp.asarray(lt, BF16), gm=jnp.asarray(gm, BF16),
                dil_bias=[_dil_bias(g) for g in range(len(DIL_PAIRS))])


def kernel(x, norm_mix, norm_mlp, w_in, qk_gain_nsa, qk_gain_dil, cmp_pe, cmp_w1, cmp_w2, w_out, w_up, w_down):
    b, s, d = x.shape
    assert d == D_MODEL and s % (DIL_PAIRS[-1][1] * QB) == 0 and s // SEL_BLOCK <= SEL_LANES
    assert s >= WIN_NSA + QB
    n = b * s
    cst = _constants(s)
    x2d = x.reshape(n, d)
    for l in range(w_in.shape[0]):
        w_re, cv = _relayout_in_weight(w_in[l], qk_gain_nsa[l], qk_gain_dil[l])
        qa, ksks, kwkw, dq, dk, kvc, vsvs, vwvw, gates, dv, sq, sk, sv = _in_proj(
            x2d, norm_mix[l].reshape(1, d), w_re, cv, cst["gm"])
        tok = lambda a: a.reshape(b, s, a.shape[-1])
        w1i, pe8, w2r, gk = _relayout_cmp(cmp_w1[l], cmp_pe[l], cmp_w2[l], qk_gain_nsa[l, 1])
        kcr, vcr = _compress(tok(kvc), w1i, pe8, w2r, gk)
        part, nsel, used = _nsa_cmp_win(tok(qa), kcr, vcr, tok(kwkw), tok(vwvw), tok(gates), cst["ovt"])
        oa = _nsa_selected(used[:, :, 0, :].reshape(n // QB, SEL_LANES), tok(qa), nsel, tok(gates), part,
                           tok(ksks), tok(vsvs), cst["kconst"], cst["sconst"])
        dil = []
        for g, (_, r) in enumerate(DIL_PAIRS):
            o, lse = _dilated(tok(dq), tok(dk), tok(dv), cst["dil_bias"][g], g, r)
            dil.append((o.reshape(n, LANES), lse.reshape(n, LANES)))
        oc = _stick_breaking(tok(sq), tok(sk), tok(sv), cst["lt"])
        x2d = _out_mlp(x2d, oa.reshape(n, 2 * LANES), dil, oc.reshape(n, -1), w_out[l].astype(BF16),
                       norm_mlp[l].reshape(1, d), w_up[l].astype(BF16), w_down[l].astype(BF16))
    return x2d.reshape(b, s, d)
```

```python
import functools
import math

import numpy as np
import jax
import jax.numpy as jnp
from jax import lax
from jax.experimental import pallas as pl
from jax.experimental.pallas import tpu as pltpu

F32 = jnp.float32
BF16 = jnp.bfloat16

D_MODEL = 1024
HEAD_DIM = 64
H_NSA = 4
H_DIL = 6
H_SB = 6
DIL_PAIRS = ((128, 1), (512, 4), (2048, 16))
CMP_LEN = 32
CMP_STRIDE = 16
CMP_HIDDEN = 128
SEL_BLOCK = 64
SEL_TOPK = 16
WIN_NSA = 512
D_FF = 4 * D_MODEL
RMS_EPS = 1e-6
NEG = -1e30
FORCE_BONUS = 1e4
LOG2E = 1.4426950408889634
SCALE = HEAD_DIM ** -0.5
LANES = 128
QB = 128
QC = 256
SEL_LANES = 128
SB_UNDERFLOW = -104.0

_SLOPES = [2.0 ** (-8.0 * i / (H_NSA + H_DIL)) for i in range(1, H_NSA + H_DIL + 1)]
SLOPES_DIL = _SLOPES[:H_DIL]
SLOPES_NSA = _SLOPES[H_DIL:]

_IN_SEGS = (("qa", 2, "norm"), ("ksks", 1, "norm"), ("kwkw", 1, "norm"), ("dq", 3, "norm"), ("dk", 3, "norm"),
            ("kvc", 1, "raw"), ("vsvs", 1, "raw"), ("vwvw", 1, "raw"), ("gate", 1, "gate"),
            ("dv", 3, "raw"), ("sq", 3, "raw"), ("sk", 3, "raw"), ("sv", 3, "raw"))
_IN_COLS = sum(n for _, n, _ in _IN_SEGS) * LANES
_SLAB = 2 * LANES
_F32_SEGS = ("kvc", "dq", "dk", "dv")
_VMEM_LIMIT = 56 * 1024 * 1024


def _cparams(*sem, vmem=_VMEM_LIMIT):
    return pltpu.CompilerParams(dimension_semantics=sem, vmem_limit_bytes=vmem)


def _nt_dot(a, b):
    return lax.dot_general(a, b, (((1,), (1,)), ((), ())), preferred_element_type=F32)


def _dot(a, b):
    return jnp.dot(a, b, preferred_element_type=F32)


def _tn_dot(a, b):
    return lax.dot_general(a, b, (((0,), (0,)), ((), ())), preferred_element_type=F32)


def _split_dot(x, m):
    hi = x.astype(BF16)
    lo = (x - hi.astype(F32)).astype(BF16)
    return _dot(hi, m) + _dot(lo, m)


def _mask_half(x, hh):
    lane = lax.broadcasted_iota(jnp.int32, x.shape, x.ndim - 1)
    keep = (lane % LANES < HEAD_DIM) if hh == 0 else (lane % LANES >= HEAD_DIM)
    return jnp.where(keep, x, jnp.zeros_like(x))


def _in_proj_body(x_ref, nw_ref, w_ref, cv_ref, gm_ref, *out_refs):
    x = x_ref[...]
    ms = jnp.mean(x * x, axis=-1, keepdims=True)
    h = (x * lax.rsqrt(ms + RMS_EPS) * nw_ref[...]).astype(BF16)
    gm = gm_ref[...]
    tiles = [(o_ref, t, kind) for (name, ntile, kind), o_ref in zip(_IN_SEGS, out_refs) for t in range(ntile)]
    for sl in range(_IN_COLS // _SLAB):
        c0 = sl * _SLAB
        y = _nt_dot(h, w_ref[c0:c0 + _SLAB, :])
        if tiles[2 * sl][2] == "norm":
            msq = _dot((y * y).astype(BF16), gm)
            y = y * lax.rsqrt(msq + RMS_EPS)
        y = y * cv_ref[:, c0:c0 + _SLAB]
        for half in range(2):
            o_ref, t, kind = tiles[2 * sl + half]
            yh = y[:, half * LANES:(half + 1) * LANES]
            if kind == "gate":
                yh = jax.nn.sigmoid(yh)
            o_ref[:, t * LANES:(t + 1) * LANES] = yh.astype(o_ref.dtype)


def _in_proj(x2d, nw, w_re, cv, gm, tm=512):
    n = x2d.shape[0]
    out_shape, out_specs = [], []
    for name, ntile, kind in _IN_SEGS:
        dt = F32 if kind == "gate" or name in _F32_SEGS else BF16
        out_shape.append(jax.ShapeDtypeStruct((n, ntile * LANES), dt))
        out_specs.append(pl.BlockSpec((tm, ntile * LANES), lambda i: (i, 0)))
    return pl.pallas_call(
        _in_proj_body,
        grid=(n // tm,),
        in_specs=[pl.BlockSpec((tm, D_MODEL), lambda i: (i, 0)),
                  pl.BlockSpec((1, D_MODEL), lambda i: (0, 0)),
                  pl.BlockSpec((_IN_COLS, D_MODEL), lambda i: (0, 0)),
                  pl.BlockSpec((1, _IN_COLS), lambda i: (0, 0)),
                  pl.BlockSpec((_SLAB, _SLAB), lambda i: (0, 0))],
        out_specs=out_specs,
        out_shape=out_shape,
        compiler_params=_cparams("parallel"),
        name="in_proj",
    )(x2d, nw, w_re, cv, gm)


def _gelu_tanh(x):
    return 0.5 * x * (1.0 + jnp.tanh(0.7978845608028654 * (x + 0.044715 * (x * x * x))))


def _cmp_body(x_ref, w1_ref, pe_ref, w2_ref, gk_ref, kc_ref, vc_ref):
    ncp = kc_ref.shape[1]
    pe = pe_ref[...].astype(BF16)
    x = jnp.concatenate([x_ref[0, pl.ds(l, ncp, stride=CMP_STRIDE), :].astype(BF16) for l in range(CMP_STRIDE)],
                        axis=1)
    outs = []
    for c in range(2):
        top = _dot(x, w1_ref[2 * c])
        bot = _dot(x, w1_ref[2 * c + 1])
        bias = _dot(pe, w1_ref[2 * c])[2 * c:2 * c + 1] + _dot(pe, w1_ref[2 * c + 1])[2 * c + 1:2 * c + 2]
        hid = top + pltpu.roll(bot, ncp - 1, 0) + bias
        outs.append(_dot(_gelu_tanh(hid).astype(BF16), w2_ref[c]))
    kc = outs[0]
    kc = kc * lax.rsqrt(jnp.mean(kc * kc, axis=-1, keepdims=True) + RMS_EPS) * gk_ref[...]
    kc_ref[0] = kc.astype(BF16)
    vc_ref[0] = outs[1].astype(BF16)


def _compress(kvc, w1i, pe8, w2r, gk):
    b, s, _ = kvc.shape
    ncp, wid = s // CMP_STRIDE, CMP_STRIDE * LANES
    return pl.pallas_call(
        _cmp_body,
        grid=(b,),
        in_specs=[pl.BlockSpec((1, s, LANES), lambda i: (i, 0, 0)),
                  pl.BlockSpec((4, wid, CMP_HIDDEN), lambda i: (0, 0, 0)),
                  pl.BlockSpec((8, wid), lambda i: (0, 0)),
                  pl.BlockSpec((2, CMP_HIDDEN, LANES), lambda i: (0, 0, 0)),
                  pl.BlockSpec((1, LANES), lambda i: (0, 0))],
        out_specs=[pl.BlockSpec((1, ncp, LANES), lambda i: (i, 0, 0)),
                   pl.BlockSpec((1, ncp, LANES), lambda i: (i, 0, 0))],
        out_shape=[jax.ShapeDtypeStruct((b, ncp, LANES), BF16)] * 2,
        compiler_params=_cparams("parallel"),
        name="nsa_compress",
    )(kvc, w1i, pe8, w2r, gk)


def _softmax_cols(s, mask):
    m = jnp.max(s, axis=0, keepdims=True)
    p = jnp.where(mask, jnp.exp2(s - m), 0.0)
    l = jnp.maximum(jnp.sum(p, axis=0, keepdims=True), 1e-30)
    return p, l


def _cw_body(q_ref, kc_ref, vc_ref, kw_ref, vw_ref, g_ref, ovt_ref, oa_ref, ns_ref, fl_ref):
    i = pl.program_id(1)
    t0 = i * QC
    q = q_ref[0]
    gt = g_ref[0].T
    kc = kc_ref[0]
    vc = vc_ref[0]
    ncp = kc.shape[0]
    wk = WIN_NSA + QC

    nq = H_NSA * QC
    heads = range(H_NSA)
    q4 = jnp.concatenate([_mask_half(q[:, (h // 2) * LANES:(h // 2 + 1) * LANES], h % 2) for h in heads], axis=0)
    head = lax.broadcasted_iota(jnp.int32, (1, nq), 1) // QC
    slope = jnp.zeros((1, nq), F32)
    for h in heads:
        slope = jnp.where(head == h, SLOPES_NSA[h] * LOG2E, slope)
    gate = lambda br: jnp.concatenate([gt[3 * h + br:3 * h + br + 1] for h in heads], axis=1)

    per_head = lambda a: jnp.concatenate([a] * H_NSA, axis=1)

    n_row = lax.broadcasted_iota(jnp.int32, (ncp, QC), 0)
    q_lane = lax.broadcasted_iota(jnp.int32, (ncp, QC), 1)
    vis = per_head((t0 - (CMP_LEN - 1)) + q_lane - CMP_STRIDE * n_row) >= 0
    cend = per_head((CMP_STRIDE * n_row + (CMP_LEN - 1)).astype(F32))

    start = pl.multiple_of(jnp.maximum(t0 - WIN_NSA, 0), QC)
    kw = kw_ref[0, pl.ds(start, wk), :]
    vw = vw_ref[0, pl.ds(start, wk), :]
    j_row = lax.broadcasted_iota(jnp.int32, (wk, QC), 0)
    r_lane = lax.broadcasted_iota(jnp.int32, (wk, QC), 1)
    dw = per_head((t0 - start) + r_lane - j_row)
    wmask = (dw >= 0) & (dw < WIN_NSA)
    kposw = per_head((start + j_row).astype(F32))

    vlane = lax.broadcasted_iota(jnp.int32, (1, LANES), 1) < HEAD_DIM
    vc = jnp.where(vlane, vc, jnp.ones((), BF16))
    vw = jnp.where(vlane, vw, jnp.ones((), BF16))
    has_key = vis[0:1, :].astype(F32)

    s = jnp.where(vis, _nt_dot(kc, q4) + slope * cend, NEG)
    p = jnp.exp2(s - jnp.max(s, axis=0, keepdims=True))
    pv = _tn_dot(vc, p.astype(BF16))
    rl = has_key / pv[HEAD_DIM:HEAD_DIM + 1]
    pn = p * rl
    psum = pn[:, 0:QC]
    for h in range(1, H_NSA):
        psum = psum + pn[:, h * QC:(h + 1) * QC]
    o = pv[:HEAD_DIM] * (rl * gate(0))
    s = jnp.where(wmask, _nt_dot(kw, q4) + slope * kposw, NEG)
    p = jnp.exp2(s - jnp.max(s, axis=0, keepdims=True))
    pv = _tn_dot(vw, p.astype(BF16))
    o = o + pv[:HEAD_DIM] * (gate(2) / pv[HEAD_DIM:HEAD_DIM + 1])
    for hp in range(H_NSA // 2):
        pair = jnp.concatenate([o[:, (2 * hp) * QC:(2 * hp + 1) * QC], o[:, (2 * hp + 1) * QC:(2 * hp + 2) * QC]], axis=0)
        oa_ref[0, :, hp * LANES:(hp + 1) * LANES] = pair.T

    hi = psum.astype(BF16)
    lo = (psum - hi.astype(F32)).astype(BF16)
    imp = _dot(ovt_ref[...], hi) + _dot(ovt_ref[...], lo)
    j = lax.broadcasted_iota(jnp.int32, (SEL_LANES, QC), 0)
    qi = lax.broadcasted_iota(jnp.int32, (SEL_LANES, QC), 1)
    cur = jnp.right_shift(t0 + qi, int(math.log2(SEL_BLOCK)))
    forced = (j == 0) | (j == cur) | (j == cur - 1)
    imp = jnp.where(forced, -3e38, jnp.where(j <= cur, imp, NEG))
    jf = j.astype(F32)
    notsel = jnp.where(forced, 0.0, 1.0)
    for _ in range(SEL_TOPK - 3):
        mx = jnp.max(imp, axis=0, keepdims=True)
        idx = jnp.min(jnp.where(imp == mx, jf, float(SEL_LANES)), axis=0, keepdims=True)
        hit = jf == idx
        notsel = jnp.where(hit, 0.0, notsel)
        imp = jnp.where(hit, -3e38, imp)
    nst = notsel.T
    ns_ref[0] = nst.astype(BF16)
    used = 1.0 - jnp.min(nst, axis=0, keepdims=True)
    fl_ref[0, 0] = jnp.broadcast_to(used, (8, SEL_LANES)).astype(jnp.int32)


def _nsa_cmp_win(qa, kcr, vcr, kwkw, vwvw, gates, ovt):
    b, s, _ = qa.shape
    ncp = kcr.shape[1]
    return pl.pallas_call(
        _cw_body,
        grid=(b, s // QC),
        in_specs=[pl.BlockSpec((1, QC, 2 * LANES), lambda bi, i: (bi, i, 0)),
                  pl.BlockSpec((1, ncp, LANES), lambda bi, i: (bi, 0, 0)),
                  pl.BlockSpec((1, ncp, LANES), lambda bi, i: (bi, 0, 0)),
                  pl.BlockSpec((1, s, LANES), lambda bi, i: (bi, 0, 0)),
                  pl.BlockSpec((1, s, LANES), lambda bi, i: (bi, 0, 0)),
                  pl.BlockSpec((1, QC, LANES), lambda bi, i: (bi, i, 0)),
                  pl.BlockSpec((SEL_LANES, ncp), lambda bi, i: (0, 0))],
        out_specs=[pl.BlockSpec((1, QC, 2 * LANES), lambda bi, i: (bi, i, 0)),
                   pl.BlockSpec((1, QC, SEL_LANES), lambda bi, i: (bi, i, 0)),
                   pl.BlockSpec((1, 1, 8, SEL_LANES), lambda bi, i: (bi, i, 0, 0))],
        out_shape=[jax.ShapeDtypeStruct((b, s, 2 * LANES), F32),
                   jax.ShapeDtypeStruct((b, s, SEL_LANES), BF16),
                   jax.ShapeDtypeStruct((b, s // QC, 8, SEL_LANES), jnp.int32)],
        compiler_params=_cparams("parallel", "parallel"),
        name="nsa_cmp_win",
    )(qa, kcr, vcr, kwkw, vwvw, gates, ovt)


def _sel_body(fl_ref, q_ref, ns_ref, g_ref, part_ref, ks_ref, vs_ref, kc_ref, sc_ref, oa_ref,
              kaug_ref, vaug_ref, qaug_ref, m_ref, acc_ref, sa_ref, sb_ref, tl_ref, *, tk, qs):
    i = pl.program_id(1)
    t0 = i * qs
    nq = H_NSA * qs
    last_tile = kaug_ref.shape[0] // tk - 1

    @pl.when(i == 0)
    def _():
        lane = lax.broadcasted_iota(jnp.int32, (kaug_ref.shape[0], LANES), 1)
        kaug_ref[:, 0:LANES] = jnp.where(lane < HEAD_DIM, ks_ref[0], kc_ref[:, 0:LANES])
        kaug_ref[:, LANES:2 * LANES] = kc_ref[:, LANES:2 * LANES]
        vaug_ref[...] = jnp.where(lane < HEAD_DIM, vs_ref[0], jnp.ones((), BF16))

    q = q_ref[0]
    ns = ns_ref[0]
    lane = lax.broadcasted_iota(jnp.int32, (qs, LANES), 1)
    for h in range(H_NSA):
        slab = q[:, (h // 2) * LANES:(h // 2 + 1) * LANES].astype(F32)
        if h % 2:
            slab = pltpu.roll(slab, HEAD_DIM, 1)
        qaug_ref[h * qs:(h + 1) * qs, 0:LANES] = jnp.where(lane < HEAD_DIM, slab, sc_ref[h:h + 1, :]).astype(BF16)
        qaug_ref[h * qs:(h + 1) * qs, LANES:2 * LANES] = ns
    m_ref[...] = jnp.full(m_ref.shape, NEG, F32)
    acc_ref[...] = jnp.zeros(acc_ref.shape, F32)

    key_row = lax.broadcasted_iota(jnp.int32, (tk, nq), 0)
    q_lane = lax.broadcasted_iota(jnp.int32, (tk, nq), 1)
    dmat = key_row - q_lane % qs

    def scores(jt):
        k0 = pl.multiple_of(jnp.minimum(jt, last_tile) * tk, tk)
        return _nt_dot(kaug_ref[pl.ds(k0, tk), :], qaug_ref[...])

    def update(s_ref, jt, diagonal):
        k0 = pl.multiple_of(jnp.minimum(jt, last_tile) * tk, tk)
        s = s_ref[...]
        if diagonal:
            s = jnp.where(dmat <= t0 - jt * tk, s, NEG)
        m_old = m_ref[...]
        m_new = jnp.maximum(m_old, jnp.max(s, axis=0, keepdims=True))
        alpha = jnp.exp2(m_old - m_new)
        p = jnp.exp2(s - m_new)
        m_ref[...] = m_new
        pv = _tn_dot(vaug_ref[pl.ds(k0, tk), :], p.astype(BF16))
        acc_ref[...] = alpha * acc_ref[...] + pv[:acc_ref.shape[0]]

    blocks_per_tile = tk // SEL_BLOCK
    diag_tile = t0 // tk
    flag_rows = qs // QC
    frow = (pl.program_id(0) * pl.num_programs(1) + i) * flag_rows

    def collect(jt, cnt):
        used = 0
        for r in range(flag_rows):
            for c in range(blocks_per_tile):
                used = used | fl_ref[frow + r, jt * blocks_per_tile + c]
        tl_ref[cnt] = jt
        return cnt + used

    n_before = lax.fori_loop(0, diag_tile, collect, 0)
    tl_ref[n_before] = diag_tile
    tl_ref[n_before + 1] = 2 * (last_tile + 1)

    def body(jj, carry):
        sb_ref[...] = scores(tl_ref[2 * jj + 1])
        update(sa_ref, tl_ref[2 * jj], False)
        sa_ref[...] = scores(tl_ref[2 * jj + 2])
        update(sb_ref, tl_ref[2 * jj + 1], False)
        return carry

    n_pairs = n_before // 2
    sa_ref[...] = scores(tl_ref[0])
    lax.fori_loop(0, n_pairs, body, 0)
    sb_ref[...] = scores(tl_ref[2 * n_pairs + 1])
    update(sa_ref, tl_ref[2 * n_pairs], True)
    update(sb_ref, tl_ref[2 * n_pairs + 1], True)

    gt = g_ref[0].T
    for hp in range(H_NSA // 2):
        rows = []
        for hh in range(2):
            h = 2 * hp + hh
            cs = slice(h * qs, (h + 1) * qs)
            rows.append(acc_ref[0:HEAD_DIM, cs] * (gt[3 * h + 1:3 * h + 2] / acc_ref[HEAD_DIM:HEAD_DIM + 1, cs]))
        cs = slice(hp * LANES, (hp + 1) * LANES)
        oa_ref[0, :, cs] = part_ref[0, :, cs] + jnp.concatenate(rows, axis=0).T


def _nsa_selected(flags, qa, nsel, gates, part, ksks, vsvs, kconst, sconst, tk=256, qs=256):
    b, s, _ = qa.shape
    assert tk % qs == 0 and qs % QC == 0
    nq = H_NSA * qs
    grid_spec = pltpu.PrefetchScalarGridSpec(
        num_scalar_prefetch=1,
        grid=(b, s // qs),
        in_specs=[pl.BlockSpec((1, qs, 2 * LANES), lambda bi, i, fl: (bi, i, 0)),
                  pl.BlockSpec((1, qs, SEL_LANES), lambda bi, i, fl: (bi, i, 0)),
                  pl.BlockSpec((1, qs, LANES), lambda bi, i, fl: (bi, i, 0)),
                  pl.BlockSpec((1, qs, 2 * LANES), lambda bi, i, fl: (bi, i, 0)),
                  pl.BlockSpec((1, s, LANES), lambda bi, i, fl: (bi, 0, 0)),
                  pl.BlockSpec((1, s, LANES), lambda bi, i, fl: (bi, 0, 0)),
                  pl.BlockSpec((s, 2 * LANES), lambda bi, i, fl: (0, 0)),
                  pl.BlockSpec((8, LANES), lambda bi, i, fl: (0, 0))],
        out_specs=pl.BlockSpec((1, qs, 2 * LANES), lambda bi, i, fl: (bi, i, 0)),
        scratch_shapes=[pltpu.VMEM((s, 2 * LANES), BF16),
                        pltpu.VMEM((s, LANES), BF16),
                        pltpu.VMEM((nq, 2 * LANES), BF16),
                        pltpu.VMEM((1, nq), F32),
                        pltpu.VMEM((HEAD_DIM + 8, nq), F32),
                        pltpu.VMEM((tk, nq), F32),
                        pltpu.VMEM((tk, nq), F32),
                        pltpu.SMEM((s // tk + 8,), jnp.int32)])
    return pl.pallas_call(
        functools.partial(_sel_body, tk=tk, qs=qs),
        grid_spec=grid_spec,
        out_shape=jax.ShapeDtypeStruct((b, s, 2 * LANES), F32),
        compiler_params=_cparams("arbitrary", "arbitrary"),
        name="nsa_selected",
    )(flags, qa, nsel, gates, part, ksks, vsvs, kconst, sconst)


def _dil_body(q_ref, kp_ref, kc_ref, vp_ref, vc_ref, bias_ref, o_ref, lse_ref, *, r, m):
    i = pl.program_id(1)
    key_row = lax.broadcasted_iota(jnp.int32, (2 * QB, QB), 0)
    first = key_row >= jnp.where(i > 0, 0, QB)

    def rows(c, u):
        return pl.ds(u * QB * r + c, QB, stride=r) if r > 1 else pl.ds(u * QB, QB)

    for c in range(r):
        for u in range(m):
            cur = rows(c, u)
            q = q_ref[0, cur, :].astype(BF16)
            if u > 0:
                k_prev, v_prev = kc_ref[0, rows(c, u - 1), :], vc_ref[0, rows(c, u - 1), :]
            else:
                k_prev, v_prev = kp_ref[0, rows(c, m - 1), :], vp_ref[0, rows(c, m - 1), :]
            kk = jnp.concatenate([k_prev, kc_ref[0, cur, :]], axis=0).astype(BF16)
            vv = jnp.concatenate([v_prev, vc_ref[0, cur, :]], axis=0).astype(BF16)
            outs, lses = [], []
            for hh in range(2):
                s = _nt_dot(kk, _mask_half(q, hh)) + bias_ref[hh]
                if u == 0:
                    s = jnp.where(first, s, NEG)
                mx = jnp.max(s, axis=0, keepdims=True)
                p = jnp.exp2(s - mx)
                l = jnp.sum(p, axis=0, keepdims=True)
                o = _tn_dot(vv, p.astype(BF16)) * (1.0 / l)
                outs.append(o[hh * HEAD_DIM:(hh + 1) * HEAD_DIM])
                lses.append(jnp.broadcast_to(mx + jnp.log2(l), (HEAD_DIM, QB)))
            o_ref[0, cur, :] = jnp.concatenate(outs, axis=0).T
            lse_ref[0, cur, :] = jnp.concatenate(lses, axis=0).T


def _dilated(dq, dk, dv, bias, g, r, span=1024):
    b, s, _ = dq.shape
    m = max(span // (QB * r), 1)
    span = m * QB * r
    cur = lambda bi, i: (bi, i, g)
    prev = lambda bi, i: (bi, jnp.maximum(i - 1, 0), g)
    return pl.pallas_call(
        functools.partial(_dil_body, r=r, m=m),
        grid=(b, s // span),
        in_specs=[pl.BlockSpec((1, span, LANES), cur),
                  pl.BlockSpec((1, span, LANES), prev), pl.BlockSpec((1, span, LANES), cur),
                  pl.BlockSpec((1, span, LANES), prev), pl.BlockSpec((1, span, LANES), cur),
                  pl.BlockSpec((2, 2 * QB, QB), lambda bi, i: (0, 0, 0))],
        out_specs=[pl.BlockSpec((1, span, LANES), lambda bi, i: (bi, i, 0))] * 2,
        out_shape=[jax.ShapeDtypeStruct((b, s, LANES), F32)] * 2,
        compiler_params=_cparams("parallel", "parallel"),
        name=f"dilated_r{r}",
    )(dq, dk, dk, dv, dv, bias)


def _dil_bias(g):
    w, r = DIL_PAIRS[g]
    assert w // r == QB
    iq = np.arange(QB)[None, :]
    jk = np.arange(2 * QB)[:, None]
    dist = iq + QB - jk
    out = np.empty((2, 2 * QB, QB), np.float32)
    for hh in range(2):
        slope = SLOPES_DIL[2 * g + hh]
        out[hh] = np.where((dist >= 0) & (dist <= QB), -slope * LOG2E * r * dist, NEG)
    return jnp.asarray(out)


def _sb_body(q_ref, k_ref, v_ref, lt_ref, o_ref, acc_ref):
    i = pl.program_id(2)
    tq = q_ref.shape[1]
    q = q_ref[0]
    lt = lt_ref[...]
    lane = lax.broadcasted_iota(jnp.int32, (tq, 2 * tq), 1)
    dmat = lax.broadcasted_iota(jnp.int32, (tq, 2 * tq), 0) - lane % tq
    q2 = jnp.concatenate([_mask_half(q, 0), _mask_half(q, 1)], axis=0)
    acc_ref[...] = jnp.zeros(acc_ref.shape, F32)

    def cond(c):
        jt, _, cmax = c
        return (jt >= 0) & (cmax > SB_UNDERFLOW * LOG2E)

    def body(c):
        jt, carry, _ = c
        pv = None
        for u in range(2):
            ju = jt - u
            k0 = pl.multiple_of(jnp.maximum(ju, 0) * tq, tq)
            kt = k_ref[0, pl.ds(k0, tq), :]
            vt = v_ref[0, pl.ds(k0, tq), :]
            z = _nt_dot(kt, q2)
            lb = jnp.minimum(z, 0.0) - jnp.log2(1.0 + jnp.exp2(-jnp.abs(z)))
            lf = lb - z
            if u == 0:
                mask = dmat < jnp.where(jt < i, tq, 0)
                lf = jnp.where(mask, lf, 0.0)
            else:
                vt = jnp.where(ju >= 0, vt, jnp.zeros_like(vt))
            agg = _dot(lt, lf.astype(BF16))
            a = jnp.exp2(lb + agg[:tq] + carry)
            if u == 0:
                a = jnp.where(mask, a, 0.0)
            pvu = _tn_dot(vt, a.astype(BF16))
            pv = pvu if pv is None else pv + pvu
            carry = carry + agg[tq:tq + 1]
        acc_ref[0] += pv[:HEAD_DIM, :tq]
        acc_ref[1] += pv[HEAD_DIM:, tq:]
        return jt - 2, carry, jnp.max(carry)

    lax.while_loop(cond, body, (i, jnp.zeros((1, 2 * tq), F32), jnp.float32(0.0)))
    o_ref[0] = jnp.concatenate([acc_ref[0], acc_ref[1]], axis=0).T


def _stick_breaking(sq, sk, sv, lt):
    b, s, _ = sq.shape
    npair = H_SB // 2
    tq = lt.shape[1]
    return pl.pallas_call(
        _sb_body,
        grid=(b, npair, s // tq),
        in_specs=[pl.BlockSpec((1, tq, LANES), lambda bi, hp, i: (bi, i, hp)),
                  pl.BlockSpec((1, s, LANES), lambda bi, hp, i: (bi, 0, hp)),
                  pl.BlockSpec((1, s, LANES), lambda bi, hp, i: (bi, 0, hp)),
                  pl.BlockSpec(lt.shape, lambda bi, hp, i: (0, 0))],
        out_specs=pl.BlockSpec((1, tq, LANES), lambda bi, hp, i: (bi, i, hp)),
        out_shape=jax.ShapeDtypeStruct((b, s, npair * LANES), F32),
        scratch_shapes=[pltpu.VMEM((2, HEAD_DIM, tq), F32)],
        compiler_params=_cparams("parallel", "parallel", "parallel"),
        name="stick_breaking",
    )(sq, sk, sv, lt)


def _out_body(x_ref, oa_ref, d0, l0, d1, l1, d2, l2, oc_ref, w_ref, o_ref):
    lses = [l0[...], l1[...], l2[...]]
    m = jnp.maximum(jnp.maximum(lses[0], lses[1]), lses[2])
    es = [jnp.exp2(l - m) for l in lses]
    den = es[0] + es[1] + es[2]
    ob = (es[0] * d0[...] + es[1] * d1[...] + es[2] * d2[...]) / den
    na = H_NSA * HEAD_DIM
    nb = na + LANES
    acc = _dot(oa_ref[...].astype(BF16), w_ref[0:na, :])
    acc = acc + _dot(ob.astype(BF16), w_ref[na:nb, :])
    acc = acc + _dot(oc_ref[...].astype(BF16), w_ref[nb:, :])
    o_ref[...] = x_ref[...] + acc


def _out_mlp_body(x_ref, oa_ref, d0, l0, d1, l1, d2, l2, oc_ref, wo_ref, nw_ref, wu_ref, wd_ref, o_ref, *, fc):
    _out_body(x_ref, oa_ref, d0, l0, d1, l1, d2, l2, oc_ref, wo_ref, o_ref)
    x = o_ref[...]
    ms = jnp.mean(x * x, axis=-1, keepdims=True)
    h = (x * lax.rsqrt(ms + RMS_EPS) * nw_ref[...]).astype(BF16)
    for c in range(D_FF // fc):
        u = jnp.maximum(_dot(h, wu_ref[:, c * fc:(c + 1) * fc]), 0.0)
        o_ref[...] += _dot((u * u).astype(BF16), wd_ref[c * fc:(c + 1) * fc, :])


def _out_mlp(x2d, oa, dil, oc, w_out, nw, wu, wd, tm=512, fc=1024):
    n = x2d.shape[0]
    row = lambda w: pl.BlockSpec((tm, w), lambda i: (i, 0))
    whole = lambda a: pl.BlockSpec(a.shape, lambda i: (0, 0))
    single = lambda a: pl.BlockSpec(a.shape, lambda i: (0, 0), pipeline_mode=pl.Buffered(1))
    ins = [x2d, oa]
    specs = [row(D_MODEL), row(2 * LANES)]
    for o, lse in dil:
        ins += [o, lse]
        specs += [row(LANES), row(LANES)]
    ins += [oc, w_out, nw, wu, wd]
    specs += [row(oc.shape[1]), whole(w_out), whole(nw), single(wu), single(wd)]
    return pl.pallas_call(
        functools.partial(_out_mlp_body, fc=fc),
        grid=(n // tm,),
        in_specs=specs,
        out_specs=row(D_MODEL),
        out_shape=jax.ShapeDtypeStruct((n, D_MODEL), F32),
        compiler_params=_cparams("parallel"),
        name="out_mlp",
    )(*ins)


def _relayout_in_weight(w, g_nsa, g_dil):
    hd = HEAD_DIM
    kv0 = H_NSA * hd
    g0 = kv0 + 6 * hd
    b0 = g0 + 3 * H_NSA
    c0 = b0 + 3 * H_DIL * hd
    w = w.T
    kv = lambda c: w[kv0 + c * hd:kv0 + (c + 1) * hd]
    gate = jnp.pad(w[g0:b0], ((0, LANES - 3 * H_NSA), (0, 0)))
    nd = H_DIL * hd
    w_re = jnp.concatenate([w[:kv0], kv(2), kv(2), kv(4), kv(4), w[b0:b0 + 2 * nd],
                            kv(0), kv(1), kv(3), kv(3), kv(5), kv(5), gate, w[b0 + 2 * nd:c0], w[c0:]],
                           axis=0).astype(BF16)
    one = lambda n: jnp.ones((n,), F32)
    cv = jnp.concatenate([
        jnp.tile(g_nsa[0], H_NSA) * (SCALE * LOG2E), jnp.tile(g_nsa[2], 2), jnp.tile(g_nsa[3], 2),
        jnp.tile(g_dil[0], H_DIL) * (SCALE * LOG2E), jnp.tile(g_dil[1], H_DIL),
        one(4 * LANES), one(nd), one(H_SB * hd) * (SCALE * LOG2E), one(2 * H_SB * hd)])
    return w_re, cv.reshape(1, _IN_COLS)


def _relayout_cmp(w1, pe, w2, gk):
    hd, half = HEAD_DIM, CMP_LEN // 2
    w1r = w1.reshape(2, 2, half, hd, CMP_HIDDEN)
    per = pe.reshape(2, 2, half, hd)
    w1i, pei = [], []
    for c in range(2):
        pad = ((0, 0), (0, 0), (0, hd), (0, 0)) if c == 0 else ((0, 0), (0, 0), (hd, 0), (0, 0))
        w1i.append(jnp.pad(w1r[c], pad).reshape(2, half * LANES, CMP_HIDDEN))
        pei.append(jnp.pad(per[c], pad[:3]).reshape(2, half * LANES))
    w1i = jnp.concatenate(w1i, axis=0).astype(BF16)
    pe8 = jnp.pad(jnp.concatenate(pei, axis=0), ((0, 4), (0, 0)))
    w2r = jnp.concatenate([w2, w2], axis=-1).astype(BF16)
    return w1i, pe8, w2r, jnp.tile(gk, 2).reshape(1, LANES)


def _constants(s):
    ncp = s // CMP_STRIDE
    n_cmp = (s - CMP_LEN) // CMP_STRIDE + 1
    n = np.arange(ncp)[:, None]
    j = np.arange(SEL_LANES)[None, :]
    ov = ((CMP_STRIDE * n <= SEL_BLOCK * j + SEL_BLOCK - 1) & (CMP_STRIDE * n + CMP_LEN - 1 >= SEL_BLOCK * j)
          & (n < n_cmp) & (j < s // SEL_BLOCK))
    pos = np.arange(s)
    kconst = np.zeros((s, 2 * LANES), np.float32)
    kconst[:, HEAD_DIM:HEAD_DIM + 3] = (pos // SEL_BLOCK * SEL_BLOCK)[:, None]
    kconst[:, HEAD_DIM + 3:HEAD_DIM + 6] = (pos % SEL_BLOCK)[:, None]
    kconst[:, LANES:] = np.where(pos[:, None] // SEL_BLOCK == j, NEG, 0.0)
    sconst = np.zeros((8, LANES), np.float32)
    for h in range(H_NSA):
        rest = np.float32(SLOPES_NSA[h] * LOG2E)
        for c in range(3):
            piece = rest.astype(BF16).astype(np.float32)
            sconst[h, HEAD_DIM + c] = sconst[h, HEAD_DIM + 3 + c] = piece
            rest = np.float32(rest - piece)
    a2 = np.arange(2 * QB)
    lt = np.concatenate([(a2[None, :] > a2[:, None]).astype(np.float32), np.ones((8, 2 * QB), np.float32)])
    a = np.arange(_SLAB)
    gm = (a[:, None] // HEAD_DIM == a[None, :] // HEAD_DIM).astype(np.float32) / HEAD_DIM
    return dict(ovt=jnp.asarray(ov.T, BF16), kconst=jnp.asarray(kconst, BF16), sconst=jnp.asarray(sconst),
                lt=jnp.asarray(lt, BF16), gm=jnp.asarray(gm, BF16),
                dil_bias=[_dil_bias(g) for g in range(len(DIL_PAIRS))])


def kernel(x, norm_mix, norm_mlp, w_in, qk_gain_nsa, qk_gain_dil, cmp_pe, cmp_w1, cmp_w2, w_out, w_up, w_down):
    b, s, d = x.shape
    assert d == D_MODEL and s % (DIL_PAIRS[-1][1] * QB) == 0 and s // SEL_BLOCK <= SEL_LANES
    assert s >= WIN_NSA + QC
    n = b * s
    cst = _constants(s)
    x2d = x.reshape(n, d)
    for l in range(w_in.shape[0]):
        w_re, cv = _relayout_in_weight(w_in[l], qk_gain_nsa[l], qk_gain_dil[l])
        qa, ksks, kwkw, dq, dk, kvc, vsvs, vwvw, gates, dv, sq, sk, sv = _in_proj(
            x2d, norm_mix[l].reshape(1, d), w_re, cv, cst["gm"])
        tok = lambda a: a.reshape(b, s, a.shape[-1])
        w1i, pe8, w2r, gk = _relayout_cmp(cmp_w1[l], cmp_pe[l], cmp_w2[l], qk_gain_nsa[l, 1])
        kcr, vcr = _compress(tok(kvc), w1i, pe8, w2r, gk)
        part, nsel, used = _nsa_cmp_win(tok(qa), kcr, vcr, tok(kwkw), tok(vwvw), tok(gates), cst["ovt"])
        oa = _nsa_selected(used[:, :, 0, :].reshape(n // QC, SEL_LANES), tok(qa), nsel, tok(gates), part,
                           tok(ksks), tok(vsvs), cst["kconst"], cst["sconst"])
        dil = []
        for g, (_, r) in enumerate(DIL_PAIRS):
            o, lse = _dilated(tok(dq), tok(dk), tok(dv), cst["dil_bias"][g], g, r)
            dil.append((o.reshape(n, LANES), lse.reshape(n, LANES)))
        oc = _stick_breaking(tok(sq), tok(sk), tok(sv), cst["lt"])
        x2d = _out_mlp(x2d, oa.reshape(n, 2 * LANES), dil, oc.reshape(n, -1), w_out[l].astype(BF16),
                       norm_mlp[l].reshape(1, d), w_up[l].astype(BF16), w_down[l].astype(BF16))
    return x2d.reshape(b, s, d)
```

```python
import functools
import math

import numpy as np
import jax
import jax.numpy as jnp
from jax import lax
from jax.experimental import pallas as pl
from jax.experimental.pallas import tpu as pltpu

F32 = jnp.float32
BF16 = jnp.bfloat16

D_MODEL = 1024
HEAD_DIM = 64
H_NSA = 4
H_DIL = 6
H_SB = 6
DIL_PAIRS = ((128, 1), (512, 4), (2048, 16))
CMP_LEN = 32
CMP_STRIDE = 16
CMP_HIDDEN = 128
SEL_BLOCK = 64
SEL_TOPK = 16
WIN_NSA = 512
D_FF = 4 * D_MODEL
RMS_EPS = 1e-6
NEG = -1e30
FORCE_BONUS = 1e4
LOG2E = 1.4426950408889634
SCALE = HEAD_DIM ** -0.5
LANES = 128
QB = 128
QC = 256
SEL_LANES = 128
SB_UNDERFLOW = -104.0

_SLOPES = [2.0 ** (-8.0 * i / (H_NSA + H_DIL)) for i in range(1, H_NSA + H_DIL + 1)]
SLOPES_DIL = _SLOPES[:H_DIL]
SLOPES_NSA = _SLOPES[H_DIL:]

_IN_SEGS = (("qa", 2, "norm"), ("ksks", 1, "norm"), ("kwkw", 1, "norm"), ("dq", 3, "norm"), ("dk", 3, "norm"),
            ("kvc", 1, "raw"), ("vsvs", 1, "raw"), ("vwvw", 1, "raw"), ("gate", 1, "gate"),
            ("dv", 3, "raw"), ("sq", 3, "raw"), ("sk", 3, "raw"), ("sv", 3, "raw"))
_IN_COLS = sum(n for _, n, _ in _IN_SEGS) * LANES
_SLAB = 2 * LANES
_F32_SEGS = ("kvc", "dq", "dk", "dv")
_VMEM_LIMIT = 56 * 1024 * 1024


def _cparams(*sem, vmem=_VMEM_LIMIT):
    return pltpu.CompilerParams(dimension_semantics=sem, vmem_limit_bytes=vmem)


def _nt_dot(a, b):
    return lax.dot_general(a, b, (((1,), (1,)), ((), ())), preferred_element_type=F32)


def _dot(a, b):
    return jnp.dot(a, b, preferred_element_type=F32)


def _tn_dot(a, b):
    return lax.dot_general(a, b, (((0,), (0,)), ((), ())), preferred_element_type=F32)


def _split_dot(x, m):
    hi = x.astype(BF16)
    lo = (x - hi.astype(F32)).astype(BF16)
    return _dot(hi, m) + _dot(lo, m)


def _mask_half(x, hh):
    lane = lax.broadcasted_iota(jnp.int32, x.shape, x.ndim - 1)
    keep = (lane % LANES < HEAD_DIM) if hh == 0 else (lane % LANES >= HEAD_DIM)
    return jnp.where(keep, x, jnp.zeros_like(x))


def _in_proj_body(x_ref, nw_ref, w_ref, cv_ref, gm_ref, *out_refs):
    x = x_ref[...]
    ms = jnp.mean(x * x, axis=-1, keepdims=True)
    h = (x * lax.rsqrt(ms + RMS_EPS) * nw_ref[...]).astype(BF16)
    gm = gm_ref[...]
    tiles = [(o_ref, t, kind) for (name, ntile, kind), o_ref in zip(_IN_SEGS, out_refs) for t in range(ntile)]
    for sl in range(_IN_COLS // _SLAB):
        c0 = sl * _SLAB
        y = _nt_dot(h, w_ref[c0:c0 + _SLAB, :])
        if tiles[2 * sl][2] == "norm":
            msq = _dot((y * y).astype(BF16), gm)
            y = y * lax.rsqrt(msq + RMS_EPS)
        y = y * cv_ref[:, c0:c0 + _SLAB]
        for half in range(2):
            o_ref, t, kind = tiles[2 * sl + half]
            yh = y[:, half * LANES:(half + 1) * LANES]
            if kind == "gate":
                yh = jax.nn.sigmoid(yh)
            o_ref[:, t * LANES:(t + 1) * LANES] = yh.astype(o_ref.dtype)


def _in_proj(x2d, nw, w_re, cv, gm, tm=1024):
    n = x2d.shape[0]
    out_shape, out_specs = [], []
    for name, ntile, kind in _IN_SEGS:
        dt = F32 if kind == "gate" or name in _F32_SEGS else BF16
        out_shape.append(jax.ShapeDtypeStruct((n, ntile * LANES), dt))
        out_specs.append(pl.BlockSpec((tm, ntile * LANES), lambda i: (i, 0)))
    return pl.pallas_call(
        _in_proj_body,
        grid=(n // tm,),
        in_specs=[pl.BlockSpec((tm, D_MODEL), lambda i: (i, 0)),
                  pl.BlockSpec((1, D_MODEL), lambda i: (0, 0)),
                  pl.BlockSpec((_IN_COLS, D_MODEL), lambda i: (0, 0)),
                  pl.BlockSpec((1, _IN_COLS), lambda i: (0, 0)),
                  pl.BlockSpec((_SLAB, _SLAB), lambda i: (0, 0))],
        out_specs=out_specs,
        out_shape=out_shape,
        compiler_params=_cparams("parallel"),
        name="in_proj",
    )(x2d, nw, w_re, cv, gm)


def _gelu_tanh(x):
    return 0.5 * x * (1.0 + jnp.tanh(0.7978845608028654 * (x + 0.044715 * (x * x * x))))


def _cmp_body(x_ref, w1_ref, pe_ref, w2_ref, gk_ref, kc_ref, vc_ref):
    ncp = kc_ref.shape[1]
    pe = pe_ref[...].astype(BF16)
    x = jnp.concatenate([x_ref[0, pl.ds(l, ncp, stride=CMP_STRIDE), :].astype(BF16) for l in range(CMP_STRIDE)],
                        axis=1)
    outs = []
    for c in range(2):
        top = _dot(x, w1_ref[2 * c])
        bot = _dot(x, w1_ref[2 * c + 1])
        bias = _dot(pe, w1_ref[2 * c])[2 * c:2 * c + 1] + _dot(pe, w1_ref[2 * c + 1])[2 * c + 1:2 * c + 2]
        hid = top + pltpu.roll(bot, ncp - 1, 0) + bias
        outs.append(_dot(_gelu_tanh(hid).astype(BF16), w2_ref[c]))
    kc = outs[0]
    kc = kc * lax.rsqrt(jnp.mean(kc * kc, axis=-1, keepdims=True) + RMS_EPS) * gk_ref[...]
    kc_ref[0] = kc.astype(BF16)
    vc_ref[0] = outs[1].astype(BF16)


def _compress(kvc, w1i, pe8, w2r, gk):
    b, s, _ = kvc.shape
    ncp, wid = s // CMP_STRIDE, CMP_STRIDE * LANES
    return pl.pallas_call(
        _cmp_body,
        grid=(b,),
        in_specs=[pl.BlockSpec((1, s, LANES), lambda i: (i, 0, 0)),
                  pl.BlockSpec((4, wid, CMP_HIDDEN), lambda i: (0, 0, 0)),
                  pl.BlockSpec((8, wid), lambda i: (0, 0)),
                  pl.BlockSpec((2, CMP_HIDDEN, LANES), lambda i: (0, 0, 0)),
                  pl.BlockSpec((1, LANES), lambda i: (0, 0))],
        out_specs=[pl.BlockSpec((1, ncp, LANES), lambda i: (i, 0, 0)),
                   pl.BlockSpec((1, ncp, LANES), lambda i: (i, 0, 0))],
        out_shape=[jax.ShapeDtypeStruct((b, ncp, LANES), BF16)] * 2,
        compiler_params=_cparams("parallel"),
        name="nsa_compress",
    )(kvc, w1i, pe8, w2r, gk)


def _softmax_cols(s, mask):
    m = jnp.max(s, axis=0, keepdims=True)
    p = jnp.where(mask, jnp.exp2(s - m), 0.0)
    l = jnp.maximum(jnp.sum(p, axis=0, keepdims=True), 1e-30)
    return p, l


def _cw_body(q_ref, kc_ref, vc_ref, kw_ref, vw_ref, g_ref, ovt_ref, oa_ref, ns_ref, fl_ref):
    i = pl.program_id(1)
    t0 = i * QC
    q = q_ref[0]
    gt = g_ref[0].T
    kc = kc_ref[0]
    vc = vc_ref[0]
    ncp = kc.shape[0]
    wk = WIN_NSA + QC

    nq = H_NSA * QC
    heads = range(H_NSA)
    q4 = jnp.concatenate([_mask_half(q[:, (h // 2) * LANES:(h // 2 + 1) * LANES], h % 2) for h in heads], axis=0)
    head = lax.broadcasted_iota(jnp.int32, (1, nq), 1) // QC
    slope = jnp.zeros((1, nq), F32)
    for h in heads:
        slope = jnp.where(head == h, SLOPES_NSA[h] * LOG2E, slope)
    gate = lambda br: jnp.concatenate([gt[3 * h + br:3 * h + br + 1] for h in heads], axis=1)

    per_head = lambda a: jnp.concatenate([a] * H_NSA, axis=1)

    n_row = lax.broadcasted_iota(jnp.int32, (ncp, QC), 0)
    q_lane = lax.broadcasted_iota(jnp.int32, (ncp, QC), 1)
    vis = per_head((t0 - (CMP_LEN - 1)) + q_lane - CMP_STRIDE * n_row) >= 0
    cend = per_head((CMP_STRIDE * n_row + (CMP_LEN - 1)).astype(F32))

    start = pl.multiple_of(jnp.maximum(t0 - WIN_NSA, 0), QC)
    kw = kw_ref[0, pl.ds(start, wk), :]
    vw = vw_ref[0, pl.ds(start, wk), :]
    j_row = lax.broadcasted_iota(jnp.int32, (wk, QC), 0)
    r_lane = lax.broadcasted_iota(jnp.int32, (wk, QC), 1)
    dw = per_head((t0 - start) + r_lane - j_row)
    wmask = (dw >= 0) & (dw < WIN_NSA)
    kposw = per_head((start + j_row).astype(F32))

    vlane = lax.broadcasted_iota(jnp.int32, (1, LANES), 1) < HEAD_DIM
    vc = jnp.where(vlane, vc, jnp.ones((), BF16))
    vw = jnp.where(vlane, vw, jnp.ones((), BF16))
    has_key = vis[0:1, :].astype(F32)

    s = jnp.where(vis, _nt_dot(kc, q4) + slope * cend, NEG)
    p = jnp.exp2(s - jnp.max(s, axis=0, keepdims=True))
    pv = _tn_dot(vc, p.astype(BF16))
    rl = has_key / pv[HEAD_DIM:HEAD_DIM + 1]
    pn = p * rl
    psum = pn[:, 0:QC]
    for h in range(1, H_NSA):
        psum = psum + pn[:, h * QC:(h + 1) * QC]
    o = pv[:HEAD_DIM] * (rl * gate(0))
    s = jnp.where(wmask, _nt_dot(kw, q4) + slope * kposw, NEG)
    p = jnp.exp2(s - jnp.max(s, axis=0, keepdims=True))
    pv = _tn_dot(vw, p.astype(BF16))
    o = o + pv[:HEAD_DIM] * (gate(2) / pv[HEAD_DIM:HEAD_DIM + 1])
    for hp in range(H_NSA // 2):
        pair = jnp.concatenate([o[:, (2 * hp) * QC:(2 * hp + 1) * QC], o[:, (2 * hp + 1) * QC:(2 * hp + 2) * QC]], axis=0)
        oa_ref[0, :, hp * LANES:(hp + 1) * LANES] = pair.T

    hi = psum.astype(BF16)
    lo = (psum - hi.astype(F32)).astype(BF16)
    imp = _dot(ovt_ref[...], hi) + _dot(ovt_ref[...], lo)
    j = lax.broadcasted_iota(jnp.int32, (SEL_LANES, QC), 0)
    qi = lax.broadcasted_iota(jnp.int32, (SEL_LANES, QC), 1)
    cur = jnp.right_shift(t0 + qi, int(math.log2(SEL_BLOCK)))
    forced = (j == 0) | (j == cur) | (j == cur - 1)
    imp = jnp.where(forced, -3e38, jnp.where(j <= cur, imp, NEG))
    jf = j.astype(F32)
    notsel = jnp.where(forced, 0.0, 1.0)
    for _ in range(SEL_TOPK - 3):
        mx = jnp.max(imp, axis=0, keepdims=True)
        idx = jnp.min(jnp.where(imp == mx, jf, float(SEL_LANES)), axis=0, keepdims=True)
        hit = jf == idx
        notsel = jnp.where(hit, 0.0, notsel)
        imp = jnp.where(hit, -3e38, imp)
    nst = notsel.T
    ns_ref[0] = nst.astype(BF16)
    used = 1.0 - jnp.min(nst, axis=0, keepdims=True)
    fl_ref[0, 0] = jnp.broadcast_to(used, (8, SEL_LANES)).astype(jnp.int32)


def _nsa_cmp_win(qa, kcr, vcr, kwkw, vwvw, gates, ovt):
    b, s, _ = qa.shape
    ncp = kcr.shape[1]
    return pl.pallas_call(
        _cw_body,
        grid=(b, s // QC),
        in_specs=[pl.BlockSpec((1, QC, 2 * LANES), lambda bi, i: (bi, i, 0)),
                  pl.BlockSpec((1, ncp, LANES), lambda bi, i: (bi, 0, 0)),
                  pl.BlockSpec((1, ncp, LANES), lambda bi, i: (bi, 0, 0)),
                  pl.BlockSpec((1, s, LANES), lambda bi, i: (bi, 0, 0)),
                  pl.BlockSpec((1, s, LANES), lambda bi, i: (bi, 0, 0)),
                  pl.BlockSpec((1, QC, LANES), lambda bi, i: (bi, i, 0)),
                  pl.BlockSpec((SEL_LANES, ncp), lambda bi, i: (0, 0))],
        out_specs=[pl.BlockSpec((1, QC, 2 * LANES), lambda bi, i: (bi, i, 0)),
                   pl.BlockSpec((1, QC, SEL_LANES), lambda bi, i: (bi, i, 0)),
                   pl.BlockSpec((1, 1, 8, SEL_LANES), lambda bi, i: (bi, i, 0, 0))],
        out_shape=[jax.ShapeDtypeStruct((b, s, 2 * LANES), F32),
                   jax.ShapeDtypeStruct((b, s, SEL_LANES), BF16),
                   jax.ShapeDtypeStruct((b, s // QC, 8, SEL_LANES), jnp.int32)],
        compiler_params=_cparams("parallel", "parallel"),
        name="nsa_cmp_win",
    )(qa, kcr, vcr, kwkw, vwvw, gates, ovt)


def _sel_body(fl_ref, q_ref, ns_ref, g_ref, part_ref, ks_ref, vs_ref, kc_ref, sc_ref, oa_ref,
              kaug_ref, vaug_ref, qaug_ref, m_ref, acc_ref, sa_ref, sb_ref, tl_ref, *, tk, qs):
    i = pl.program_id(1)
    t0 = i * qs
    nq = H_NSA * qs
    last_tile = kaug_ref.shape[0] // tk - 1

    @pl.when(i == 0)
    def _():
        lane = lax.broadcasted_iota(jnp.int32, (kaug_ref.shape[0], LANES), 1)
        kaug_ref[:, 0:LANES] = jnp.where(lane < HEAD_DIM, ks_ref[0], kc_ref[:, 0:LANES])
        kaug_ref[:, LANES:2 * LANES] = kc_ref[:, LANES:2 * LANES]
        vaug_ref[...] = jnp.where(lane < HEAD_DIM, vs_ref[0], jnp.ones((), BF16))

    q = q_ref[0]
    ns = ns_ref[0]
    lane = lax.broadcasted_iota(jnp.int32, (qs, LANES), 1)
    for h in range(H_NSA):
        slab = q[:, (h // 2) * LANES:(h // 2 + 1) * LANES].astype(F32)
        if h % 2:
            slab = pltpu.roll(slab, HEAD_DIM, 1)
        qaug_ref[h * qs:(h + 1) * qs, 0:LANES] = jnp.where(lane < HEAD_DIM, slab, sc_ref[h:h + 1, :]).astype(BF16)
        qaug_ref[h * qs:(h + 1) * qs, LANES:2 * LANES] = ns
    m_ref[...] = jnp.full(m_ref.shape, NEG, F32)
    acc_ref[...] = jnp.zeros(acc_ref.shape, F32)

    key_row = lax.broadcasted_iota(jnp.int32, (tk, nq), 0)
    q_lane = lax.broadcasted_iota(jnp.int32, (tk, nq), 1)
    dmat = key_row - q_lane % qs

    def scores(jt):
        k0 = pl.multiple_of(jnp.minimum(jt, last_tile) * tk, tk)
        return _nt_dot(kaug_ref[pl.ds(k0, tk), :], qaug_ref[...])

    def update(s_ref, jt, diagonal):
        k0 = pl.multiple_of(jnp.minimum(jt, last_tile) * tk, tk)
        s = s_ref[...]
        if diagonal:
            s = jnp.where(dmat <= t0 - jt * tk, s, NEG)
        m_old = m_ref[...]
        m_new = jnp.maximum(m_old, jnp.max(s, axis=0, keepdims=True))
        alpha = jnp.exp2(m_old - m_new)
        p = jnp.exp2(s - m_new)
        m_ref[...] = m_new
        pv = _tn_dot(vaug_ref[pl.ds(k0, tk), :], p.astype(BF16))
        acc_ref[...] = alpha * acc_ref[...] + pv[:acc_ref.shape[0]]

    blocks_per_tile = tk // SEL_BLOCK
    diag_tile = t0 // tk
    flag_rows = qs // QC
    frow = (pl.program_id(0) * pl.num_programs(1) + i) * flag_rows

    def collect(jt, cnt):
        used = 0
        for r in range(flag_rows):
            for c in range(blocks_per_tile):
                used = used | fl_ref[frow + r, jt * blocks_per_tile + c]
        tl_ref[cnt] = jt
        return cnt + used

    n_before = lax.fori_loop(0, diag_tile, collect, 0)
    tl_ref[n_before] = diag_tile
    tl_ref[n_before + 1] = 2 * (last_tile + 1)

    def body(jj, carry):
        sb_ref[...] = scores(tl_ref[2 * jj + 1])
        update(sa_ref, tl_ref[2 * jj], False)
        sa_ref[...] = scores(tl_ref[2 * jj + 2])
        update(sb_ref, tl_ref[2 * jj + 1], False)
        return carry

    n_pairs = n_before // 2
    sa_ref[...] = scores(tl_ref[0])
    lax.fori_loop(0, n_pairs, body, 0)
    sb_ref[...] = scores(tl_ref[2 * n_pairs + 1])
    update(sa_ref, tl_ref[2 * n_pairs], True)
    update(sb_ref, tl_ref[2 * n_pairs + 1], True)

    gt = g_ref[0].T
    for hp in range(H_NSA // 2):
        rows = []
        for hh in range(2):
            h = 2 * hp + hh
            cs = slice(h * qs, (h + 1) * qs)
            rows.append(acc_ref[0:HEAD_DIM, cs] * (gt[3 * h + 1:3 * h + 2] / acc_ref[HEAD_DIM:HEAD_DIM + 1, cs]))
        cs = slice(hp * LANES, (hp + 1) * LANES)
        oa_ref[0, :, cs] = part_ref[0, :, cs] + jnp.concatenate(rows, axis=0).T


def _nsa_selected(flags, qa, nsel, gates, part, ksks, vsvs, kconst, sconst, tk=256, qs=256):
    b, s, _ = qa.shape
    assert tk % qs == 0 and qs % QC == 0
    nq = H_NSA * qs
    grid_spec = pltpu.PrefetchScalarGridSpec(
        num_scalar_prefetch=1,
        grid=(b, s // qs),
        in_specs=[pl.BlockSpec((1, qs, 2 * LANES), lambda bi, i, fl: (bi, i, 0)),
                  pl.BlockSpec((1, qs, SEL_LANES), lambda bi, i, fl: (bi, i, 0)),
                  pl.BlockSpec((1, qs, LANES), lambda bi, i, fl: (bi, i, 0)),
                  pl.BlockSpec((1, qs, 2 * LANES), lambda bi, i, fl: (bi, i, 0)),
                  pl.BlockSpec((1, s, LANES), lambda bi, i, fl: (bi, 0, 0)),
                  pl.BlockSpec((1, s, LANES), lambda bi, i, fl: (bi, 0, 0)),
                  pl.BlockSpec((s, 2 * LANES), lambda bi, i, fl: (0, 0)),
                  pl.BlockSpec((8, LANES), lambda bi, i, fl: (0, 0))],
        out_specs=pl.BlockSpec((1, qs, 2 * LANES), lambda bi, i, fl: (bi, i, 0)),
        scratch_shapes=[pltpu.VMEM((s, 2 * LANES), BF16),
                        pltpu.VMEM((s, LANES), BF16),
                        pltpu.VMEM((nq, 2 * LANES), BF16),
                        pltpu.VMEM((1, nq), F32),
                        pltpu.VMEM((HEAD_DIM + 8, nq), F32),
                        pltpu.VMEM((tk, nq), F32),
                        pltpu.VMEM((tk, nq), F32),
                        pltpu.SMEM((s // tk + 8,), jnp.int32)])
    return pl.pallas_call(
        functools.partial(_sel_body, tk=tk, qs=qs),
        grid_spec=grid_spec,
        out_shape=jax.ShapeDtypeStruct((b, s, 2 * LANES), F32),
        compiler_params=_cparams("arbitrary", "arbitrary"),
        name="nsa_selected",
    )(flags, qa, nsel, gates, part, ksks, vsvs, kconst, sconst)


def _dil_body(q_ref, kp_ref, kc_ref, vp_ref, vc_ref, bias_ref, o_ref, lse_ref, *, r, m):
    i = pl.program_id(1)
    key_row = lax.broadcasted_iota(jnp.int32, (2 * QB, QB), 0)
    first = key_row >= jnp.where(i > 0, 0, QB)

    def rows(c, u):
        return pl.ds(u * QB * r + c, QB, stride=r) if r > 1 else pl.ds(u * QB, QB)

    for c in range(r):
        for u in range(m):
            cur = rows(c, u)
            q = q_ref[0, cur, :].astype(BF16)
            if u > 0:
                k_prev, v_prev = kc_ref[0, rows(c, u - 1), :], vc_ref[0, rows(c, u - 1), :]
            else:
                k_prev, v_prev = kp_ref[0, rows(c, m - 1), :], vp_ref[0, rows(c, m - 1), :]
            kk = jnp.concatenate([k_prev, kc_ref[0, cur, :]], axis=0).astype(BF16)
            vv = jnp.concatenate([v_prev, vc_ref[0, cur, :]], axis=0).astype(BF16)
            outs, lses = [], []
            for hh in range(2):
                s = _nt_dot(kk, _mask_half(q, hh)) + bias_ref[hh]
                if u == 0:
                    s = jnp.where(first, s, NEG)
                mx = jnp.max(s, axis=0, keepdims=True)
                p = jnp.exp2(s - mx)
                l = jnp.sum(p, axis=0, keepdims=True)
                o = _tn_dot(vv, p.astype(BF16)) * (1.0 / l)
                outs.append(o[hh * HEAD_DIM:(hh + 1) * HEAD_DIM])
                lses.append(jnp.broadcast_to(mx + jnp.log2(l), (HEAD_DIM, QB)))
            o_ref[0, cur, :] = jnp.concatenate(outs, axis=0).T
            lse_ref[0, cur, :] = jnp.concatenate(lses, axis=0).T


def _dilated(dq, dk, dv, bias, g, r, span=2048):
    b, s, _ = dq.shape
    m = max(span // (QB * r), 1)
    span = m * QB * r
    cur = lambda bi, i: (bi, i, g)
    prev = lambda bi, i: (bi, jnp.maximum(i - 1, 0), g)
    return pl.pallas_call(
        functools.partial(_dil_body, r=r, m=m),
        grid=(b, s // span),
        in_specs=[pl.BlockSpec((1, span, LANES), cur),
                  pl.BlockSpec((1, span, LANES), prev), pl.BlockSpec((1, span, LANES), cur),
                  pl.BlockSpec((1, span, LANES), prev), pl.BlockSpec((1, span, LANES), cur),
                  pl.BlockSpec((2, 2 * QB, QB), lambda bi, i: (0, 0, 0))],
        out_specs=[pl.BlockSpec((1, span, LANES), lambda bi, i: (bi, i, 0))] * 2,
        out_shape=[jax.ShapeDtypeStruct((b, s, LANES), F32)] * 2,
        compiler_params=_cparams("parallel", "parallel"),
        name=f"dilated_r{r}",
    )(dq, dk, dk, dv, dv, bias)


def _dil_bias(g):
    w, r = DIL_PAIRS[g]
    assert w // r == QB
    iq = np.arange(QB)[None, :]
    jk = np.arange(2 * QB)[:, None]
    dist = iq + QB - jk
    out = np.empty((2, 2 * QB, QB), np.float32)
    for hh in range(2):
        slope = SLOPES_DIL[2 * g + hh]
        out[hh] = np.where((dist >= 0) & (dist <= QB), -slope * LOG2E * r * dist, NEG)
    return jnp.asarray(out)


def _sb_body(q_ref, k_ref, v_ref, lt_ref, o_ref, acc_ref):
    i = pl.program_id(2)
    tq = q_ref.shape[1]
    q = q_ref[0]
    lt = lt_ref[...]
    lane = lax.broadcasted_iota(jnp.int32, (tq, 2 * tq), 1)
    dmat = lax.broadcasted_iota(jnp.int32, (tq, 2 * tq), 0) - lane % tq
    q2 = jnp.concatenate([_mask_half(q, 0), _mask_half(q, 1)], axis=0)
    acc_ref[...] = jnp.zeros(acc_ref.shape, F32)

    def cond(c):
        jt, _, cmax = c
        return (jt >= 0) & (cmax > SB_UNDERFLOW * LOG2E)

    def body(c):
        jt, carry, _ = c
        pv = None
        for u in range(2):
            ju = jt - u
            k0 = pl.multiple_of(jnp.maximum(ju, 0) * tq, tq)
            kt = k_ref[0, pl.ds(k0, tq), :]
            vt = v_ref[0, pl.ds(k0, tq), :]
            z = _nt_dot(kt, q2)
            lb = jnp.minimum(z, 0.0) - jnp.log2(1.0 + jnp.exp2(-jnp.abs(z)))
            lf = lb - z
            if u == 0:
                mask = dmat < jnp.where(jt < i, tq, 0)
                lf = jnp.where(mask, lf, 0.0)
            else:
                vt = jnp.where(ju >= 0, vt, jnp.zeros_like(vt))
            agg = _dot(lt, lf.astype(BF16))
            a = jnp.exp2(lb + agg[:tq] + carry)
            if u == 0:
                a = jnp.where(mask, a, 0.0)
            pvu = _tn_dot(vt, a.astype(BF16))
            pv = pvu if pv is None else pv + pvu
            carry = carry + agg[tq:tq + 1]
        acc_ref[0] += pv[:HEAD_DIM, :tq]
        acc_ref[1] += pv[HEAD_DIM:, tq:]
        return jt - 2, carry, jnp.max(carry)

    lax.while_loop(cond, body, (i, jnp.zeros((1, 2 * tq), F32), jnp.float32(0.0)))
    o_ref[0] = jnp.concatenate([acc_ref[0], acc_ref[1]], axis=0).T


def _stick_breaking(sq, sk, sv, lt):
    b, s, _ = sq.shape
    npair = H_SB // 2
    tq = lt.shape[1]
    return pl.pallas_call(
        _sb_body,
        grid=(b, npair, s // tq),
        in_specs=[pl.BlockSpec((1, tq, LANES), lambda bi, hp, i: (bi, i, hp)),
                  pl.BlockSpec((1, s, LANES), lambda bi, hp, i: (bi, 0, hp)),
                  pl.BlockSpec((1, s, LANES), lambda bi, hp, i: (bi, 0, hp)),
                  pl.BlockSpec(lt.shape, lambda bi, hp, i: (0, 0))],
        out_specs=pl.BlockSpec((1, tq, LANES), lambda bi, hp, i: (bi, i, hp)),
        out_shape=jax.ShapeDtypeStruct((b, s, npair * LANES), F32),
        scratch_shapes=[pltpu.VMEM((2, HEAD_DIM, tq), F32)],
        compiler_params=_cparams("parallel", "parallel", "parallel"),
        name="stick_breaking",
    )(sq, sk, sv, lt)


def _out_body(x_ref, oa_ref, d0, l0, d1, l1, d2, l2, oc_ref, w_ref, o_ref):
    lses = [l0[...], l1[...], l2[...]]
    m = jnp.maximum(jnp.maximum(lses[0], lses[1]), lses[2])
    es = [jnp.exp2(l - m) for l in lses]
    den = es[0] + es[1] + es[2]
    ob = (es[0] * d0[...] + es[1] * d1[...] + es[2] * d2[...]) / den
    na = H_NSA * HEAD_DIM
    nb = na + LANES
    acc = _dot(oa_ref[...].astype(BF16), w_ref[0:na, :])
    acc = acc + _dot(ob.astype(BF16), w_ref[na:nb, :])
    acc = acc + _dot(oc_ref[...].astype(BF16), w_ref[nb:, :])
    o_ref[...] = x_ref[...] + acc


def _out_mlp_body(x_ref, oa_ref, d0, l0, d1, l1, d2, l2, oc_ref, wo_ref, nw_ref, wu_ref, wd_ref, o_ref, *, fc):
    _out_body(x_ref, oa_ref, d0, l0, d1, l1, d2, l2, oc_ref, wo_ref, o_ref)
    x = o_ref[...]
    ms = jnp.mean(x * x, axis=-1, keepdims=True)
    h = (x * lax.rsqrt(ms + RMS_EPS) * nw_ref[...]).astype(BF16)
    for c in range(D_FF // fc):
        u = jnp.maximum(_dot(h, wu_ref[:, c * fc:(c + 1) * fc]), 0.0)
        o_ref[...] += _dot((u * u).astype(BF16), wd_ref[c * fc:(c + 1) * fc, :])


def _out_mlp(x2d, oa, dil, oc, w_out, nw, wu, wd, tm=512, fc=1024):
    n = x2d.shape[0]
    row = lambda w: pl.BlockSpec((tm, w), lambda i: (i, 0))
    whole = lambda a: pl.BlockSpec(a.shape, lambda i: (0, 0))
    single = lambda a: pl.BlockSpec(a.shape, lambda i: (0, 0), pipeline_mode=pl.Buffered(1))
    ins = [x2d, oa]
    specs = [row(D_MODEL), row(2 * LANES)]
    for o, lse in dil:
        ins += [o, lse]
        specs += [row(LANES), row(LANES)]
    ins += [oc, w_out, nw, wu, wd]
    specs += [row(oc.shape[1]), whole(w_out), whole(nw), single(wu), single(wd)]
    return pl.pallas_call(
        functools.partial(_out_mlp_body, fc=fc),
        grid=(n // tm,),
        in_specs=specs,
        out_specs=row(D_MODEL),
        out_shape=jax.ShapeDtypeStruct((n, D_MODEL), F32),
        compiler_params=_cparams("parallel"),
        name="out_mlp",
    )(*ins)


def _relayout_in_weight(w, g_nsa, g_dil):
    hd = HEAD_DIM
    kv0 = H_NSA * hd
    g0 = kv0 + 6 * hd
    b0 = g0 + 3 * H_NSA
    c0 = b0 + 3 * H_DIL * hd
    w = w.T
    kv = lambda c: w[kv0 + c * hd:kv0 + (c + 1) * hd]
    gate = jnp.pad(w[g0:b0], ((0, LANES - 3 * H_NSA), (0, 0)))
    nd = H_DIL * hd
    w_re = jnp.concatenate([w[:kv0], kv(2), kv(2), kv(4), kv(4), w[b0:b0 + 2 * nd],
                            kv(0), kv(1), kv(3), kv(3), kv(5), kv(5), gate, w[b0 + 2 * nd:c0], w[c0:]],
                           axis=0).astype(BF16)
    one = lambda n: jnp.ones((n,), F32)
    cv = jnp.concatenate([
        jnp.tile(g_nsa[0], H_NSA) * (SCALE * LOG2E), jnp.tile(g_nsa[2], 2), jnp.tile(g_nsa[3], 2),
        jnp.tile(g_dil[0], H_DIL) * (SCALE * LOG2E), jnp.tile(g_dil[1], H_DIL),
        one(4 * LANES), one(nd), one(H_SB * hd) * (SCALE * LOG2E), one(2 * H_SB * hd)])
    return w_re, cv.reshape(1, _IN_COLS)


def _relayout_cmp(w1, pe, w2, gk):
    hd, half = HEAD_DIM, CMP_LEN // 2
    w1r = w1.reshape(2, 2, half, hd, CMP_HIDDEN)
    per = pe.reshape(2, 2, half, hd)
    w1i, pei = [], []
    for c in range(2):
        pad = ((0, 0), (0, 0), (0, hd), (0, 0)) if c == 0 else ((0, 0), (0, 0), (hd, 0), (0, 0))
        w1i.append(jnp.pad(w1r[c], pad).reshape(2, half * LANES, CMP_HIDDEN))
        pei.append(jnp.pad(per[c], pad[:3]).reshape(2, half * LANES))
    w1i = jnp.concatenate(w1i, axis=0).astype(BF16)
    pe8 = jnp.pad(jnp.concatenate(pei, axis=0), ((0, 4), (0, 0)))
    w2r = jnp.concatenate([w2, w2], axis=-1).astype(BF16)
    return w1i, pe8, w2r, jnp.tile(gk, 2).reshape(1, LANES)


def _constants(s):
    ncp = s // CMP_STRIDE
    n_cmp = (s - CMP_LEN) // CMP_STRIDE + 1
    n = np.arange(ncp)[:, None]
    j = np.arange(SEL_LANES)[None, :]
    ov = ((CMP_STRIDE * n <= SEL_BLOCK * j + SEL_BLOCK - 1) & (CMP_STRIDE * n + CMP_LEN - 1 >= SEL_BLOCK * j)
          & (n < n_cmp) & (j < s // SEL_BLOCK))
    pos = np.arange(s)
    kconst = np.zeros((s, 2 * LANES), np.float32)
    kconst[:, HEAD_DIM:HEAD_DIM + 3] = (pos // SEL_BLOCK * SEL_BLOCK)[:, None]
    kconst[:, HEAD_DIM + 3:HEAD_DIM + 6] = (pos % SEL_BLOCK)[:, None]
    kconst[:, LANES:] = np.where(pos[:, None] // SEL_BLOCK == j, NEG, 0.0)
    sconst = np.zeros((8, LANES), np.float32)
    for h in range(H_NSA):
        rest = np.float32(SLOPES_NSA[h] * LOG2E)
        for c in range(3):
            piece = rest.astype(BF16).astype(np.float32)
            sconst[h, HEAD_DIM + c] = sconst[h, HEAD_DIM + 3 + c] = piece
            rest = np.float32(rest - piece)
    a2 = np.arange(2 * QB)
    lt = np.concatenate([(a2[None, :] > a2[:, None]).astype(np.float32), np.ones((8, 2 * QB), np.float32)])
    a = np.arange(_SLAB)
    gm = (a[:, None] // HEAD_DIM == a[None, :] // HEAD_DIM).astype(np.float32) / HEAD_DIM
    return dict(ovt=jnp.asarray(ov.T, BF16), kconst=jnp.asarray(kconst, BF16), sconst=jnp.asarray(sconst),
                lt=jnp.asarray(lt, BF16), gm=jnp.asarray(gm, BF16),
                dil_bias=[_dil_bias(g) for g in range(len(DIL_PAIRS))])


def kernel(x, norm_mix, norm_mlp, w_in, qk_gain_nsa, qk_gain_dil, cmp_pe, cmp_w1, cmp_w2, w_out, w_up, w_down):
    b, s, d = x.shape
    assert d == D_MODEL and s % (DIL_PAIRS[-1][1] * QB) == 0 and s // SEL_BLOCK <= SEL_LANES
    assert s >= WIN_NSA + QC
    n = b * s
    cst = _constants(s)
    x2d = x.reshape(n, d)
    for l in range(w_in.shape[0]):
        w_re, cv = _relayout_in_weight(w_in[l], qk_gain_nsa[l], qk_gain_dil[l])
        qa, ksks, kwkw, dq, dk, kvc, vsvs, vwvw, gates, dv, sq, sk, sv = _in_proj(
            x2d, norm_mix[l].reshape(1, d), w_re, cv, cst["gm"])
        tok = lambda a: a.reshape(b, s, a.shape[-1])
        w1i, pe8, w2r, gk = _relayout_cmp(cmp_w1[l], cmp_pe[l], cmp_w2[l], qk_gain_nsa[l, 1])
        kcr, vcr = _compress(tok(kvc), w1i, pe8, w2r, gk)
        part, nsel, used = _nsa_cmp_win(tok(qa), kcr, vcr, tok(kwkw), tok(vwvw), tok(gates), cst["ovt"])
        oa = _nsa_selected(used[:, :, 0, :].reshape(n // QC, SEL_LANES), tok(qa), nsel, tok(gates), part,
                           tok(ksks), tok(vsvs), cst["kconst"], cst["sconst"])
        dil = []
        for g, (_, r) in enumerate(DIL_PAIRS):
            o, lse = _dilated(tok(dq), tok(dk), tok(dv), cst["dil_bias"][g], g, r)
            dil.append((o.reshape(n, LANES), lse.reshape(n, LANES)))
        oc = _stick_breaking(tok(sq), tok(sk), tok(sv), cst["lt"])
        x2d = _out_mlp(x2d, oa.reshape(n, 2 * LANES), dil, oc.reshape(n, -1), w_out[l].astype(BF16),
                       norm_mlp[l].reshape(1, d), w_up[l].astype(BF16), w_down[l].astype(BF16))
    return x2d.reshape(b, s, d)
```

```python
import functools
import math

import numpy as np
import jax
import jax.numpy as jnp
from jax import lax
from jax.experimental import pallas as pl
from jax.experimental.pallas import tpu as pltpu

F32 = jnp.float32
BF16 = jnp.bfloat16

D_MODEL = 1024
HEAD_DIM = 64
H_NSA = 4
H_DIL = 6
H_SB = 6
DIL_PAIRS = ((128, 1), (512, 4), (2048, 16))
CMP_LEN = 32
CMP_STRIDE = 16
CMP_HIDDEN = 128
SEL_BLOCK = 64
SEL_TOPK = 16
WIN_NSA = 512
D_FF = 4 * D_MODEL
RMS_EPS = 1e-6
NEG = -1e30
FORCE_BONUS = 1e4
LOG2E = 1.4426950408889634
SCALE = HEAD_DIM ** -0.5
LANES = 128
QB = 128
QC = 256
SEL_LANES = 128
SB_UNDERFLOW = -104.0

_SLOPES = [2.0 ** (-8.0 * i / (H_NSA + H_DIL)) for i in range(1, H_NSA + H_DIL + 1)]
SLOPES_DIL = _SLOPES[:H_DIL]
SLOPES_NSA = _SLOPES[H_DIL:]

_IN_SEGS = (("qa", 2, "norm"), ("ksks", 1, "norm"), ("kwkw", 1, "norm"), ("dq", 3, "norm"), ("dk", 3, "norm"),
            ("kvc", 1, "raw"), ("vsvs", 1, "raw"), ("vwvw", 1, "raw"), ("gate", 1, "gate"),
            ("dv", 3, "raw"), ("sq", 3, "raw"), ("sk", 3, "raw"), ("sv", 3, "raw"))
_IN_COLS = sum(n for _, n, _ in _IN_SEGS) * LANES
_SLAB = 2 * LANES
_F32_SEGS = ("kvc", "dq", "dk", "dv")
_VMEM_LIMIT = 56 * 1024 * 1024


def _cparams(*sem, vmem=_VMEM_LIMIT):
    return pltpu.CompilerParams(dimension_semantics=sem, vmem_limit_bytes=vmem)


def _nt_dot(a, b):
    return lax.dot_general(a, b, (((1,), (1,)), ((), ())), preferred_element_type=F32)


def _dot(a, b):
    return jnp.dot(a, b, preferred_element_type=F32)


def _tn_dot(a, b):
    return lax.dot_general(a, b, (((0,), (0,)), ((), ())), preferred_element_type=F32)


def _split_dot(x, m):
    hi = x.astype(BF16)
    lo = (x - hi.astype(F32)).astype(BF16)
    return _dot(hi, m) + _dot(lo, m)


def _mask_half(x, hh):
    lane = lax.broadcasted_iota(jnp.int32, x.shape, x.ndim - 1)
    keep = (lane % LANES < HEAD_DIM) if hh == 0 else (lane % LANES >= HEAD_DIM)
    return jnp.where(keep, x, jnp.zeros_like(x))


def _in_proj_body(x_ref, nw_ref, w_ref, cv_ref, gm_ref, *out_refs):
    x = x_ref[...]
    ms = jnp.mean(x * x, axis=-1, keepdims=True)
    h = (x * lax.rsqrt(ms + RMS_EPS) * nw_ref[...]).astype(BF16)
    gm = gm_ref[...]
    tiles = [(o_ref, t, kind) for (name, ntile, kind), o_ref in zip(_IN_SEGS, out_refs) for t in range(ntile)]
    for sl in range(_IN_COLS // _SLAB):
        c0 = sl * _SLAB
        y = _nt_dot(h, w_ref[c0:c0 + _SLAB, :])
        if tiles[2 * sl][2] == "norm":
            msq = _dot((y * y).astype(BF16), gm)
            y = y * lax.rsqrt(msq + RMS_EPS)
        y = y * cv_ref[:, c0:c0 + _SLAB]
        for half in range(2):
            o_ref, t, kind = tiles[2 * sl + half]
            yh = y[:, half * LANES:(half + 1) * LANES]
            if kind == "gate":
                yh = jax.nn.sigmoid(yh)
            o_ref[:, t * LANES:(t + 1) * LANES] = yh.astype(o_ref.dtype)


def _in_proj(x2d, nw, w_re, cv, gm, tm=1024):
    n = x2d.shape[0]
    out_shape, out_specs = [], []
    for name, ntile, kind in _IN_SEGS:
        dt = F32 if kind == "gate" or name in _F32_SEGS else BF16
        out_shape.append(jax.ShapeDtypeStruct((n, ntile * LANES), dt))
        out_specs.append(pl.BlockSpec((tm, ntile * LANES), lambda i: (i, 0)))
    return pl.pallas_call(
        _in_proj_body,
        grid=(n // tm,),
        in_specs=[pl.BlockSpec((tm, D_MODEL), lambda i: (i, 0)),
                  pl.BlockSpec((1, D_MODEL), lambda i: (0, 0)),
                  pl.BlockSpec((_IN_COLS, D_MODEL), lambda i: (0, 0)),
                  pl.BlockSpec((1, _IN_COLS), lambda i: (0, 0)),
                  pl.BlockSpec((_SLAB, _SLAB), lambda i: (0, 0))],
        out_specs=out_specs,
        out_shape=out_shape,
        compiler_params=_cparams("parallel"),
        name="in_proj",
    )(x2d, nw, w_re, cv, gm)


def _gelu_tanh(x):
    return 0.5 * x * (1.0 + jnp.tanh(0.7978845608028654 * (x + 0.044715 * (x * x * x))))


def _cmp_body(x_ref, w1_ref, pe_ref, w2_ref, gk_ref, kc_ref, vc_ref):
    ncp = kc_ref.shape[1]
    pe = pe_ref[...].astype(BF16)
    x = jnp.concatenate([x_ref[0, pl.ds(l, ncp, stride=CMP_STRIDE), :].astype(BF16) for l in range(CMP_STRIDE)],
                        axis=1)
    outs = []
    for c in range(2):
        top = _dot(x, w1_ref[2 * c])
        bot = _dot(x, w1_ref[2 * c + 1])
        bias = _dot(pe, w1_ref[2 * c])[2 * c:2 * c + 1] + _dot(pe, w1_ref[2 * c + 1])[2 * c + 1:2 * c + 2]
        hid = top + pltpu.roll(bot, ncp - 1, 0) + bias
        outs.append(_dot(_gelu_tanh(hid).astype(BF16), w2_ref[c]))
    kc = outs[0]
    kc = kc * lax.rsqrt(jnp.mean(kc * kc, axis=-1, keepdims=True) + RMS_EPS) * gk_ref[...]
    kc_ref[0] = kc.astype(BF16)
    vc_ref[0] = outs[1].astype(BF16)


def _compress(kvc, w1i, pe8, w2r, gk):
    b, s, _ = kvc.shape
    ncp, wid = s // CMP_STRIDE, CMP_STRIDE * LANES
    return pl.pallas_call(
        _cmp_body,
        grid=(b,),
        in_specs=[pl.BlockSpec((1, s, LANES), lambda i: (i, 0, 0)),
                  pl.BlockSpec((4, wid, CMP_HIDDEN), lambda i: (0, 0, 0)),
                  pl.BlockSpec((8, wid), lambda i: (0, 0)),
                  pl.BlockSpec((2, CMP_HIDDEN, LANES), lambda i: (0, 0, 0)),
                  pl.BlockSpec((1, LANES), lambda i: (0, 0))],
        out_specs=[pl.BlockSpec((1, ncp, LANES), lambda i: (i, 0, 0)),
                   pl.BlockSpec((1, ncp, LANES), lambda i: (i, 0, 0))],
        out_shape=[jax.ShapeDtypeStruct((b, ncp, LANES), BF16)] * 2,
        compiler_params=_cparams("parallel"),
        name="nsa_compress",
    )(kvc, w1i, pe8, w2r, gk)


def _softmax_cols(s, mask):
    m = jnp.max(s, axis=0, keepdims=True)
    p = jnp.where(mask, jnp.exp2(s - m), 0.0)
    l = jnp.maximum(jnp.sum(p, axis=0, keepdims=True), 1e-30)
    return p, l


def _cw_body(q_ref, kc_ref, vc_ref, kw_ref, vw_ref, g_ref, ovt_ref, oa_ref, ns_ref, fl_ref):
    i = pl.program_id(1)
    t0 = i * QC
    q = q_ref[0]
    gt = g_ref[0].T
    kc = kc_ref[0]
    vc = vc_ref[0]
    ncp = kc.shape[0]
    wk = WIN_NSA + QC

    pair_of = lambda a: jnp.concatenate([a, a], axis=1)

    n_row = lax.broadcasted_iota(jnp.int32, (ncp, QC), 0)
    q_lane = lax.broadcasted_iota(jnp.int32, (ncp, QC), 1)
    vis = pair_of((t0 - (CMP_LEN - 1)) + q_lane - CMP_STRIDE * n_row) >= 0
    cend = pair_of((CMP_STRIDE * n_row + (CMP_LEN - 1)).astype(F32))

    start = pl.multiple_of(jnp.maximum(t0 - WIN_NSA, 0), QC)
    kw = kw_ref[0, pl.ds(start, wk), :]
    vw = vw_ref[0, pl.ds(start, wk), :]
    j_row = lax.broadcasted_iota(jnp.int32, (wk, QC), 0)
    r_lane = lax.broadcasted_iota(jnp.int32, (wk, QC), 1)
    dw = pair_of((t0 - start) + r_lane - j_row)
    wmask = (dw >= 0) & (dw < WIN_NSA)
    kposw = pair_of((start + j_row).astype(F32))

    vlane = lax.broadcasted_iota(jnp.int32, (1, LANES), 1) < HEAD_DIM
    vc = jnp.where(vlane, vc, jnp.ones((), BF16))
    vw = jnp.where(vlane, vw, jnp.ones((), BF16))
    has_key = vis[0:1, :].astype(F32)
    second = lax.broadcasted_iota(jnp.int32, (1, 2 * QC), 1) >= QC

    psum = jnp.zeros((ncp, QC), F32)
    for hp in range(H_NSA // 2):
        h0, h1 = 2 * hp, 2 * hp + 1
        slab = q[:, hp * LANES:(hp + 1) * LANES]
        q2 = jnp.concatenate([_mask_half(slab, 0), _mask_half(slab, 1)], axis=0)
        slope = jnp.where(second, SLOPES_NSA[h1] * LOG2E, SLOPES_NSA[h0] * LOG2E)
        gate = lambda br: jnp.concatenate([gt[3 * h0 + br:3 * h0 + br + 1], gt[3 * h1 + br:3 * h1 + br + 1]], axis=1)
        s = jnp.where(vis, _nt_dot(kc, q2) + slope * cend, NEG)
        p = jnp.exp2(s - jnp.max(s, axis=0, keepdims=True))
        pv = _tn_dot(vc, p.astype(BF16))
        rl = has_key / pv[HEAD_DIM:HEAD_DIM + 1]
        pn = p * rl
        psum = psum + pn[:, :QC] + pn[:, QC:]
        o = pv[:HEAD_DIM] * (rl * gate(0))
        s = jnp.where(wmask, _nt_dot(kw, q2) + slope * kposw, NEG)
        p = jnp.exp2(s - jnp.max(s, axis=0, keepdims=True))
        pv = _tn_dot(vw, p.astype(BF16))
        o = o + pv[:HEAD_DIM] * (gate(2) / pv[HEAD_DIM:HEAD_DIM + 1])
        oa_ref[0, :, hp * LANES:(hp + 1) * LANES] = jnp.concatenate([o[:, :QC], o[:, QC:]], axis=0).T

    hi = psum.astype(BF16)
    lo = (psum - hi.astype(F32)).astype(BF16)
    imp = _dot(ovt_ref[...], hi) + _dot(ovt_ref[...], lo)
    j = lax.broadcasted_iota(jnp.int32, (SEL_LANES, QC), 0)
    qi = lax.broadcasted_iota(jnp.int32, (SEL_LANES, QC), 1)
    cur = jnp.right_shift(t0 + qi, int(math.log2(SEL_BLOCK)))
    forced = (j == 0) | (j == cur) | (j == cur - 1)
    imp = jnp.where(forced, -3e38, jnp.where(j <= cur, imp, NEG))
    jf = j.astype(F32)
    notsel = jnp.where(forced, 0.0, 1.0)
    for _ in range(SEL_TOPK - 3):
        mx = jnp.max(imp, axis=0, keepdims=True)
        idx = jnp.min(jnp.where(imp == mx, jf, float(SEL_LANES)), axis=0, keepdims=True)
        hit = jf == idx
        notsel = jnp.where(hit, 0.0, notsel)
        imp = jnp.where(hit, -3e38, imp)
    nst = notsel.T
    ns_ref[0] = nst.astype(BF16)
    used = 1.0 - jnp.min(nst, axis=0, keepdims=True)
    fl_ref[0, 0] = jnp.broadcast_to(used, (8, SEL_LANES)).astype(jnp.int32)


def _nsa_cmp_win(qa, kcr, vcr, kwkw, vwvw, gates, ovt):
    b, s, _ = qa.shape
    ncp = kcr.shape[1]
    return pl.pallas_call(
        _cw_body,
        grid=(b, s // QC),
        in_specs=[pl.BlockSpec((1, QC, 2 * LANES), lambda bi, i: (bi, i, 0)),
                  pl.BlockSpec((1, ncp, LANES), lambda bi, i: (bi, 0, 0)),
                  pl.BlockSpec((1, ncp, LANES), lambda bi, i: (bi, 0, 0)),
                  pl.BlockSpec((1, s, LANES), lambda bi, i: (bi, 0, 0)),
                  pl.BlockSpec((1, s, LANES), lambda bi, i: (bi, 0, 0)),
                  pl.BlockSpec((1, QC, LANES), lambda bi, i: (bi, i, 0)),
                  pl.BlockSpec((SEL_LANES, ncp), lambda bi, i: (0, 0))],
        out_specs=[pl.BlockSpec((1, QC, 2 * LANES), lambda bi, i: (bi, i, 0)),
                   pl.BlockSpec((1, QC, SEL_LANES), lambda bi, i: (bi, i, 0)),
                   pl.BlockSpec((1, 1, 8, SEL_LANES), lambda bi, i: (bi, i, 0, 0))],
        out_shape=[jax.ShapeDtypeStruct((b, s, 2 * LANES), F32),
                   jax.ShapeDtypeStruct((b, s, SEL_LANES), BF16),
                   jax.ShapeDtypeStruct((b, s // QC, 8, SEL_LANES), jnp.int32)],
        compiler_params=_cparams("parallel", "parallel"),
        name="nsa_cmp_win",
    )(qa, kcr, vcr, kwkw, vwvw, gates, ovt)


def _sel_body(fl_ref, q_ref, ns_ref, g_ref, part_ref, ks_ref, vs_ref, kc_ref, sc_ref, oa_ref,
              kaug_ref, vaug_ref, qaug_ref, m_ref, acc_ref, sa_ref, sb_ref, tl_ref, *, tk, qs):
    i = pl.program_id(1)
    t0 = i * qs
    nq = H_NSA * qs
    last_tile = kaug_ref.shape[0] // tk - 1

    @pl.when(i == 0)
    def _():
        lane = lax.broadcasted_iota(jnp.int32, (kaug_ref.shape[0], LANES), 1)
        kaug_ref[:, 0:LANES] = jnp.where(lane < HEAD_DIM, ks_ref[0], kc_ref[:, 0:LANES])
        kaug_ref[:, LANES:2 * LANES] = kc_ref[:, LANES:2 * LANES]
        vaug_ref[...] = jnp.where(lane < HEAD_DIM, vs_ref[0], jnp.ones((), BF16))

    q = q_ref[0]
    ns = ns_ref[0]
    lane = lax.broadcasted_iota(jnp.int32, (qs, LANES), 1)
    for h in range(H_NSA):
        slab = q[:, (h // 2) * LANES:(h // 2 + 1) * LANES].astype(F32)
        if h % 2:
            slab = pltpu.roll(slab, HEAD_DIM, 1)
        qaug_ref[h * qs:(h + 1) * qs, 0:LANES] = jnp.where(lane < HEAD_DIM, slab, sc_ref[h:h + 1, :]).astype(BF16)
        qaug_ref[h * qs:(h + 1) * qs, LANES:2 * LANES] = ns
    m_ref[...] = jnp.full(m_ref.shape, NEG, F32)
    acc_ref[...] = jnp.zeros(acc_ref.shape, F32)

    key_row = lax.broadcasted_iota(jnp.int32, (tk, nq), 0)
    q_lane = lax.broadcasted_iota(jnp.int32, (tk, nq), 1)
    dmat = key_row - q_lane % qs

    def scores(jt):
        k0 = pl.multiple_of(jnp.minimum(jt, last_tile) * tk, tk)
        return _nt_dot(kaug_ref[pl.ds(k0, tk), :], qaug_ref[...])

    def update(s_ref, jt, diagonal):
        k0 = pl.multiple_of(jnp.minimum(jt, last_tile) * tk, tk)
        s = s_ref[...]
        if diagonal:
            s = jnp.where(dmat <= t0 - jt * tk, s, NEG)
        m_old = m_ref[...]
        m_new = jnp.maximum(m_old, jnp.max(s, axis=0, keepdims=True))
        alpha = jnp.exp2(m_old - m_new)
        p = jnp.exp2(s - m_new)
        m_ref[...] = m_new
        pv = _tn_dot(vaug_ref[pl.ds(k0, tk), :], p.astype(BF16))
        acc_ref[...] = alpha * acc_ref[...] + pv[:acc_ref.shape[0]]

    blocks_per_tile = tk // SEL_BLOCK
    diag_tile = t0 // tk
    flag_rows = qs // QC
    frow = (pl.program_id(0) * pl.num_programs(1) + i) * flag_rows

    def collect(jt, cnt):
        used = 0
        for r in range(flag_rows):
            for c in range(blocks_per_tile):
                used = used | fl_ref[frow + r, jt * blocks_per_tile + c]
        tl_ref[cnt] = jt
        return cnt + used

    n_before = lax.fori_loop(0, diag_tile, collect, 0)
    tl_ref[n_before] = diag_tile
    tl_ref[n_before + 1] = 2 * (last_tile + 1)

    def body(jj, carry):
        sb_ref[...] = scores(tl_ref[2 * jj + 1])
        update(sa_ref, tl_ref[2 * jj], False)
        sa_ref[...] = scores(tl_ref[2 * jj + 2])
        update(sb_ref, tl_ref[2 * jj + 1], False)
        return carry

    n_pairs = n_before // 2
    sa_ref[...] = scores(tl_ref[0])
    lax.fori_loop(0, n_pairs, body, 0)
    sb_ref[...] = scores(tl_ref[2 * n_pairs + 1])
    update(sa_ref, tl_ref[2 * n_pairs], True)
    update(sb_ref, tl_ref[2 * n_pairs + 1], True)

    gt = g_ref[0].T
    for hp in range(H_NSA // 2):
        rows = []
        for hh in range(2):
            h = 2 * hp + hh
            cs = slice(h * qs, (h + 1) * qs)
            rows.append(acc_ref[0:HEAD_DIM, cs] * (gt[3 * h + 1:3 * h + 2] / acc_ref[HEAD_DIM:HEAD_DIM + 1, cs]))
        cs = slice(hp * LANES, (hp + 1) * LANES)
        oa_ref[0, :, cs] = part_ref[0, :, cs] + jnp.concatenate(rows, axis=0).T


def _nsa_selected(flags, qa, nsel, gates, part, ksks, vsvs, kconst, sconst, tk=256, qs=256):
    b, s, _ = qa.shape
    assert tk % qs == 0 and qs % QC == 0
    nq = H_NSA * qs
    grid_spec = pltpu.PrefetchScalarGridSpec(
        num_scalar_prefetch=1,
        grid=(b, s // qs),
        in_specs=[pl.BlockSpec((1, qs, 2 * LANES), lambda bi, i, fl: (bi, i, 0)),
                  pl.BlockSpec((1, qs, SEL_LANES), lambda bi, i, fl: (bi, i, 0)),
                  pl.BlockSpec((1, qs, LANES), lambda bi, i, fl: (bi, i, 0)),
                  pl.BlockSpec((1, qs, 2 * LANES), lambda bi, i, fl: (bi, i, 0)),
                  pl.BlockSpec((1, s, LANES), lambda bi, i, fl: (bi, 0, 0)),
                  pl.BlockSpec((1, s, LANES), lambda bi, i, fl: (bi, 0, 0)),
                  pl.BlockSpec((s, 2 * LANES), lambda bi, i, fl: (0, 0)),
                  pl.BlockSpec((8, LANES), lambda bi, i, fl: (0, 0))],
        out_specs=pl.BlockSpec((1, qs, 2 * LANES), lambda bi, i, fl: (bi, i, 0)),
        scratch_shapes=[pltpu.VMEM((s, 2 * LANES), BF16),
                        pltpu.VMEM((s, LANES), BF16),
                        pltpu.VMEM((nq, 2 * LANES), BF16),
                        pltpu.VMEM((1, nq), F32),
                        pltpu.VMEM((HEAD_DIM + 8, nq), F32),
                        pltpu.VMEM((tk, nq), F32),
                        pltpu.VMEM((tk, nq), F32),
                        pltpu.SMEM((s // tk + 8,), jnp.int32)])
    return pl.pallas_call(
        functools.partial(_sel_body, tk=tk, qs=qs),
        grid_spec=grid_spec,
        out_shape=jax.ShapeDtypeStruct((b, s, 2 * LANES), F32),
        compiler_params=_cparams("arbitrary", "arbitrary"),
        name="nsa_selected",
    )(flags, qa, nsel, gates, part, ksks, vsvs, kconst, sconst)


def _dil_body(q_ref, kp_ref, kc_ref, vp_ref, vc_ref, bias_ref, o_ref, lse_ref, *, r, m):
    i = pl.program_id(1)
    key_row = lax.broadcasted_iota(jnp.int32, (2 * QB, QB), 0)
    first = key_row >= jnp.where(i > 0, 0, QB)

    def rows(c, u):
        return pl.ds(u * QB * r + c, QB, stride=r) if r > 1 else pl.ds(u * QB, QB)

    for c in range(r):
        for u in range(m):
            cur = rows(c, u)
            q = q_ref[0, cur, :].astype(BF16)
            if u > 0:
                k_prev, v_prev = kc_ref[0, rows(c, u - 1), :], vc_ref[0, rows(c, u - 1), :]
            else:
                k_prev, v_prev = kp_ref[0, rows(c, m - 1), :], vp_ref[0, rows(c, m - 1), :]
            kk = jnp.concatenate([k_prev, kc_ref[0, cur, :]], axis=0).astype(BF16)
            vv = jnp.concatenate([v_prev, vc_ref[0, cur, :]], axis=0).astype(BF16)
            outs, lses = [], []
            for hh in range(2):
                s = _nt_dot(kk, _mask_half(q, hh)) + bias_ref[hh]
                if u == 0:
                    s = jnp.where(first, s, NEG)
                mx = jnp.max(s, axis=0, keepdims=True)
                p = jnp.exp2(s - mx)
                l = jnp.sum(p, axis=0, keepdims=True)
                o = _tn_dot(vv, p.astype(BF16)) * (1.0 / l)
                outs.append(o[hh * HEAD_DIM:(hh + 1) * HEAD_DIM])
                lses.append(jnp.broadcast_to(mx + jnp.log2(l), (HEAD_DIM, QB)))
            o_ref[0, cur, :] = jnp.concatenate(outs, axis=0).T
            lse_ref[0, cur, :] = jnp.concatenate(lses, axis=0).T


def _dilated(dq, dk, dv, bias, g, r, span=2048):
    b, s, _ = dq.shape
    m = max(span // (QB * r), 1)
    span = m * QB * r
    cur = lambda bi, i: (bi, i, g)
    prev = lambda bi, i: (bi, jnp.maximum(i - 1, 0), g)
    return pl.pallas_call(
        functools.partial(_dil_body, r=r, m=m),
        grid=(b, s // span),
        in_specs=[pl.BlockSpec((1, span, LANES), cur),
                  pl.BlockSpec((1, span, LANES), prev), pl.BlockSpec((1, span, LANES), cur),
                  pl.BlockSpec((1, span, LANES), prev), pl.BlockSpec((1, span, LANES), cur),
                  pl.BlockSpec((2, 2 * QB, QB), lambda bi, i: (0, 0, 0))],
        out_specs=[pl.BlockSpec((1, span, LANES), lambda bi, i: (bi, i, 0))] * 2,
        out_shape=[jax.ShapeDtypeStruct((b, s, LANES), F32)] * 2,
        compiler_params=_cparams("parallel", "parallel"),
        name=f"dilated_r{r}",
    )(dq, dk, dk, dv, dv, bias)


def _dil_bias(g):
    w, r = DIL_PAIRS[g]
    assert w // r == QB
    iq = np.arange(QB)[None, :]
    jk = np.arange(2 * QB)[:, None]
    dist = iq + QB - jk
    out = np.empty((2, 2 * QB, QB), np.float32)
    for hh in range(2):
        slope = SLOPES_DIL[2 * g + hh]
        out[hh] = np.where((dist >= 0) & (dist <= QB), -slope * LOG2E * r * dist, NEG)
    return jnp.asarray(out)


def _sb_body(q_ref, k_ref, v_ref, lt_ref, o_ref, acc_ref):
    i = pl.program_id(2)
    tq = q_ref.shape[1]
    q = q_ref[0]
    lt = lt_ref[...]
    lane = lax.broadcasted_iota(jnp.int32, (tq, 2 * tq), 1)
    dmat = lax.broadcasted_iota(jnp.int32, (tq, 2 * tq), 0) - lane % tq
    q2 = jnp.concatenate([_mask_half(q, 0), _mask_half(q, 1)], axis=0)
    acc_ref[...] = jnp.zeros(acc_ref.shape, F32)

    def cond(c):
        jt, _, cmax = c
        return (jt >= 0) & (cmax > SB_UNDERFLOW * LOG2E)

    def body(c):
        jt, carry, _ = c
        pv = None
        for u in range(2):
            ju = jt - u
            k0 = pl.multiple_of(jnp.maximum(ju, 0) * tq, tq)
            kt = k_ref[0, pl.ds(k0, tq), :]
            vt = v_ref[0, pl.ds(k0, tq), :]
            z = _nt_dot(kt, q2)
            lb = jnp.minimum(z, 0.0) - jnp.log2(1.0 + jnp.exp2(-jnp.abs(z)))
            lf = lb - z
            if u == 0:
                mask = dmat < jnp.where(jt < i, tq, 0)
                lf = jnp.where(mask, lf, 0.0)
            else:
                vt = jnp.where(ju >= 0, vt, jnp.zeros_like(vt))
            agg = _dot(lt, lf.astype(BF16))
            a = jnp.exp2(lb + agg[:tq] + carry)
            if u == 0:
                a = jnp.where(mask, a, 0.0)
            pvu = _tn_dot(vt, a.astype(BF16))
            pv = pvu if pv is None else pv + pvu
            carry = carry + agg[tq:tq + 1]
        acc_ref[0] += pv[:HEAD_DIM, :tq]
        acc_ref[1] += pv[HEAD_DIM:, tq:]
        return jt - 2, carry, jnp.max(carry)

    lax.while_loop(cond, body, (i, jnp.zeros((1, 2 * tq), F32), jnp.float32(0.0)))
    o_ref[0] = jnp.concatenate([acc_ref[0], acc_ref[1]], axis=0).T


def _stick_breaking(sq, sk, sv, lt):
    b, s, _ = sq.shape
    npair = H_SB // 2
    tq = lt.shape[1]
    return pl.pallas_call(
        _sb_body,
        grid=(b, npair, s // tq),
        in_specs=[pl.BlockSpec((1, tq, LANES), lambda bi, hp, i: (bi, i, hp)),
                  pl.BlockSpec((1, s, LANES), lambda bi, hp, i: (bi, 0, hp)),
                  pl.BlockSpec((1, s, LANES), lambda bi, hp, i: (bi, 0, hp)),
                  pl.BlockSpec(lt.shape, lambda bi, hp, i: (0, 0))],
        out_specs=pl.BlockSpec((1, tq, LANES), lambda bi, hp, i: (bi, i, hp)),
        out_shape=jax.ShapeDtypeStruct((b, s, npair * LANES), F32),
        scratch_shapes=[pltpu.VMEM((2, HEAD_DIM, tq), F32)],
        compiler_params=_cparams("parallel", "parallel", "parallel"),
        name="stick_breaking",
    )(sq, sk, sv, lt)


def _out_body(x_ref, oa_ref, d0, l0, d1, l1, d2, l2, oc_ref, w_ref, o_ref):
    lses = [l0[...], l1[...], l2[...]]
    m = jnp.maximum(jnp.maximum(lses[0], lses[1]), lses[2])
    es = [jnp.exp2(l - m) for l in lses]
    den = es[0] + es[1] + es[2]
    ob = (es[0] * d0[...] + es[1] * d1[...] + es[2] * d2[...]) / den
    na = H_NSA * HEAD_DIM
    nb = na + LANES
    acc = _dot(oa_ref[...].astype(BF16), w_ref[0:na, :])
    acc = acc + _dot(ob.astype(BF16), w_ref[na:nb, :])
    acc = acc + _dot(oc_ref[...].astype(BF16), w_ref[nb:, :])
    o_ref[...] = x_ref[...] + acc


def _out_mlp_body(x_ref, oa_ref, d0, l0, d1, l1, d2, l2, oc_ref, wo_ref, nw_ref, wu_ref, wd_ref, o_ref, *, fc):
    _out_body(x_ref, oa_ref, d0, l0, d1, l1, d2, l2, oc_ref, wo_ref, o_ref)
    x = o_ref[...]
    ms = jnp.mean(x * x, axis=-1, keepdims=True)
    h = (x * lax.rsqrt(ms + RMS_EPS) * nw_ref[...]).astype(BF16)
    for c in range(D_FF // fc):
        u = jnp.maximum(_dot(h, wu_ref[:, c * fc:(c + 1) * fc]), 0.0)
        o_ref[...] += _dot((u * u).astype(BF16), wd_ref[c * fc:(c + 1) * fc, :])


def _out_mlp(x2d, oa, dil, oc, w_out, nw, wu, wd, tm=512, fc=1024):
    n = x2d.shape[0]
    row = lambda w: pl.BlockSpec((tm, w), lambda i: (i, 0))
    whole = lambda a: pl.BlockSpec(a.shape, lambda i: (0, 0))
    single = lambda a: pl.BlockSpec(a.shape, lambda i: (0, 0), pipeline_mode=pl.Buffered(1))
    ins = [x2d, oa]
    specs = [row(D_MODEL), row(2 * LANES)]
    for o, lse in dil:
        ins += [o, lse]
        specs += [row(LANES), row(LANES)]
    ins += [oc, w_out, nw, wu, wd]
    specs += [row(oc.shape[1]), whole(w_out), whole(nw), single(wu), single(wd)]
    return pl.pallas_call(
        functools.partial(_out_mlp_body, fc=fc),
        grid=(n // tm,),
        in_specs=specs,
        out_specs=row(D_MODEL),
        out_shape=jax.ShapeDtypeStruct((n, D_MODEL), F32),
        compiler_params=_cparams("parallel"),
        name="out_mlp",
    )(*ins)


def _relayout_in_weight(w, g_nsa, g_dil):
    hd = HEAD_DIM
    kv0 = H_NSA * hd
    g0 = kv0 + 6 * hd
    b0 = g0 + 3 * H_NSA
    c0 = b0 + 3 * H_DIL * hd
    w = w.T
    kv = lambda c: w[kv0 + c * hd:kv0 + (c + 1) * hd]
    gate = jnp.pad(w[g0:b0], ((0, LANES - 3 * H_NSA), (0, 0)))
    nd = H_DIL * hd
    w_re = jnp.concatenate([w[:kv0], kv(2), kv(2), kv(4), kv(4), w[b0:b0 + 2 * nd],
                            kv(0), kv(1), kv(3), kv(3), kv(5), kv(5), gate, w[b0 + 2 * nd:c0], w[c0:]],
                           axis=0).astype(BF16)
    one = lambda n: jnp.ones((n,), F32)
    cv = jnp.concatenate([
        jnp.tile(g_nsa[0], H_NSA) * (SCALE * LOG2E), jnp.tile(g_nsa[2], 2), jnp.tile(g_nsa[3], 2),
        jnp.tile(g_dil[0], H_DIL) * (SCALE * LOG2E), jnp.tile(g_dil[1], H_DIL),
        one(4 * LANES), one(nd), one(H_SB * hd) * (SCALE * LOG2E), one(2 * H_SB * hd)])
    return w_re, cv.reshape(1, _IN_COLS)


def _relayout_cmp(w1, pe, w2, gk):
    hd, half = HEAD_DIM, CMP_LEN // 2
    w1r = w1.reshape(2, 2, half, hd, CMP_HIDDEN)
    per = pe.reshape(2, 2, half, hd)
    w1i, pei = [], []
    for c in range(2):
        pad = ((0, 0), (0, 0), (0, hd), (0, 0)) if c == 0 else ((0, 0), (0, 0), (hd, 0), (0, 0))
        w1i.append(jnp.pad(w1r[c], pad).reshape(2, half * LANES, CMP_HIDDEN))
        pei.append(jnp.pad(per[c], pad[:3]).reshape(2, half * LANES))
    w1i = jnp.concatenate(w1i, axis=0).astype(BF16)
    pe8 = jnp.pad(jnp.concatenate(pei, axis=0), ((0, 4), (0, 0)))
    w2r = jnp.concatenate([w2, w2], axis=-1).astype(BF16)
    return w1i, pe8, w2r, jnp.tile(gk, 2).reshape(1, LANES)


def _constants(s):
    ncp = s // CMP_STRIDE
    n_cmp = (s - CMP_LEN) // CMP_STRIDE + 1
    n = np.arange(ncp)[:, None]
    j = np.arange(SEL_LANES)[None, :]
    ov = ((CMP_STRIDE * n <= SEL_BLOCK * j + SEL_BLOCK - 1) & (CMP_STRIDE * n + CMP_LEN - 1 >= SEL_BLOCK * j)
          & (n < n_cmp) & (j < s // SEL_BLOCK))
    pos = np.arange(s)
    kconst = np.zeros((s, 2 * LANES), np.float32)
    kconst[:, HEAD_DIM:HEAD_DIM + 3] = (pos // SEL_BLOCK * SEL_BLOCK)[:, None]
    kconst[:, HEAD_DIM + 3:HEAD_DIM + 6] = (pos % SEL_BLOCK)[:, None]
    kconst[:, LANES:] = np.where(pos[:, None] // SEL_BLOCK == j, NEG, 0.0)
    sconst = np.zeros((8, LANES), np.float32)
    for h in range(H_NSA):
        rest = np.float32(SLOPES_NSA[h] * LOG2E)
        for c in range(3):
            piece = rest.astype(BF16).astype(np.float32)
            sconst[h, HEAD_DIM + c] = sconst[h, HEAD_DIM + 3 + c] = piece
            rest = np.float32(rest - piece)
    a2 = np.arange(2 * QB)
    lt = np.concatenate([(a2[None, :] > a2[:, None]).astype(np.float32), np.ones((8, 2 * QB), np.float32)])
    a = np.arange(_SLAB)
    gm = (a[:, None] // HEAD_DIM == a[None, :] // HEAD_DIM).astype(np.float32) / HEAD_DIM
    return dict(ovt=jnp.asarray(ov.T, BF16), kconst=jnp.asarray(kconst, BF16), sconst=jnp.asarray(sconst),
                lt=jnp.asarray(lt, BF16), gm=jnp.asarray(gm, BF16),
                dil_bias=[_dil_bias(g) for g in range(len(DIL_PAIRS))])


def kernel(x, norm_mix, norm_mlp, w_in, qk_gain_nsa, qk_gain_dil, cmp_pe, cmp_w1, cmp_w2, w_out, w_up, w_down):
    b, s, d = x.shape
    assert d == D_MODEL and s % (DIL_PAIRS[-1][1] * QB) == 0 and s // SEL_BLOCK <= SEL_LANES
    assert s >= WIN_NSA + QC
    n = b * s
    cst = _constants(s)
    x2d = x.reshape(n, d)
    for l in range(w_in.shape[0]):
        w_re, cv = _relayout_in_weight(w_in[l], qk_gain_nsa[l], qk_gain_dil[l])
        qa, ksks, kwkw, dq, dk, kvc, vsvs, vwvw, gates, dv, sq, sk, sv = _in_proj(
            x2d, norm_mix[l].reshape(1, d), w_re, cv, cst["gm"])
        tok = lambda a: a.reshape(b, s, a.shape[-1])
        w1i, pe8, w2r, gk = _relayout_cmp(cmp_w1[l], cmp_pe[l], cmp_w2[l], qk_gain_nsa[l, 1])
        kcr, vcr = _compress(tok(kvc), w1i, pe8, w2r, gk)
        part, nsel, used = _nsa_cmp_win(tok(qa), kcr, vcr, tok(kwkw), tok(vwvw), tok(gates), cst["ovt"])
        oa = _nsa_selected(used[:, :, 0, :].reshape(n // QC, SEL_LANES), tok(qa), nsel, tok(gates), part,
                           tok(ksks), tok(vsvs), cst["kconst"], cst["sconst"])
        dil = []
        for g, (_, r) in enumerate(DIL_PAIRS):
            o, lse = _dilated(tok(dq), tok(dk), tok(dv), cst["dil_bias"][g], g, r)
            dil.append((o.reshape(n, LANES), lse.reshape(n, LANES)))
        oc = _stick_breaking(tok(sq), tok(sk), tok(sv), cst["lt"])
        x2d = _out_mlp(x2d, oa.reshape(n, 2 * LANES), dil, oc.reshape(n, -1), w_out[l].astype(BF16),
                       norm_mlp[l].reshape(1, d), w_up[l].astype(BF16), w_down[l].astype(BF16))
    return x2d.reshape(b, s, d)
```
